```python
import math
import jax
import jax.numpy as jnp
from jax import lax
import numpy as np

D_MODEL = 2048
BATCH = 8
SEQ = 4096
DEPTH = 4

HEAD_DIM = 128
REL_HEADS = 8
A_HEADS = REL_HEADS
A_KV_HEADS = 2
A_WINDOW = 128
B_HEADS = 8
C_HEADS = REL_HEADS
C_KV_GROUPS = 2
CMP_BLOCK = 32
CMP_STRIDE = 16
CMP_HIDDEN = 256
SEL_BLOCK = 64
N_SELECT = 8
C_WINDOW = 512
D_HEADS = 8
D_CONV = 4
GDN_CHUNK = 64
NUM_BUCKETS = 32
MAX_DISTANCE = 128
D_FF = 11 * D_MODEL // 4
FFN_CONV = 3
Q_BLOCK = 128
EPS = 1e-6
NEG_INF = -1e30
FORCE_SCORE = 1e9
SCALE = HEAD_DIM ** -0.5
EVEN_WIDTH = (A_HEADS + B_HEADS) * HEAD_DIM
ODD_WIDTH = (C_HEADS + D_HEADS) * HEAD_DIM
EVEN_SPLITS = (A_HEADS * HEAD_DIM, A_KV_HEADS * HEAD_DIM, A_KV_HEADS * HEAD_DIM,
               B_HEADS * HEAD_DIM, B_HEADS * HEAD_DIM, B_HEADS * HEAD_DIM, B_HEADS)
ODD_SPLITS = ((C_HEADS * HEAD_DIM,) + (C_KV_GROUPS * HEAD_DIM,) * 6 + (3 * C_HEADS,)
              + (D_HEADS * HEAD_DIM,) * 3 + (D_HEADS, D_HEADS, D_HEADS * HEAD_DIM))
EVEN_COLS = sum(EVEN_SPLITS)
ODD_COLS = sum(ODD_SPLITS)

kernel_name = 'hybrid_swa_fox_nsa_gdn_convffn_trunk'


def rmsnorm(x, g):
    xf = x.astype(jnp.float32)
    y = xf * lax.rsqrt(jnp.mean(xf * xf, axis=-1, keepdims=True) + EPS)
    return (y * g.astype(jnp.float32)).astype(x.dtype)


def l2norm(x):
    return x * lax.rsqrt(jnp.sum(x * x, axis=-1, keepdims=True) + EPS)


def split_cols(x, sizes):
    cuts = [int(c) for c in np.cumsum(sizes)[:-1]]
    return jnp.split(x, cuts, axis=-1)


def causal_dwconv(x, w):
    width, t = w.shape[0], x.shape[1]
    xp = jnp.pad(x, ((0, 0), (width - 1, 0), (0, 0)))
    return sum(xp[:, j:j + t] * w[j] for j in range(width))


def masked_softmax(logits, mask):
    p = jax.nn.softmax(jnp.where(mask, logits, NEG_INF), axis=-1)
    return jnp.where(mask, p, 0.0)


def t5_bucket(dist):
    max_exact = NUM_BUCKETS // 2
    n = jnp.maximum(dist, 0)
    log_ratio = jnp.log(jnp.maximum(n, 1).astype(jnp.float32) / max_exact) / math.log(MAX_DISTANCE / max_exact)
    large = jnp.minimum(max_exact + (log_ratio * (NUM_BUCKETS - max_exact)).astype(jnp.int32), NUM_BUCKETS - 1)
    return jnp.where(n < max_exact, n, large)


def banded_blocks(x, window):
    bsz, t = x.shape[:2]
    nb, nprev = t // Q_BLOCK, window // Q_BLOCK
    xp = jnp.pad(x, ((0, 0), (window, 0), (0, 0), (0, 0))).reshape(bsz, nb + nprev, Q_BLOCK, *x.shape[2:])
    return jnp.concatenate([xp[:, s:s + nb] for s in range(nprev + 1)], axis=2)


def banded_mask_bias(rel_bias, window, nb):
    span = window + Q_BLOCK
    kl = jnp.arange(span)
    dist = jnp.arange(Q_BLOCK)[:, None] + window - kl[None, :]
    kpos = jnp.arange(nb)[:, None, None] * Q_BLOCK - window + kl
    mask = (dist >= 0) & (dist < window) & (kpos >= 0)
    bias = rel_bias[t5_bucket(dist)].transpose(2, 0, 1)
    return mask, bias


def swa_sink_attention(q, k, v, sinks, rel_bias):
    bsz, t, _ = q.shape
    nb = t // Q_BLOCK
    g, r = A_KV_HEADS, A_HEADS // A_KV_HEADS
    qb = q.reshape(bsz, nb, Q_BLOCK, g, r, HEAD_DIM)
    kb = banded_blocks(k.reshape(bsz, t, g, HEAD_DIM), A_WINDOW)
    vb = banded_blocks(v.reshape(bsz, t, g, HEAD_DIM), A_WINDOW)
    mask, bias = banded_mask_bias(rel_bias, A_WINDOW, nb)
    logits = (jnp.einsum('bnqgrd,bnkgd->bgrnqk', qb, kb).astype(jnp.float32) * SCALE
              + bias.reshape(g, r, 1, Q_BLOCK, -1).astype(jnp.float32))
    logits = jnp.where(mask, logits, NEG_INF)
    sink = sinks.astype(jnp.float32).reshape(g, r, 1, 1, 1)
    m = jnp.maximum(logits.max(axis=-1, keepdims=True), sink)
    e = jnp.where(mask, jnp.exp(logits - m), 0.0)
    p = e / (e.sum(axis=-1, keepdims=True) + jnp.exp(sink - m))
    o = jnp.einsum('bgrnqk,bnkgd->bnqgrd', p.astype(v.dtype), vb)
    return o.reshape(bsz, t, A_HEADS * HEAD_DIM)


def forgetting_attention(q, k, v, f_logit):
    bsz, t, _ = q.shape
    nb = t // Q_BLOCK
    heads = lambda a: a.reshape(bsz, t, B_HEADS, HEAD_DIM).transpose(0, 2, 1, 3)
    q, k, v = heads(q), heads(k), heads(v)
    c = jnp.cumsum(jax.nn.log_sigmoid(f_logit.astype(jnp.float32)), axis=1).transpose(0, 2, 1)
    qb = jnp.moveaxis(q.reshape(bsz, B_HEADS, nb, Q_BLOCK, HEAD_DIM), 2, 0)
    cb = jnp.moveaxis(c.reshape(bsz, B_HEADS, nb, Q_BLOCK), 2, 0)
    kpos = jnp.arange(t)

    def block(args):
        i, qi, ci = args
        qpos = i * Q_BLOCK + jnp.arange(Q_BLOCK)
        logits = (jnp.einsum('bhqd,bhkd->bhqk', qi, k).astype(jnp.float32) * SCALE
                  + ci[..., None] - c[:, :, None, :])
        p = masked_softmax(logits, kpos[None, :] <= qpos[:, None])
        return jnp.einsum('bhqk,bhkd->bhqd', p.astype(v.dtype), v)

    o = lax.map(block, (jnp.arange(nb), qb, cb))
    return o.transpose(1, 0, 3, 2, 4).reshape(bsz, t, B_HEADS * HEAD_DIM)


def compress_tokens(x, pe, w1, w2):
    bsz, t, g, d = x.shape
    ratio = CMP_BLOCK // CMP_STRIDE
    n_cmp = t // CMP_STRIDE - ratio + 1
    chunks = x.reshape(bsz, t // CMP_STRIDE, CMP_STRIDE, g, d)
    blocks = jnp.concatenate([chunks[:, m:m + n_cmp] for m in range(ratio)], axis=2) + pe[:, None, :]
    flat = blocks.transpose(0, 1, 3, 2, 4).reshape(bsz, n_cmp, g, CMP_BLOCK * d)
    return jax.nn.gelu(flat @ w1) @ w2


def nsa_attention(q, k_cmp, v_cmp, k_sel, v_sel, k_win, v_win, gate_logits, cmp_pos, cmp_w1, cmp_w2, rel_bias):
    bsz, t, _ = q.shape
    g, r = C_KV_GROUPS, C_HEADS // C_KV_GROUPS
    nb = t // Q_BLOCK
    kv = lambda a: a.reshape(bsz, t, g, HEAD_DIM)
    kc = compress_tokens(kv(k_cmp), cmp_pos[0], cmp_w1[0], cmp_w2[0])
    vc = compress_tokens(kv(v_cmp), cmp_pos[1], cmp_w1[1], cmp_w2[1])
    cmp_end = jnp.arange(kc.shape[1]) * CMP_STRIDE + CMP_BLOCK - 1
    n_sb = t // SEL_BLOCK
    n_sel = min(N_SELECT, n_sb)
    ks = kv(k_sel).reshape(bsz, n_sb, SEL_BLOCK, g, HEAD_DIM).transpose(0, 3, 1, 2, 4)
    vs = kv(v_sel).reshape(bsz, n_sb, SEL_BLOCK, g, HEAD_DIM).transpose(0, 3, 1, 2, 4)
    kw = banded_blocks(kv(k_win), C_WINDOW)
    vw = banded_blocks(kv(v_win), C_WINDOW)
    mask_w, bias_w = banded_mask_bias(rel_bias, C_WINDOW, nb)
    bias_w = bias_w.reshape(g, r, Q_BLOCK, -1).astype(jnp.float32)
    tab = rel_bias.T.reshape(g, r, NUM_BUCKETS).astype(jnp.float32)
    gates = jax.nn.sigmoid(gate_logits.astype(jnp.float32)).reshape(bsz, nb, Q_BLOCK, 3, g, r)
    qb = q.reshape(bsz, nb, Q_BLOCK, g, r, HEAD_DIM)
    b_ar = jnp.arange(bsz)[:, None, None, None]
    g_ar = jnp.arange(g)[None, :, None, None]
    ratio = CMP_BLOCK // CMP_STRIDE
    n_chunk = t // CMP_STRIDE
    sel_ids = jnp.arange(n_sb)

    def block(args):
        i, qi, kwi, vwi, mwi, gi = args
        qpos = i * Q_BLOCK + jnp.arange(Q_BLOCK)
        lc = jnp.einsum('bqgrd,bcgd->bgrqc', qi, kc).astype(jnp.float32) * SCALE
        pc = masked_softmax(lc, cmp_end[None, :] <= qpos[:, None])
        o_cmp = jnp.einsum('bgrqc,bcgd->bqgrd', pc.astype(vc.dtype), vc)
        imp = jnp.pad(pc.sum(axis=2), ((0, 0), (0, 0), (0, 0), (ratio - 1, ratio - 1)))
        chunk = sum(imp[..., m:m + n_chunk] for m in range(ratio))
        blk = chunk.reshape(bsz, g, Q_BLOCK, n_sb, SEL_BLOCK // CMP_STRIDE).sum(-1)
        cur = (qpos // SEL_BLOCK)[:, None]
        forced = (sel_ids == 0) | (sel_ids == cur) | (sel_ids == cur - 1)
        score = jnp.where(forced, FORCE_SCORE, jnp.where(sel_ids > cur, -FORCE_SCORE, blk))
        _, idx = lax.top_k(score, n_sel)
        k_g = ks[b_ar, g_ar, idx].reshape(bsz, g, Q_BLOCK, n_sel * SEL_BLOCK, HEAD_DIM)
        v_g = vs[b_ar, g_ar, idx].reshape(bsz, g, Q_BLOCK, n_sel * SEL_BLOCK, HEAD_DIM)
        kpos = (idx[..., None] * SEL_BLOCK + jnp.arange(SEL_BLOCK)).reshape(bsz, g, Q_BLOCK, -1)
        dist = qpos[:, None] - kpos
        bias_s = jnp.moveaxis(tab[g_ar, :, t5_bucket(dist)], -1, 2)
        ls = jnp.einsum('bqgrd,bgqkd->bgrqk', qi, k_g).astype(jnp.float32) * SCALE + bias_s
        ps = masked_softmax(ls, (dist >= 0)[:, :, None])
        o_sel = jnp.einsum('bgrqk,bgqkd->bqgrd', ps.astype(v_g.dtype), v_g)
        lw = jnp.einsum('bqgrd,bkgd->bgrqk', qi, kwi).astype(jnp.float32) * SCALE + bias_w
        pw = masked_softmax(lw, mwi)
        o_win = jnp.einsum('bgrqk,bkgd->bqgrd', pw.astype(vwi.dtype), vwi)
        out = (gi[:, :, 0, :, :, None] * o_cmp + gi[:, :, 1, :, :, None] * o_sel
               + gi[:, :, 2, :, :, None] * o_win)
        return out.astype(q.dtype)

    mv = lambda a: jnp.moveaxis(a, 1, 0)
    o = lax.map(block, (jnp.arange(nb), mv(qb), mv(kw), mv(vw), mask_w, mv(gates)))
    return jnp.moveaxis(o, 0, 1).reshape(bsz, t, C_HEADS * HEAD_DIM)


def gated_deltanet(q, k, v, beta_logit, a, z, conv_w, a_log, dt_bias, norm_g):
    dtype = q.dtype
    bsz, t, _ = q.shape
    h, d, c = D_HEADS, HEAD_DIM, GDN_CHUNK
    n = t // c
    f32 = jnp.float32
    qkv = jax.nn.silu(causal_dwconv(jnp.concatenate([q, k, v], axis=-1), conv_w)).astype(f32)
    q, k, v = jnp.split(qkv, 3, axis=-1)
    chunked = lambda x: x.reshape(bsz, n, c, h, d).transpose(0, 3, 1, 2, 4)
    q = l2norm(chunked(q)) * SCALE
    k = l2norm(chunked(k))
    v = chunked(v)
    beta = jax.nn.sigmoid(beta_logit.astype(f32)).reshape(bsz, n, c, h).transpose(0, 3, 1, 2)
    g = -jnp.exp(a_log.astype(f32)) * jax.nn.softplus(a.astype(f32) + dt_bias.astype(f32))
    gam = jnp.cumsum(g.reshape(bsz, n, c, h).transpose(0, 3, 1, 2), axis=-1)
    causal = jnp.tril(jnp.ones((c, c), bool))
    strict = jnp.tril(jnp.ones((c, c), bool), -1)
    diff = gam[..., :, None] - gam[..., None, :]
    decay = jnp.where(causal, jnp.exp(jnp.where(causal, diff, 0.0)), 0.0)
    k_beta = k * beta[..., None]
    m = jnp.eye(c, dtype=f32) + jnp.where(strict, jnp.einsum('bhnid,bhnjd->bhnij', k_beta, k) * decay, 0.0)
    u = lax.linalg.triangular_solve(m, v * beta[..., None], left_side=True, lower=True, unit_diagonal=True)
    w = lax.linalg.triangular_solve(m, k_beta * jnp.exp(gam)[..., None], left_side=True, lower=True, unit_diagonal=True)

    def step(state, inp):
        qc, kc, uc, wc, dc, gc = inp
        v_new = uc - jnp.einsum('bhid,bhde->bhie', wc, state)
        o = (jnp.einsum('bhid,bhde->bhie', qc * jnp.exp(gc)[..., None], state)
             + jnp.einsum('bhij,bhje->bhie', jnp.einsum('bhid,bhjd->bhij', qc, kc) * dc, v_new))
        g_last = gc[..., -1:]
        state = (state * jnp.exp(g_last)[..., None]
                 + jnp.einsum('bhjd,bhje->bhde', kc * jnp.exp(g_last - gc)[..., None], v_new))
        return state, o

    xs = tuple(jnp.moveaxis(arr, 2, 0) for arr in (q, k, u, w, decay, gam))
    _, o = lax.scan(step, jnp.zeros((bsz, h, d, d), f32), xs)
    o = o.transpose(1, 0, 3, 2, 4).reshape(bsz, t, h, d)
    o = rmsnorm(o, norm_g) * jax.nn.silu(z.astype(f32).reshape(bsz, t, h, d))
    return o.reshape(bsz, t, h * d).astype(dtype)


def even_mixer(h, w_in, b_forget, sinks, w_out, rel_bias):
    qa, ka, va, qb, kb, vb, f = split_cols(h @ w_in, EVEN_SPLITS)
    o_a = swa_sink_attention(qa, ka, va, sinks, rel_bias)
    o_b = forgetting_attention(qb, kb, vb, f + b_forget)
    return jnp.concatenate([o_a, o_b.astype(o_a.dtype)], axis=-1) @ w_out


def odd_mixer(h, w_in, cmp_pos, cmp_w1, cmp_w2, conv_w, a_log, dt_bias, gdn_norm, w_out, rel_bias):
    (qc, kcmp, vcmp, ksel, vsel, kwin, vwin, gates,
     qd, kd, vd, beta, a, z) = split_cols(h @ w_in, ODD_SPLITS)
    o_c = nsa_attention(qc, kcmp, vcmp, ksel, vsel, kwin, vwin, gates, cmp_pos, cmp_w1, cmp_w2, rel_bias)
    o_d = gated_deltanet(qd, kd, vd, beta, a, z, conv_w, a_log, dt_bias, gdn_norm)
    return jnp.concatenate([o_c, o_d.astype(o_c.dtype)], axis=-1) @ w_out


def conv_ffn(h, w_up, conv_w, conv_b, w_down):
    u, g = jnp.split(h @ w_up, 2, axis=-1)
    g = causal_dwconv(g, conv_w) + conv_b
    return (jax.nn.silu(g) * u) @ w_down


def setup_inputs(seed: int = 0) -> dict:
    key = jax.random.key(seed)
    ks = jax.random.split(key, 22)
    n_ev, n_od = (DEPTH + 1) // 2, DEPTH // 2
    f32 = jnp.float32

    def nrm(k, shape, scale):
        return scale * jax.random.normal(k, shape, f32)

    def gain(k, shape):
        return 1.0 + 0.02 * jax.random.normal(k, shape, f32)

    dt = jnp.exp(jax.random.uniform(ks[15], (n_od, D_HEADS), f32, math.log(1e-3), math.log(1e-1)))
    return {
        'x': nrm(ks[0], (BATCH, SEQ, D_MODEL), 1.0),
        'rel_bias': nrm(ks[1], (NUM_BUCKETS, REL_HEADS), 0.5),
        'norm_mix': gain(ks[2], (DEPTH, D_MODEL)),
        'norm_ffn': gain(ks[3], (DEPTH, D_MODEL)),
        'norm_final': gain(ks[4], (D_MODEL,)),
        'ev_w_in': nrm(ks[5], (n_ev, D_MODEL, EVEN_COLS), D_MODEL ** -0.5),
        'ev_b_forget': 2.0 + nrm(ks[6], (n_ev, B_HEADS), 0.5),
        'ev_sinks': nrm(ks[7], (n_ev, A_HEADS), 0.5),
        'ev_w_out': nrm(ks[8], (n_ev, EVEN_WIDTH, D_MODEL), EVEN_WIDTH ** -0.5),
        'od_w_in': nrm(ks[9], (n_od, D_MODEL, ODD_COLS), D_MODEL ** -0.5),
        'od_cmp_pos': nrm(ks[10], (n_od, 2, CMP_BLOCK, HEAD_DIM), 0.1),
        'od_cmp_w1': nrm(ks[11], (n_od, 2, CMP_BLOCK * HEAD_DIM, CMP_HIDDEN), (CMP_BLOCK * HEAD_DIM) ** -0.5),
        'od_cmp_w2': nrm(ks[12], (n_od, 2, CMP_HIDDEN, HEAD_DIM), CMP_HIDDEN ** -0.5),
        'od_conv_w': nrm(ks[13], (n_od, D_CONV, 3 * D_HEADS * HEAD_DIM), D_CONV ** -0.5),
        'od_a_log': jnp.log(jax.random.uniform(ks[14], (n_od, D_HEADS), f32, 1.0, 16.0)),
        'od_dt_bias': dt + jnp.log(-jnp.expm1(-dt)),
        'od_gdn_norm': gain(ks[16], (n_od, HEAD_DIM)),
        'od_w_out': nrm(ks[17], (n_od, ODD_WIDTH, D_MODEL), ODD_WIDTH ** -0.5),
        'ffn_w_up': nrm(ks[18], (DEPTH, D_MODEL, 2 * D_FF), D_MODEL ** -0.5),
        'ffn_conv_w': nrm(ks[19], (DEPTH, FFN_CONV, D_FF), FFN_CONV ** -0.5),
        'ffn_conv_b': nrm(ks[20], (DEPTH, D_FF), 0.02),
        'ffn_w_down': nrm(ks[21], (DEPTH, D_FF, D_MODEL), D_FF ** -0.5),
    }


def reference(x, rel_bias, norm_mix, norm_ffn, norm_final,
              ev_w_in, ev_b_forget, ev_sinks, ev_w_out,
              od_w_in, od_cmp_pos, od_cmp_w1, od_cmp_w2, od_conv_w, od_a_log, od_dt_bias, od_gdn_norm, od_w_out,
              ffn_w_up, ffn_conv_w, ffn_conv_b, ffn_w_down):
    h = x
    for layer in range(DEPTH):
        j = layer // 2
        hn = rmsnorm(h, norm_mix[layer])
        if layer % 2 == 0:
            mix = even_mixer(hn, ev_w_in[j], ev_b_forget[j], ev_sinks[j], ev_w_out[j], rel_bias)
        else:
            mix = odd_mixer(hn, od_w_in[j], od_cmp_pos[j], od_cmp_w1[j], od_cmp_w2[j], od_conv_w[j],
                            od_a_log[j], od_dt_bias[j], od_gdn_norm[j], od_w_out[j], rel_bias)
        h = h + mix.astype(h.dtype)
        h = h + conv_ffn(rmsnorm(h, norm_ffn[layer]), ffn_w_up[layer], ffn_conv_w[layer],
                         ffn_conv_b[layer], ffn_w_down[layer]).astype(h.dtype)
    return rmsnorm(h, norm_final)
```

```python
import functools
import math

import jax
import jax.numpy as jnp
import numpy as np
from jax import lax
from jax.experimental import pallas as pl
from jax.experimental.pallas import tpu as pltpu

D_MODEL = 2048
DEPTH = 4
HEAD_DIM = 128
A_HEADS = 8
A_KV_HEADS = 2
A_WINDOW = 128
B_HEADS = 8
C_HEADS = 8
C_KV_GROUPS = 2
CMP_BLOCK = 32
CMP_STRIDE = 16
CMP_HIDDEN = 256
SEL_BLOCK = 64
N_SELECT = 8
C_WINDOW = 512
D_HEADS = 8
D_CONV = 4
GDN_CHUNK = 64
NUM_BUCKETS = 32
MAX_DISTANCE = 128
D_FF = 11 * D_MODEL // 4
FFN_CONV = 3
Q_BLOCK = 128
EPS = 1e-6
NEG_INF = -1e30
FORCE_SCORE = 1e9
SCALE = HEAD_DIM ** -0.5

F32 = jnp.float32
BF16 = jnp.bfloat16
LANES = 128
HALO = 16
VMEM_LIMIT = 52 * 1024 * 1024


def _cp(dims, vmem=VMEM_LIMIT):
    return pltpu.CompilerParams(dimension_semantics=dims, vmem_limit_bytes=vmem)


def _dot(a, b):
    return jnp.dot(a, b, preferred_element_type=F32)


def _dot_nt(a, b):
    return lax.dot_general(a, b, (((1,), (1,)), ((), ())), preferred_element_type=F32)


def _rms(x, g):
    return x * lax.rsqrt(jnp.mean(x * x, axis=-1, keepdims=True) + EPS) * g


def _sigmoid(x):
    return 1.0 / (1.0 + jnp.exp(-x))


def _silu(x):
    return x * _sigmoid(x)


def _softplus(x):
    return jnp.maximum(x, 0.0) + jnp.log1p(jnp.exp(-jnp.abs(x)))


def _norm_matmul_kernel(x_ref, g_ref, w_ref, o_ref, xn_ref):
    @pl.when(pl.program_id(1) == 0)
    def _():
        xn_ref[...] = _rms(x_ref[...], g_ref[...]).astype(BF16)

    o_ref[...] = _dot(xn_ref[...], w_ref[...]).astype(o_ref.dtype)


def norm_matmul(x, g, w, out_dtype, tm=512, tn=512):
    m, k = x.shape
    n = w.shape[1]
    tn = min(tn, n)
    assert m % tm == 0 and n % tn == 0
    return pl.pallas_call(
        _norm_matmul_kernel,
        grid=(m // tm, n // tn),
        in_specs=[pl.BlockSpec((tm, k), lambda i, j: (i, 0)),
                  pl.BlockSpec((1, k), lambda i, j: (0, 0)),
                  pl.BlockSpec((k, tn), lambda i, j: (0, j))],
        out_specs=pl.BlockSpec((tm, tn), lambda i, j: (i, j)),
        out_shape=jax.ShapeDtypeStruct((m, n), out_dtype),
        scratch_shapes=[pltpu.VMEM((tm, k), BF16)],
        compiler_params=_cp(("parallel", "arbitrary")),
        name="norm_matmul",
    )(x, g.reshape(1, k), w)


def _out_proj_kernel(a1_ref, a2_ref, w1_ref, w2_ref, h_ref, o_ref):
    o_ref[...] = h_ref[...] + _dot(a1_ref[...], w1_ref[...]) + _dot(a2_ref[...], w2_ref[...])


def out_proj(a1, a2, w, h, tm=512):
    m, k1 = a1.shape
    k2 = a2.shape[1]
    n = w.shape[1]
    w1, w2 = w[:k1], w[k1:]
    return pl.pallas_call(
        _out_proj_kernel,
        grid=(m // tm,),
        in_specs=[pl.BlockSpec((tm, k1), lambda i: (i, 0)),
                  pl.BlockSpec((tm, k2), lambda i: (i, 0)),
                  pl.BlockSpec((k1, n), lambda i: (0, 0)),
                  pl.BlockSpec((k2, n), lambda i: (0, 0)),
                  pl.BlockSpec((tm, n), lambda i: (i, 0))],
        out_specs=pl.BlockSpec((tm, n), lambda i: (i, 0)),
        out_shape=jax.ShapeDtypeStruct((m, n), F32),
        compiler_params=_cp(("parallel",)),
        name="out_proj",
    )(a1, a2, w1, w2, h)


def _ffn_kernel(seq_tiles, h_ref, hp_ref, g_ref, wu_ref, wg_ref, cw_ref, cb_ref, wd_ref, o_ref, xn_ref):
    i = pl.program_id(0)
    j = pl.program_id(1)
    tm = h_ref.shape[0]

    @pl.when(j == 0)
    def _():
        keep = (i % seq_tiles != 0).astype(F32)
        xn_ref[0:HALO, :] = (_rms(hp_ref[...], g_ref[...]) * keep).astype(BF16)
        xn_ref[HALO:, :] = _rms(h_ref[...], g_ref[...]).astype(BF16)
        o_ref[...] = h_ref[...]

    u = _dot(xn_ref[HALO:, :], wu_ref[...])
    ge = _dot(xn_ref[...], wg_ref[...])
    cw = cw_ref[...]
    gc = (cw[0:1] * ge[HALO - 2:HALO - 2 + tm] + cw[1:2] * ge[HALO - 1:HALO - 1 + tm]
          + cw[2:3] * ge[HALO:] + cb_ref[...])
    act = _silu(gc) * u
    o_ref[...] += _dot(act.astype(BF16), wd_ref[...])


def conv_ffn(h, g, w_up_u, w_up_g, conv_w, conv_b, w_down, seq, tm=512, tf=512):
    m, k = h.shape
    dff = w_up_u.shape[1]
    assert m % tm == 0 and dff % tf == 0 and seq % tm == 0 and tm % HALO == 0
    hb = tm // HALO
    return pl.pallas_call(
        functools.partial(_ffn_kernel, seq // tm),
        grid=(m // tm, dff // tf),
        in_specs=[pl.BlockSpec((tm, k), lambda i, j: (i, 0), pipeline_mode=pl.Buffered(1)),
                  pl.BlockSpec((HALO, k), lambda i, j: (jnp.maximum(i * hb - 1, 0), 0)),
                  pl.BlockSpec((1, k), lambda i, j: (0, 0)),
                  pl.BlockSpec((k, tf), lambda i, j: (0, j)),
                  pl.BlockSpec((k, tf), lambda i, j: (0, j)),
                  pl.BlockSpec((FFN_CONV, tf), lambda i, j: (0, j)),
                  pl.BlockSpec((1, tf), lambda i, j: (0, j)),
                  pl.BlockSpec((tf, k), lambda i, j: (j, 0))],
        out_specs=pl.BlockSpec((tm, k), lambda i, j: (i, 0)),
        out_shape=jax.ShapeDtypeStruct((m, k), F32),
        scratch_shapes=[pltpu.VMEM((HALO + tm, k), BF16)],
        compiler_params=_cp(("parallel", "arbitrary")),
        name="conv_ffn",
    )(h, h, g.reshape(1, k), w_up_u, w_up_g, conv_w, conv_b.reshape(1, dff), w_down)


def _final_norm_kernel(x_ref, g_ref, o_ref):
    o_ref[...] = _rms(x_ref[...], g_ref[...])


def final_norm(x, g, tm=1024):
    m, k = x.shape
    return pl.pallas_call(
        _final_norm_kernel,
        grid=(m // tm,),
        in_specs=[pl.BlockSpec((tm, k), lambda i: (i, 0)), pl.BlockSpec((1, k), lambda i: (0, 0))],
        out_specs=pl.BlockSpec((tm, k), lambda i: (i, 0)),
        out_shape=jax.ShapeDtypeStruct((m, k), F32),
        compiler_params=_cp(("parallel",)),
        name="final_norm",
    )(x, g.reshape(1, k))


def _t5_bucket(dist):
    max_exact = NUM_BUCKETS // 2
    n = jnp.maximum(dist, 0)
    log_ratio = jnp.log(jnp.maximum(n, 1).astype(F32) / max_exact) / math.log(MAX_DISTANCE / max_exact)
    large = jnp.minimum(max_exact + (log_ratio * (NUM_BUCKETS - max_exact)).astype(jnp.int32), NUM_BUCKETS - 1)
    return jnp.where(n < max_exact, n, large)


def _bias_table(rel_bias, dist, mask):
    b = rel_bias.astype(F32)[_t5_bucket(dist)].transpose(2, 0, 1)
    return jnp.where(mask[None], b, NEG_INF)


def _swa_kernel(sink_ref, q_ref, kp_ref, kc_ref, vp_ref, vc_ref, bias_ref, o_ref):
    g = pl.program_id(1)
    i = pl.program_id(2)
    r = A_HEADS // A_KV_HEADS
    k = jnp.concatenate([kp_ref[0], kc_ref[0]], axis=0)
    v = jnp.concatenate([vp_ref[0], vc_ref[0]], axis=0)
    col = lax.broadcasted_iota(jnp.int32, (Q_BLOCK, 2 * Q_BLOCK), 1)
    first = jnp.logical_and(i == 0, col < Q_BLOCK)
    for hh in range(r):
        qh = q_ref[0, :, hh * HEAD_DIM:(hh + 1) * HEAD_DIM]
        logits = _dot_nt(qh, k) * SCALE + bias_ref[0, hh]
        logits = jnp.where(first, NEG_INF, logits)
        sink = sink_ref[g * r + hh]
        m = jnp.maximum(jnp.max(logits, axis=-1, keepdims=True), sink)
        e = jnp.exp(logits - m)
        p = e / (jnp.sum(e, axis=-1, keepdims=True) + jnp.exp(sink - m))
        o_ref[0, :, hh * HEAD_DIM:(hh + 1) * HEAD_DIM] = _dot(p.astype(BF16), v).astype(o_ref.dtype)


def swa_attention(big, sinks, rel_bias, bsz, t):
    nb = t // Q_BLOCK
    r = A_HEADS // A_KV_HEADS
    ql = jnp.arange(Q_BLOCK)[:, None]
    kl = jnp.arange(2 * Q_BLOCK)[None, :]
    dist = ql + A_WINDOW - kl
    table = _bias_table(rel_bias, dist, (dist >= 0) & (dist < A_WINDOW))
    table = table.reshape(A_KV_HEADS, r, Q_BLOCK, 2 * Q_BLOCK)
    kcol = A_HEADS * HEAD_DIM // LANES
    vcol = kcol + A_KV_HEADS
    prev = lambda i: jnp.maximum(i - 1, 0)
    return pl.pallas_call(
        _swa_kernel,
        grid=(bsz, A_KV_HEADS, nb),
        in_specs=[pl.BlockSpec(memory_space=pltpu.SMEM),
                  pl.BlockSpec((1, Q_BLOCK, r * HEAD_DIM), lambda b, g, i: (b, i, g)),
                  pl.BlockSpec((1, Q_BLOCK, HEAD_DIM), lambda b, g, i: (b, prev(i), kcol + g)),
                  pl.BlockSpec((1, Q_BLOCK, HEAD_DIM), lambda b, g, i: (b, i, kcol + g)),
                  pl.BlockSpec((1, Q_BLOCK, HEAD_DIM), lambda b, g, i: (b, prev(i), vcol + g)),
                  pl.BlockSpec((1, Q_BLOCK, HEAD_DIM), lambda b, g, i: (b, i, vcol + g)),
                  pl.BlockSpec((1, r, Q_BLOCK, 2 * Q_BLOCK), lambda b, g, i: (g, 0, 0, 0))],
        out_specs=pl.BlockSpec((1, Q_BLOCK, r * HEAD_DIM), lambda b, g, i: (b, i, g)),
        out_shape=jax.ShapeDtypeStruct((bsz, t, A_HEADS * HEAD_DIM), BF16),
        compiler_params=_cp(("parallel", "parallel", "arbitrary")),
        name="swa_attention",
    )(sinks.astype(F32), big, big, big, big, big, table)


def _forget_cumsum_kernel(f_ref, b_ref, c_ref):
    x = f_ref[0] + b_ref[...]
    y = jnp.minimum(x, 0.0) - jnp.log1p(jnp.exp(-jnp.abs(x)))
    t = y.shape[1]
    lane = lax.broadcasted_iota(jnp.int32, y.shape, 1)
    s = 1
    while s < t:
        y = y + jnp.where(lane >= s, pltpu.roll(y, s, 1), 0.0)
        s *= 2
    c_ref[0] = y


def forget_cumsum(f_t, b_forget):
    bsz, h, t = f_t.shape
    return pl.pallas_call(
        _forget_cumsum_kernel,
        grid=(bsz,),
        in_specs=[pl.BlockSpec((1, h, t), lambda b: (b, 0, 0)), pl.BlockSpec((h, 1), lambda b: (0, 0))],
        out_specs=pl.BlockSpec((1, h, t), lambda b: (b, 0, 0)),
        out_shape=jax.ShapeDtypeStruct((bsz, h, t), F32),
        compiler_params=_cp(("parallel",)),
        name="forget_cumsum",
    )(f_t, b_forget.astype(F32).reshape(h, 1))


def _softmax_step(s, v, m, l, acc):
    m_new = jnp.maximum(m, jnp.max(s, axis=-1, keepdims=True))
    alpha = jnp.exp(m - m_new)
    p = jnp.exp(s - m_new)
    l = alpha * l + jnp.sum(p, axis=-1, keepdims=True)
    acc = alpha * acc + _dot(p.astype(BF16), v)
    return m_new, l, acc


def _fox_kernel(tq, q_ref, k_ref, v_ref, c_ref, o_ref):
    i = pl.program_id(2)
    q = q_ref[0]

    def tile(j, carry, diag):
        off = pl.multiple_of(j * tq, tq)
        k = k_ref[0, pl.ds(off, tq), :]
        v = v_ref[0, pl.ds(off, tq), :]
        s = _dot_nt(q, k) * SCALE - c_ref[0, :, pl.ds(off, tq)]
        if diag:
            row = lax.broadcasted_iota(jnp.int32, (tq, tq), 0)
            col = lax.broadcasted_iota(jnp.int32, (tq, tq), 1)
            s = jnp.where(col <= row, s, NEG_INF)
        return _softmax_step(s, v, *carry)

    init = (jnp.full((tq, 1), NEG_INF, F32), jnp.zeros((tq, 1), F32), jnp.zeros((tq, HEAD_DIM), F32))
    carry = tile(i, init, True)
    m, l, acc = lax.fori_loop(0, i, lambda j, c: tile(j, c, False), carry)
    o_ref[0] = (acc / l).astype(o_ref.dtype)


def fox_attention(big, c, bsz, t, tq=512):
    qcol = (A_HEADS + 2 * A_KV_HEADS) * HEAD_DIM // LANES
    kcol = qcol + B_HEADS
    vcol = kcol + B_HEADS
    return pl.pallas_call(
        functools.partial(_fox_kernel, tq),
        grid=(bsz, B_HEADS, t // tq),
        in_specs=[pl.BlockSpec((1, tq, HEAD_DIM), lambda b, h, i: (b, i, qcol + h)),
                  pl.BlockSpec((1, t, HEAD_DIM), lambda b, h, i: (b, 0, kcol + h)),
                  pl.BlockSpec((1, t, HEAD_DIM), lambda b, h, i: (b, 0, vcol + h)),
                  pl.BlockSpec((1, 1, t), lambda b, h, i: (b * B_HEADS + h, 0, 0))],
        out_specs=pl.BlockSpec((1, tq, HEAD_DIM), lambda b, h, i: (b, i, h)),
        out_shape=jax.ShapeDtypeStruct((bsz, t, B_HEADS * HEAD_DIM), BF16),
        compiler_params=_cp(("parallel", "parallel", "arbitrary")),
        name="fox_attention",
    )(big, big, big, c)


def _compress_kernel(x_ref, pe_ref, w1_ref, w2_ref, o_ref):
    half = CMP_STRIDE * HEAD_DIM
    x = x_ref[0, 0, 0]
    n = x.shape[0]
    xa = (x + pe_ref[0, :, :half]).astype(BF16)
    xb = (x + pe_ref[0, :, half:]).astype(BF16)
    a = _dot(xa, w1_ref[0, :half, :])
    b = _dot(xb, w1_ref[0, half:, :])
    hid = a + pltpu.roll(b, n - 1, 0)
    hid = jax.nn.gelu(hid, approximate=True)
    out = _dot(hid.astype(BF16), w2_ref[0])
    row = lax.broadcasted_iota(jnp.int32, out.shape, 0)
    o_ref[0, 0, 0] = jnp.where(row < n - 1, out, 0.0).astype(o_ref.dtype)


def compress_tokens(xflat, pe, w1, w2):
    two, bsz, g, n, width = xflat.shape
    return pl.pallas_call(
        _compress_kernel,
        grid=(two, bsz, g),
        in_specs=[pl.BlockSpec((1, 1, 1, n, width), lambda s, b, gg: (s, b, gg, 0, 0)),
                  pl.BlockSpec((1, 1, 2 * width), lambda s, b, gg: (s, 0, 0)),
                  pl.BlockSpec((1, 2 * width, CMP_HIDDEN), lambda s, b, gg: (s, 0, 0)),
                  pl.BlockSpec((1, CMP_HIDDEN, HEAD_DIM), lambda s, b, gg: (s, 0, 0))],
        out_specs=pl.BlockSpec((1, 1, 1, n, HEAD_DIM), lambda s, b, gg: (s, b, gg, 0, 0)),
        out_shape=jax.ShapeDtypeStruct((two, bsz, g, n, HEAD_DIM), BF16),
        compiler_params=_cp(("parallel", "parallel", "parallel")),
        name="nsa_compress",
    )(xflat, pe, w1, w2)


def _split_bf16(x, parts):
    out = []
    for _ in range(parts - 1):
        p = x.astype(BF16)
        out.append(p)
        x = x - p.astype(F32)
    out.append(x.astype(BF16))
    return out


def _nsa_kernel(n_sel, far_ref, q_ref, kc_ref, vc_ref, ks_ref, vs_ref, kw_ref, vw_ref, gate_ref,
                bdiag_ref, bprev_ref, mimp_ref, expand_ref, o_ref, selx_ref):
    g = pl.program_id(1)
    i = pl.program_id(2)
    r = C_HEADS // C_KV_GROUPS
    rows = r * Q_BLOCK
    n_sb = expand_ref.shape[0]
    n_cmp = kc_ref.shape[3]

    q = jnp.concatenate([q_ref[0, :, hh * HEAD_DIM:(hh + 1) * HEAD_DIM] for hh in range(r)], axis=0)
    ql = lax.broadcasted_iota(jnp.int32, (Q_BLOCK, 1), 0)
    qpos = i * Q_BLOCK + ql
    qpos_r = jnp.concatenate([qpos] * r, axis=0)

    cend = lax.broadcasted_iota(jnp.int32, (1, n_cmp), 1) * CMP_STRIDE + (CMP_BLOCK - 1)
    vis = cend <= qpos_r
    lc = jnp.where(vis, _dot_nt(q, kc_ref[0, 0, 0]) * SCALE, NEG_INF)
    mc = jnp.max(lc, axis=-1, keepdims=True)
    ec = jnp.where(vis, jnp.exp(lc - mc), 0.0)
    den = jnp.sum(ec, axis=-1, keepdims=True)
    pc = ec / jnp.where(den > 0.0, den, 1.0)
    o_cmp = _dot(pc.astype(BF16), vc_ref[0, 0, 0])

    imp = pc[0:Q_BLOCK]
    for hh in range(1, r):
        imp = imp + pc[hh * Q_BLOCK:(hh + 1) * Q_BLOCK]
    blk = sum(_dot(part, mimp_ref[...]) for part in _split_bf16(imp, 3))
    ids = lax.broadcasted_iota(jnp.int32, (Q_BLOCK, n_sb), 1)
    cur = qpos // SEL_BLOCK
    forced = (ids == 0) | (ids == cur) | (ids == cur - 1)
    score = jnp.where(forced, FORCE_SCORE, jnp.where(ids > cur, -FORCE_SCORE, blk))
    sel = jnp.zeros((Q_BLOCK, n_sb), F32)
    for _ in range(n_sel):
        mx = jnp.max(score, axis=-1, keepdims=True)
        first = jnp.min(jnp.where(score == mx, ids, n_sb), axis=-1, keepdims=True)
        pick = ids == first
        sel = jnp.where(pick, 1.0, sel)
        score = jnp.where(pick, -jnp.inf, score)
    selx_ref[...] = _dot(sel.astype(BF16), expand_ref[...])

    far = jnp.concatenate([jnp.full((Q_BLOCK, 1), far_ref[g * r + hh], F32) for hh in range(r)], axis=0)
    bdiag = bdiag_ref[0]
    bprev = bprev_ref[0]
    init = (jnp.full((rows, 1), NEG_INF, F32), jnp.zeros((rows, 1), F32), jnp.zeros((rows, HEAD_DIM), F32))

    def key_tile(k_ref, v_ref, kt):
        off = pl.multiple_of(kt * Q_BLOCK, Q_BLOCK)
        return k_ref[0, pl.ds(off, Q_BLOCK), :], v_ref[0, pl.ds(off, Q_BLOCK), :], off

    def sel_mask(off):
        m1 = selx_ref[:, pl.ds(off, Q_BLOCK)] > 0.5
        return jnp.concatenate([m1] * r, axis=0)

    k, v, off = key_tile(ks_ref, vs_ref, i)
    s = jnp.where(sel_mask(off), _dot_nt(q, k) * SCALE + bdiag, NEG_INF)
    carry = _softmax_step(s, v, *init)
    ip = jnp.maximum(i - 1, 0)
    k, v, off = key_tile(ks_ref, vs_ref, ip)
    s = jnp.where(jnp.logical_and(sel_mask(off), i >= 1), _dot_nt(q, k) * SCALE + bprev, NEG_INF)
    carry = _softmax_step(s, v, *carry)

    def sel_far(kt, c):
        k, v, off = key_tile(ks_ref, vs_ref, kt)
        s = jnp.where(sel_mask(off), _dot_nt(q, k) * SCALE + far, NEG_INF)
        return _softmax_step(s, v, *c)

    m, l, acc = lax.fori_loop(0, jnp.maximum(i - 1, 0), sel_far, carry)
    o_sel = acc / l

    k, v, off = key_tile(kw_ref, vw_ref, i)
    carry = _softmax_step(_dot_nt(q, k) * SCALE + bdiag, v, *init)
    k, v, off = key_tile(kw_ref, vw_ref, ip)
    s = jnp.where(i >= 1, _dot_nt(q, k) * SCALE + bprev, NEG_INF)
    carry = _softmax_step(s, v, *carry)
    n_full = C_WINDOW // Q_BLOCK
    for d in range(2, n_full + 1):
        k, v, off = key_tile(kw_ref, vw_ref, jnp.maximum(i - d, 0))
        s = _dot_nt(q, k) * SCALE + far
        ok = i >= d
        if d == n_full:
            kl = lax.broadcasted_iota(jnp.int32, (rows, Q_BLOCK), 1)
            qrow = jnp.concatenate([ql] * r, axis=0)
            ok = jnp.logical_and(ok, kl > qrow)
        carry = _softmax_step(jnp.where(ok, s, NEG_INF), v, *carry)
    m, l, acc = carry
    o_win = acc / l

    gates = _sigmoid(gate_ref[0])
    lane = lax.broadcasted_iota(jnp.int32, gates.shape, 1)

    def gate(branch, hh):
        idx = branch * C_HEADS + g * r + hh
        return jnp.sum(jnp.where(lane == idx, gates, 0.0), axis=-1, keepdims=True)

    for hh in range(r):
        sl = slice(hh * Q_BLOCK, (hh + 1) * Q_BLOCK)
        out = gate(0, hh) * o_cmp[sl] + gate(1, hh) * o_sel[sl] + gate(2, hh) * o_win[sl]
        o_ref[0, :, hh * HEAD_DIM:(hh + 1) * HEAD_DIM] = out.astype(o_ref.dtype)


def nsa_attention(qkv, cmp_kv, small, rel_bias, bsz, t):
    nb = t // Q_BLOCK
    n_sb = t // SEL_BLOCK
    n_sel = min(N_SELECT, n_sb)
    n_chunk = t // CMP_STRIDE
    g_, r = C_KV_GROUPS, C_HEADS // C_KV_GROUPS
    ql = jnp.arange(Q_BLOCK)[:, None]
    kl = jnp.arange(Q_BLOCK)[None, :]
    bdiag = _bias_table(rel_bias, ql - kl, ql - kl >= 0).reshape(g_, r * Q_BLOCK, Q_BLOCK)
    dprev = ql - kl + Q_BLOCK
    bprev = _bias_table(rel_bias, dprev, dprev >= 0).reshape(g_, r * Q_BLOCK, Q_BLOCK)
    far = rel_bias.astype(F32)[_t5_bucket(jnp.int32(2 * Q_BLOCK))]
    tok = np.arange(n_chunk)[:, None]
    blk = np.arange(n_sb)[None, :]
    per = SEL_BLOCK // CMP_STRIDE
    mimp = ((tok // per == blk).astype(np.float32) + ((tok + 1) // per == blk).astype(np.float32))
    mimp[n_chunk - 1] = 0.0
    expand = (np.arange(n_sb)[:, None] == np.arange(t)[None, :] // SEL_BLOCK).astype(np.float32)
    qw = r * HEAD_DIM
    c0 = C_HEADS * HEAD_DIM // LANES
    kv_spec = lambda off: pl.BlockSpec((1, t, HEAD_DIM), lambda b, g, i: (b, 0, c0 + off * g_ + g))
    cmp_spec = lambda s: pl.BlockSpec((1, 1, 1, n_chunk, HEAD_DIM), lambda b, g, i: (s, b, g, 0, 0))
    return pl.pallas_call(
        functools.partial(_nsa_kernel, n_sel),
        grid=(bsz, g_, nb),
        in_specs=[pl.BlockSpec(memory_space=pltpu.SMEM),
                  pl.BlockSpec((1, Q_BLOCK, qw), lambda b, g, i: (b, i, g)),
                  cmp_spec(0), cmp_spec(1),
                  kv_spec(0), kv_spec(1), kv_spec(2), kv_spec(3),
                  pl.BlockSpec((1, Q_BLOCK, LANES), lambda b, g, i: (b, i, 0)),
                  pl.BlockSpec((1, r * Q_BLOCK, Q_BLOCK), lambda b, g, i: (g, 0, 0)),
                  pl.BlockSpec((1, r * Q_BLOCK, Q_BLOCK), lambda b, g, i: (g, 0, 0)),
                  pl.BlockSpec((n_chunk, n_sb), lambda b, g, i: (0, 0)),
                  pl.BlockSpec((n_sb, t), lambda b, g, i: (0, 0))],
        out_specs=pl.BlockSpec((1, Q_BLOCK, qw), lambda b, g, i: (b, i, g)),
        out_shape=jax.ShapeDtypeStruct((bsz, t, C_HEADS * HEAD_DIM), BF16),
        scratch_shapes=[pltpu.VMEM((Q_BLOCK, t), F32)],
        compiler_params=_cp(("parallel", "parallel", "arbitrary")),
        name="nsa_attention",
    )(far, qkv, cmp_kv, cmp_kv, qkv, qkv, qkv, qkv, small, bdiag, bprev,
      jnp.asarray(mimp, BF16), jnp.asarray(expand, BF16))


def _bmm(a, b):
    return jnp.einsum('cij,cjk->cik', a, b, preferred_element_type=F32)


def _bmm_nt(a, b):
    return jnp.einsum('cik,cjk->cij', a, b, preferred_element_type=F32)


def _bmm_hi(a, b):
    a1, a2 = _split_bf16(a, 2)
    b1, b2 = _split_bf16(b, 2)
    return _bmm(a1, b1) + (_bmm(a1, b2) + _bmm(a2, b1))


def _gdn_prep_kernel(al_ref, dt_ref, q_ref, qp_ref, k_ref, kp_ref, v_ref, vp_ref, wq_ref, wk_ref, wv_ref,
                     a_ref, beta_ref, arow_ref, u_ref, w_ref, qg_ref, kd_ref, attn_ref, eg_ref):
    h = pl.program_id(1)
    rb = pl.program_id(2)
    rows = q_ref.shape[1]
    c = GDN_CHUNK
    nc = rows // c
    a_log = al_ref[h]
    dt_b = dt_ref[h]

    def conv_silu(x_ref, xp_ref, w_ref):
        prev = jnp.where(rb == 0, 0.0, xp_ref[0])
        x = jnp.concatenate([prev, x_ref[0]], axis=0)
        w = w_ref[...]
        n0 = prev.shape[0] - (D_CONV - 1)
        y = sum(w[j:j + 1] * x[n0 + j:n0 + j + rows] for j in range(D_CONV))
        return _silu(y)

    def l2(x):
        return x * lax.rsqrt(jnp.sum(x * x, axis=-1, keepdims=True) + EPS)

    q = (l2(conv_silu(q_ref, qp_ref, wq_ref)) * SCALE).reshape(nc, c, HEAD_DIM)
    k = l2(conv_silu(k_ref, kp_ref, wk_ref)).reshape(nc, c, HEAD_DIM)
    v = conv_silu(v_ref, vp_ref, wv_ref).reshape(nc, c, HEAD_DIM)

    beta = _sigmoid(beta_ref[0, 0]).reshape(nc, c, HEAD_DIM)
    g_col = (-jnp.exp(a_log) * _softplus(a_ref[0, 0] + dt_b)).reshape(nc, c, HEAD_DIM)
    g_row = (-jnp.exp(a_log) * _softplus(arow_ref[0, 0] + dt_b)).reshape(nc, c, c)

    ii = lax.broadcasted_iota(jnp.int32, (nc, c, c), 1)
    jj = lax.broadcasted_iota(jnp.int32, (nc, c, c), 2)
    tril = (jj <= ii).astype(BF16)
    triu = (ii <= jj).astype(BF16)
    gam = sum(_bmm(tril, p) for p in _split_bf16(g_col, 3))
    gam_row = sum(_bmm(p, triu) for p in _split_bf16(g_row, 3))
    causal = jj <= ii
    decay = jnp.where(causal, jnp.exp(jnp.where(causal, gam[:, :, :c] - gam_row, 0.0)), 0.0)

    kb = k * beta
    kbf = k.astype(BF16)
    lmat = jnp.where(jj < ii, _bmm_nt(kb.astype(BF16), kbf) * decay, 0.0)
    eye = (ii == jj).astype(F32)
    inv = eye - lmat
    pw = lmat
    for _ in range(int(math.log2(c)) - 1):
        pw = _bmm_hi(pw, pw)
        inv = inv + _bmm_hi(inv, pw)
    u = _bmm_hi(inv, v * beta)
    w = _bmm_hi(inv, kb * jnp.exp(gam))
    attn = _bmm_nt(q.astype(BF16), kbf) * decay
    g_last = jnp.broadcast_to(gam[:, c - 1:c, :], gam.shape)

    u_ref[0, 0] = u.reshape(rows, HEAD_DIM)
    w_ref[0, 0] = w.reshape(rows, HEAD_DIM).astype(BF16)
    qg_ref[0, 0] = (q * jnp.exp(gam)).reshape(rows, HEAD_DIM).astype(BF16)
    kd_ref[0, 0] = (k * jnp.exp(g_last - gam)).reshape(rows, HEAD_DIM).astype(BF16)
    attn_ref[0, 0] = attn.reshape(rows, c).astype(BF16)
    eg_ref[0, 0] = jnp.exp(g_last[:, 0:8, :])


def _gdn_scan_kernel(u_ref, w_ref, qg_ref, kd_ref, attn_ref, eg_ref, z_ref, ng_ref, o_ref):
    c = GDN_CHUNK
    n = u_ref.shape[2] // c
    ng = ng_ref[...]

    def step(ci, state):
        off = pl.multiple_of(ci * c, c)
        sl = pl.ds(off, c)
        sb = state.astype(BF16)
        v_new = u_ref[0, 0, sl, :] - _dot(w_ref[0, 0, sl, :], sb)
        vb = v_new.astype(BF16)
        o = _dot(qg_ref[0, 0, sl, :], sb) + _dot(attn_ref[0, 0, sl, :], vb)
        eg = eg_ref[0, 0, ci]
        state = (state.reshape(HEAD_DIM // 8, 8, HEAD_DIM) * eg[None]).reshape(HEAD_DIM, HEAD_DIM)
        state = state + lax.dot_general(kd_ref[0, 0, sl, :], vb, (((0,), (0,)), ((), ())),
                                        preferred_element_type=F32)
        o_ref[0, sl, :] = (_rms(o, ng) * _silu(z_ref[0, sl, :])).astype(o_ref.dtype)
        return state

    lax.fori_loop(0, n, step, jnp.zeros((HEAD_DIM, HEAD_DIM), F32))


def gated_deltanet(wide, small, conv_w, a_log, dt_bias, norm_g, bsz, t, rows=512):
    c = GDN_CHUNK
    h = D_HEADS
    nblk = t // rows
    qcol = 2 * C_KV_GROUPS
    kcol, vcol, zcol = qcol + h, qcol + 2 * h, qcol + 3 * h
    gate_col = 3 * C_HEADS
    beta_t = small[:, :, gate_col:gate_col + h].transpose(0, 2, 1)
    a_t = small[:, :, gate_col + h:gate_col + 2 * h].transpose(0, 2, 1)
    beta_b = jnp.broadcast_to(beta_t[..., None], (bsz, h, t, HEAD_DIM))
    a_b = jnp.broadcast_to(a_t[..., None], (bsz, h, t, HEAD_DIM))
    a_row = jnp.broadcast_to(a_t.reshape(bsz, h, t // c, 1, c), (bsz, h, t // c, c, c)).reshape(bsz, h, t, c)
    hb = rows // 8
    main = lambda col: pl.BlockSpec((1, rows, HEAD_DIM), lambda b, hh, r: (b, r, col + hh))
    halo = lambda col: pl.BlockSpec((1, 8, HEAD_DIM), lambda b, hh, r: (b, jnp.maximum(r * hb - 1, 0), col + hh))
    cw = lambda off: pl.BlockSpec((D_CONV, HEAD_DIM), lambda b, hh, r: (0, off * h + hh))
    per_tok = lambda width: pl.BlockSpec((1, 1, rows, width), lambda b, hh, r: (b, hh, r, 0))
    smem = pl.BlockSpec(memory_space=pltpu.SMEM)
    shp = lambda width, dt: jax.ShapeDtypeStruct((bsz, h, t, width), dt)
    u, w, qg, kd, attn, eg = pl.pallas_call(
        _gdn_prep_kernel,
        grid=(bsz, h, nblk),
        in_specs=[smem, smem, main(qcol), halo(qcol), main(kcol), halo(kcol), main(vcol), halo(vcol),
                  cw(0), cw(1), cw(2), per_tok(HEAD_DIM), per_tok(HEAD_DIM), per_tok(c)],
        out_specs=[per_tok(HEAD_DIM), per_tok(HEAD_DIM), per_tok(HEAD_DIM), per_tok(HEAD_DIM), per_tok(c),
                   pl.BlockSpec((1, 1, rows // c, 8, HEAD_DIM), lambda b, hh, r: (b, hh, r, 0, 0))],
        out_shape=[shp(HEAD_DIM, F32), shp(HEAD_DIM, BF16), shp(HEAD_DIM, BF16), shp(HEAD_DIM, BF16), shp(c, BF16),
                   jax.ShapeDtypeStruct((bsz, h, t // c, 8, HEAD_DIM), F32)],
        compiler_params=_cp(("parallel", "parallel", "parallel")),
        name="gdn_prep",
    )(a_log.astype(F32), dt_bias.astype(F32), wide, wide, wide, wide, wide, wide,
      conv_w, conv_w, conv_w, a_b, beta_b, a_row)
    whole = lambda width: pl.BlockSpec((1, 1, t, width), lambda b, hh: (b, hh, 0, 0))
    return pl.pallas_call(
        _gdn_scan_kernel,
        grid=(bsz, h),
        in_specs=[whole(HEAD_DIM), whole(HEAD_DIM), whole(HEAD_DIM), whole(HEAD_DIM), whole(c),
                  pl.BlockSpec((1, 1, t // c, 8, HEAD_DIM), lambda b, hh: (b, hh, 0, 0, 0)),
                  pl.BlockSpec((1, t, HEAD_DIM), lambda b, hh: (b, 0, zcol + hh)),
                  pl.BlockSpec((1, HEAD_DIM), lambda b, hh: (0, 0))],
        out_specs=pl.BlockSpec((1, t, HEAD_DIM), lambda b, hh: (b, 0, hh)),
        out_shape=jax.ShapeDtypeStruct((bsz, t, h * HEAD_DIM), BF16),
        compiler_params=_cp(("parallel", "parallel")),
        name="gdn_scan",
    )(u, w, qg, kd, attn, eg, wide, norm_g.astype(F32).reshape(1, HEAD_DIM))


def _pad_cols(w, width):
    return jnp.pad(w, ((0, 0), (0, width - w.shape[1])))


def even_mixer(h, norm_g, w_in, b_forget, sinks, w_out, rel_bias, bsz, t):
    n_big = (A_HEADS + 2 * A_KV_HEADS + 3 * B_HEADS) * HEAD_DIM
    big = norm_matmul(h, norm_g, w_in[:, :n_big].astype(BF16), BF16, tn=768).reshape(bsz, t, n_big)
    small = norm_matmul(h, norm_g, _pad_cols(w_in[:, n_big:], LANES).astype(BF16), F32)
    o_a = swa_attention(big, sinks, rel_bias, bsz, t)
    f_t = small.reshape(bsz, t, LANES)[:, :, :B_HEADS].transpose(0, 2, 1)
    c = forget_cumsum(f_t, b_forget).reshape(bsz * B_HEADS, 1, t)
    o_b = fox_attention(big, c, bsz, t)
    return out_proj(o_a.reshape(bsz * t, -1), o_b.reshape(bsz * t, -1), w_out.astype(BF16), h)


def odd_mixer(h, norm_g, w_in, cmp_pos, cmp_w1, cmp_w2, conv_w, a_log, dt_bias, gdn_norm, w_out, rel_bias, bsz, t):
    hd = HEAD_DIM
    g = C_KV_GROUPS
    o_q = 0
    o_kcmp = C_HEADS * hd
    o_ksel = o_kcmp + 2 * g * hd
    o_gates = o_ksel + 4 * g * hd
    o_qd = o_gates + 3 * C_HEADS
    o_beta = o_qd + 3 * D_HEADS * hd
    o_z = o_beta + 2 * D_HEADS
    w_bf = jnp.concatenate([w_in[:, o_q:o_kcmp], w_in[:, o_ksel:o_gates]], axis=1)
    w_f32 = jnp.concatenate([w_in[:, o_kcmp:o_ksel], w_in[:, o_qd:o_beta], w_in[:, o_z:]], axis=1)
    w_small = _pad_cols(jnp.concatenate([w_in[:, o_gates:o_qd], w_in[:, o_beta:o_z]], axis=1), LANES)
    qkv = norm_matmul(h, norm_g, w_bf.astype(BF16), BF16).reshape(bsz, t, -1)
    wide = norm_matmul(h, norm_g, w_f32.astype(BF16), F32, tn=768).reshape(bsz, t, -1)
    small = norm_matmul(h, norm_g, w_small.astype(BF16), F32).reshape(bsz, t, LANES)

    n_chunk = t // CMP_STRIDE
    xflat = wide[:, :, :2 * g * hd].reshape(bsz, n_chunk, CMP_STRIDE, 2, g, hd)
    xflat = xflat.transpose(3, 0, 4, 1, 2, 5).reshape(2, bsz, g, n_chunk, CMP_STRIDE * hd)
    cmp_kv = compress_tokens(xflat, cmp_pos.reshape(2, 1, CMP_BLOCK * hd), cmp_w1.astype(BF16), cmp_w2.astype(BF16))
    o_c = nsa_attention(qkv, cmp_kv, small, rel_bias, bsz, t)
    o_d = gated_deltanet(wide, small, conv_w, a_log, dt_bias, gdn_norm, bsz, t)
    return out_proj(o_c.reshape(bsz * t, -1), o_d.reshape(bsz * t, -1), w_out.astype(BF16), h)


def kernel(x, rel_bias, norm_mix, norm_ffn, norm_final, ev_w_in, ev_b_forget, ev_sinks, ev_w_out, od_w_in,
           od_cmp_pos, od_cmp_w1, od_cmp_w2, od_conv_w, od_a_log, od_dt_bias, od_gdn_norm, od_w_out, ffn_w_up,
           ffn_conv_w, ffn_conv_b, ffn_w_down):
    bsz, t, d = x.shape
    h = x.reshape(bsz * t, d)
    for layer in range(norm_mix.shape[0]):
        j = layer // 2
        if layer % 2 == 0:
            h = even_mixer(h, norm_mix[layer], ev_w_in[j], ev_b_forget[j], ev_sinks[j], ev_w_out[j], rel_bias,
                           bsz, t)
        else:
            h = odd_mixer(h, norm_mix[layer], od_w_in[j], od_cmp_pos[j], od_cmp_w1[j], od_cmp_w2[j], od_conv_w[j],
                          od_a_log[j], od_dt_bias[j], od_gdn_norm[j], od_w_out[j], rel_bias, bsz, t)
        dff = ffn_w_down.shape[1]
        w_up = ffn_w_up[layer].astype(BF16)
        h = conv_ffn(h, norm_ffn[layer], w_up[:, :dff], w_up[:, dff:], ffn_conv_w[layer], ffn_conv_b[layer],
                     ffn_w_down[layer].astype(BF16), t)
    return final_norm(h, norm_final).reshape(bsz, t, d)
```

```python
import functools
import math

import jax
import jax.numpy as jnp
import numpy as np
from jax import lax
from jax.experimental import pallas as pl
from jax.experimental.pallas import tpu as pltpu

D_MODEL = 2048
DEPTH = 4
HEAD_DIM = 128
A_HEADS = 8
A_KV_HEADS = 2
A_WINDOW = 128
B_HEADS = 8
C_HEADS = 8
C_KV_GROUPS = 2
CMP_BLOCK = 32
CMP_STRIDE = 16
CMP_HIDDEN = 256
SEL_BLOCK = 64
N_SELECT = 8
C_WINDOW = 512
D_HEADS = 8
D_CONV = 4
GDN_CHUNK = 64
NUM_BUCKETS = 32
MAX_DISTANCE = 128
D_FF = 11 * D_MODEL // 4
FFN_CONV = 3
Q_BLOCK = 128
EPS = 1e-6
NEG_INF = -1e30
FORCE_SCORE = 1e9
SCALE = HEAD_DIM ** -0.5
LOG2E = math.log2(math.e)

F32 = jnp.float32
BF16 = jnp.bfloat16
LANES = 128
HALO = 16
VMEM_LIMIT = 52 * 1024 * 1024


def _cp(dims, vmem=VMEM_LIMIT):
    return pltpu.CompilerParams(dimension_semantics=dims, vmem_limit_bytes=vmem)


def _dot(a, b):
    return jnp.dot(a, b, preferred_element_type=F32)


def _dot_nt(a, b):
    return lax.dot_general(a, b, (((1,), (1,)), ((), ())), preferred_element_type=F32)


def _rms(x, g):
    return x * lax.rsqrt(jnp.mean(x * x, axis=-1, keepdims=True) + EPS) * g


def _sigmoid(x):
    return 1.0 / (1.0 + jnp.exp(-x))


def _silu(x):
    return x * _sigmoid(x)


def _softplus(x):
    return jnp.maximum(x, 0.0) + jnp.log1p(jnp.exp(-jnp.abs(x)))


def _split_bf16(x, parts):
    out = []
    for _ in range(parts - 1):
        p = x.astype(BF16)
        out.append(p)
        x = x - p.astype(F32)
    out.append(x.astype(BF16))
    return out


def _norm_matmul_kernel(x_ref, g_ref, w_ref, o_ref, xn_ref):
    @pl.when(pl.program_id(1) == 0)
    def _():
        xn_ref[...] = _rms(x_ref[...], g_ref[...]).astype(BF16)

    o_ref[...] = _dot(xn_ref[...], w_ref[...]).astype(o_ref.dtype)


def norm_matmul(x, g, w, out_dtype, tm=512, tn=512):
    m, k = x.shape
    n = w.shape[1]
    tn = min(tn, n)
    assert m % tm == 0 and n % tn == 0
    return pl.pallas_call(
        _norm_matmul_kernel,
        grid=(m // tm, n // tn),
        in_specs=[pl.BlockSpec((tm, k), lambda i, j: (i, 0)),
                  pl.BlockSpec((1, k), lambda i, j: (0, 0)),
                  pl.BlockSpec((k, tn), lambda i, j: (0, j))],
        out_specs=pl.BlockSpec((tm, tn), lambda i, j: (i, j)),
        out_shape=jax.ShapeDtypeStruct((m, n), out_dtype),
        scratch_shapes=[pltpu.VMEM((tm, k), BF16)],
        compiler_params=_cp(("parallel", "arbitrary")),
        name="norm_matmul",
    )(x, g.reshape(1, k), w)


def _out_proj_kernel(a1_ref, a2_ref, w1_ref, w2_ref, h_ref, o_ref):
    o_ref[...] = h_ref[...] + _dot(a1_ref[...], w1_ref[...]) + _dot(a2_ref[...], w2_ref[...])


def out_proj(a1, a2, w, h, tm=512):
    m, k1 = a1.shape
    k2 = a2.shape[1]
    n = w.shape[1]
    w1, w2 = w[:k1], w[k1:]
    return pl.pallas_call(
        _out_proj_kernel,
        grid=(m // tm,),
        in_specs=[pl.BlockSpec((tm, k1), lambda i: (i, 0)),
                  pl.BlockSpec((tm, k2), lambda i: (i, 0)),
                  pl.BlockSpec((k1, n), lambda i: (0, 0)),
                  pl.BlockSpec((k2, n), lambda i: (0, 0)),
                  pl.BlockSpec((tm, n), lambda i: (i, 0))],
        out_specs=pl.BlockSpec((tm, n), lambda i: (i, 0)),
        out_shape=jax.ShapeDtypeStruct((m, n), F32),
        compiler_params=_cp(("parallel",)),
        name="out_proj",
    )(a1, a2, w1, w2, h)


def _ffn_kernel(seq_tiles, h_ref, hp_ref, g_ref, wu_ref, wg_ref, cw_ref, cb_ref, wd_ref, o_ref, xn_ref):
    i = pl.program_id(0)
    j = pl.program_id(1)
    tm = h_ref.shape[0]

    @pl.when(j == 0)
    def _():
        keep = (i % seq_tiles != 0).astype(F32)
        xn_ref[0:HALO, :] = (_rms(hp_ref[...], g_ref[...]) * keep).astype(BF16)
        xn_ref[HALO:, :] = _rms(h_ref[...], g_ref[...]).astype(BF16)
        o_ref[...] = h_ref[...]

    u = _dot(xn_ref[HALO:, :], wu_ref[...])
    ge = _dot(xn_ref[...], wg_ref[...])
    cw = cw_ref[...]
    gc = (cw[0:1] * ge[HALO - 2:HALO - 2 + tm] + cw[1:2] * ge[HALO - 1:HALO - 1 + tm]
          + cw[2:3] * ge[HALO:] + cb_ref[...])
    act = _silu(gc) * u
    o_ref[...] += _dot(act.astype(BF16), wd_ref[...])


def conv_ffn(h, g, w_up_u, w_up_g, conv_w, conv_b, w_down, seq, tm=512, tf=512):
    m, k = h.shape
    dff = w_up_u.shape[1]
    assert m % tm == 0 and dff % tf == 0 and seq % tm == 0 and tm % HALO == 0
    hb = tm // HALO
    return pl.pallas_call(
        functools.partial(_ffn_kernel, seq // tm),
        grid=(m // tm, dff // tf),
        in_specs=[pl.BlockSpec((tm, k), lambda i, j: (i, 0), pipeline_mode=pl.Buffered(1)),
                  pl.BlockSpec((HALO, k), lambda i, j: (jnp.maximum(i * hb - 1, 0), 0)),
                  pl.BlockSpec((1, k), lambda i, j: (0, 0)),
                  pl.BlockSpec((k, tf), lambda i, j: (0, j)),
                  pl.BlockSpec((k, tf), lambda i, j: (0, j)),
                  pl.BlockSpec((FFN_CONV, tf), lambda i, j: (0, j)),
                  pl.BlockSpec((1, tf), lambda i, j: (0, j)),
                  pl.BlockSpec((tf, k), lambda i, j: (j, 0))],
        out_specs=pl.BlockSpec((tm, k), lambda i, j: (i, 0)),
        out_shape=jax.ShapeDtypeStruct((m, k), F32),
        scratch_shapes=[pltpu.VMEM((HALO + tm, k), BF16)],
        compiler_params=_cp(("parallel", "arbitrary")),
        name="conv_ffn",
    )(h, h, g.reshape(1, k), w_up_u, w_up_g, conv_w, conv_b.reshape(1, dff), w_down)


def _final_norm_kernel(x_ref, g_ref, o_ref):
    o_ref[...] = _rms(x_ref[...], g_ref[...])


def final_norm(x, g, tm=1024):
    m, k = x.shape
    return pl.pallas_call(
        _final_norm_kernel,
        grid=(m // tm,),
        in_specs=[pl.BlockSpec((tm, k), lambda i: (i, 0)), pl.BlockSpec((1, k), lambda i: (0, 0))],
        out_specs=pl.BlockSpec((tm, k), lambda i: (i, 0)),
        out_shape=jax.ShapeDtypeStruct((m, k), F32),
        compiler_params=_cp(("parallel",)),
        name="final_norm",
    )(x, g.reshape(1, k))


def _t5_bucket(dist):
    max_exact = NUM_BUCKETS // 2
    n = jnp.maximum(dist, 0)
    log_ratio = jnp.log(jnp.maximum(n, 1).astype(F32) / max_exact) / math.log(MAX_DISTANCE / max_exact)
    large = jnp.minimum(max_exact + (log_ratio * (NUM_BUCKETS - max_exact)).astype(jnp.int32), NUM_BUCKETS - 1)
    return jnp.where(n < max_exact, n, large)


def _bias_table(rel_bias, dist, mask):
    b = rel_bias.astype(F32)[_t5_bucket(dist)].transpose(2, 0, 1)
    return jnp.where(mask[None], b, NEG_INF)


def _swa_kernel(sink_ref, q_ref, kp_ref, kc_ref, vp_ref, vc_ref, bias_ref, o_ref):
    g = pl.program_id(1)
    i = pl.program_id(2)
    r = A_HEADS // A_KV_HEADS
    k = jnp.concatenate([kp_ref[0], kc_ref[0]], axis=0)
    v = jnp.concatenate([vp_ref[0], vc_ref[0]], axis=0)
    col = lax.broadcasted_iota(jnp.int32, (Q_BLOCK, 2 * Q_BLOCK), 1)
    first = jnp.logical_and(i == 0, col < Q_BLOCK)
    for hh in range(r):
        qh = q_ref[0, :, hh * HEAD_DIM:(hh + 1) * HEAD_DIM]
        logits = _dot_nt(qh, k) * SCALE + bias_ref[0, hh]
        logits = jnp.where(first, NEG_INF, logits)
        sink = sink_ref[g * r + hh]
        m = jnp.maximum(jnp.max(logits, axis=-1, keepdims=True), sink)
        e = jnp.exp(logits - m)
        p = e / (jnp.sum(e, axis=-1, keepdims=True) + jnp.exp(sink - m))
        o_ref[0, :, hh * HEAD_DIM:(hh + 1) * HEAD_DIM] = _dot(p.astype(BF16), v).astype(o_ref.dtype)


def swa_attention(big, sinks, rel_bias, bsz, t):
    nb = t // Q_BLOCK
    r = A_HEADS // A_KV_HEADS
    ql = jnp.arange(Q_BLOCK)[:, None]
    kl = jnp.arange(2 * Q_BLOCK)[None, :]
    dist = ql + A_WINDOW - kl
    table = _bias_table(rel_bias, dist, (dist >= 0) & (dist < A_WINDOW))
    table = table.reshape(A_KV_HEADS, r, Q_BLOCK, 2 * Q_BLOCK)
    kcol = A_HEADS * HEAD_DIM // LANES
    vcol = kcol + A_KV_HEADS
    prev = lambda i: jnp.maximum(i - 1, 0)
    return pl.pallas_call(
        _swa_kernel,
        grid=(bsz, A_KV_HEADS, nb),
        in_specs=[pl.BlockSpec(memory_space=pltpu.SMEM),
                  pl.BlockSpec((1, Q_BLOCK, r * HEAD_DIM), lambda b, g, i: (b, i, g)),
                  pl.BlockSpec((1, Q_BLOCK, HEAD_DIM), lambda b, g, i: (b, prev(i), kcol + g)),
                  pl.BlockSpec((1, Q_BLOCK, HEAD_DIM), lambda b, g, i: (b, i, kcol + g)),
                  pl.BlockSpec((1, Q_BLOCK, HEAD_DIM), lambda b, g, i: (b, prev(i), vcol + g)),
                  pl.BlockSpec((1, Q_BLOCK, HEAD_DIM), lambda b, g, i: (b, i, vcol + g)),
                  pl.BlockSpec((1, r, Q_BLOCK, 2 * Q_BLOCK), lambda b, g, i: (g, 0, 0, 0))],
        out_specs=pl.BlockSpec((1, Q_BLOCK, r * HEAD_DIM), lambda b, g, i: (b, i, g)),
        out_shape=jax.ShapeDtypeStruct((bsz, t, A_HEADS * HEAD_DIM), BF16),
        compiler_params=_cp(("parallel", "parallel", "arbitrary")),
        name="swa_attention",
    )(sinks.astype(F32), big, big, big, big, big, table)


C_TERMS = 3


def _forget_cumsum_kernel(f_ref, b_ref, c_ref):
    x = f_ref[0] + b_ref[...]
    y = jnp.minimum(x, 0.0) - jnp.log1p(jnp.exp(-jnp.abs(x)))
    t = y.shape[1]
    lane = lax.broadcasted_iota(jnp.int32, y.shape, 1)
    s = 1
    while s < t:
        y = y + jnp.where(lane >= s, pltpu.roll(y, s, 1), 0.0)
        s *= 2
    for n, part in enumerate(_split_bf16(y * (-1.0 / SCALE), C_TERMS)):
        c_ref[0, n] = part


def forget_cumsum(f_t, b_forget):
    bsz, h, t = f_t.shape
    return pl.pallas_call(
        _forget_cumsum_kernel,
        grid=(bsz,),
        in_specs=[pl.BlockSpec((1, h, t), lambda b: (b, 0, 0)), pl.BlockSpec((h, 1), lambda b: (0, 0))],
        out_specs=pl.BlockSpec((1, C_TERMS, h, t), lambda b: (b, 0, 0, 0)),
        out_shape=jax.ShapeDtypeStruct((bsz, C_TERMS, h, t), BF16),
        compiler_params=_cp(("parallel",)),
        name="forget_cumsum",
    )(f_t, b_forget.astype(F32).reshape(h, 1))


def _fox_kernel(tq, q_ref, k_ref, ck_ref, vt_ref, o_ref, m_ref, l_ref, acc_ref):
    i = pl.program_id(2)
    c2 = SCALE * LOG2E
    half = tq // 2
    lane = lax.broadcasted_iota(jnp.int32, (tq, LANES), 1)
    qa = jnp.concatenate([q_ref[0], jnp.where(lane < C_TERMS, 1.0, 0.0).astype(BF16)], axis=1)
    krow = lax.broadcasted_iota(jnp.int32, (tq, half), 0)
    qcol = lax.broadcasted_iota(jnp.int32, (tq, half), 1)

    def chunk(j, diag):
        off = pl.multiple_of(j * tq, tq)
        k = jnp.concatenate([k_ref[0, pl.ds(off, tq), :], ck_ref[0, 0, pl.ds(off, tq), :]], axis=1)
        vt = vt_ref[0, 0, :, pl.ds(off, tq)]
        z_all = _dot_nt(k, qa) * c2
        for hf in range(2):
            cs = slice(hf * half, (hf + 1) * half)
            z = z_all[:, cs]
            if diag:
                z = jnp.where(krow <= qcol + hf * half, z, NEG_INF)
            zmax = jnp.max(z, axis=0, keepdims=True)
            if diag:
                m_new = zmax
            else:
                m_old = m_ref[:, cs]
                m_new = jnp.maximum(m_old, zmax)
                alpha = jnp.exp2(m_old - m_new)
            p = jnp.exp2(z - m_new)
            psum = jnp.sum(p, axis=0, keepdims=True)
            pv = _dot(vt, p.astype(BF16))
            m_ref[:, cs] = m_new
            if diag:
                l_ref[:, cs] = psum
                acc_ref[:, cs] = pv
            else:
                l_ref[:, cs] = alpha * l_ref[:, cs] + psum
                acc_ref[:, cs] = acc_ref[:, cs] * alpha + pv

    chunk(i, True)

    def far(j, carry):
        chunk(j, False)
        return carry

    lax.fori_loop(0, i, far, 0)
    out = acc_ref[...] * (1.0 / l_ref[...])
    for n in range(tq // HEAD_DIM):
        rs = slice(n * HEAD_DIM, (n + 1) * HEAD_DIM)
        o_ref[0, rs, :] = out[:, rs].T.astype(o_ref.dtype)


def fox_attention(big, c_parts, bsz, t, tq=512):
    qcol = (A_HEADS + 2 * A_KV_HEADS) * HEAD_DIM // LANES
    kcol = qcol + B_HEADS
    vcol = kcol + B_HEADS
    ck = jnp.pad(c_parts.transpose(0, 2, 3, 1), ((0, 0), (0, 0), (0, 0), (0, LANES - C_TERMS)))
    v0 = vcol * LANES
    vt = big[:, :, v0:v0 + B_HEADS * HEAD_DIM].reshape(bsz, t, B_HEADS, HEAD_DIM).transpose(0, 2, 3, 1)
    return pl.pallas_call(
        functools.partial(_fox_kernel, tq),
        grid=(bsz, B_HEADS, t // tq),
        in_specs=[pl.BlockSpec((1, tq, HEAD_DIM), lambda b, h, i: (b, i, qcol + h)),
                  pl.BlockSpec((1, t, HEAD_DIM), lambda b, h, i: (b, 0, kcol + h)),
                  pl.BlockSpec((1, 1, t, LANES), lambda b, h, i: (b, h, 0, 0)),
                  pl.BlockSpec((1, 1, HEAD_DIM, t), lambda b, h, i: (b, h, 0, 0))],
        out_specs=pl.BlockSpec((1, tq, HEAD_DIM), lambda b, h, i: (b, i, h)),
        scratch_shapes=[pltpu.VMEM((1, tq), F32), pltpu.VMEM((1, tq), F32), pltpu.VMEM((HEAD_DIM, tq), F32)],
        out_shape=jax.ShapeDtypeStruct((bsz, t, B_HEADS * HEAD_DIM), BF16),
        compiler_params=_cp(("parallel", "parallel", "arbitrary")),
        name="fox_attention",
    )(big, big, ck, vt)


def _compress_kernel(x_ref, pe_ref, w1_ref, w2_ref, o_ref):
    half = CMP_STRIDE * HEAD_DIM
    x = x_ref[0, 0, 0]
    n = x.shape[0]
    xa = (x + pe_ref[0, :, :half]).astype(BF16)
    xb = (x + pe_ref[0, :, half:]).astype(BF16)
    a = _dot(xa, w1_ref[0, :half, :])
    b = _dot(xb, w1_ref[0, half:, :])
    hid = a + pltpu.roll(b, n - 1, 0)
    hid = jax.nn.gelu(hid, approximate=True)
    out = _dot(hid.astype(BF16), w2_ref[0])
    row = lax.broadcasted_iota(jnp.int32, out.shape, 0)
    o_ref[0, 0, 0] = jnp.where(row < n - 1, out, 0.0).astype(o_ref.dtype)


def compress_tokens(xflat, pe, w1, w2):
    two, bsz, g, n, width = xflat.shape
    return pl.pallas_call(
        _compress_kernel,
        grid=(two, bsz, g),
        in_specs=[pl.BlockSpec((1, 1, 1, n, width), lambda s, b, gg: (s, b, gg, 0, 0)),
                  pl.BlockSpec((1, 1, 2 * width), lambda s, b, gg: (s, 0, 0)),
                  pl.BlockSpec((1, 2 * width, CMP_HIDDEN), lambda s, b, gg: (s, 0, 0)),
                  pl.BlockSpec((1, CMP_HIDDEN, HEAD_DIM), lambda s, b, gg: (s, 0, 0))],
        out_specs=pl.BlockSpec((1, 1, 1, n, HEAD_DIM), lambda s, b, gg: (s, b, gg, 0, 0)),
        out_shape=jax.ShapeDtypeStruct((two, bsz, g, n, HEAD_DIM), BF16),
        compiler_params=_cp(("parallel", "parallel", "parallel")),
        name="nsa_compress",
    )(xflat, pe, w1, w2)


NSA_PAD = 4 * Q_BLOCK
MASK_BIG = 2.0 ** 100


def _nsa_kernel(n_sel, q_ref, kc_ref, vct_ref, ks_ref, vst_ref, kw_ref, vwt_ref, gate_ref, near_ref, mimp_ref,
                o_ref, m_ref, l_ref, acc_ref, osel_ref, ocmp_ref):
    g = pl.program_id(1)
    i = pl.program_id(2)
    r = C_HEADS // C_KV_GROUPS
    rows = r * Q_BLOCK
    n_sb = mimp_ref.shape[0]
    n_cmp = kc_ref.shape[3]
    c2 = SCALE * LOG2E

    q = jnp.concatenate([q_ref[0, :, hh * HEAD_DIM:(hh + 1) * HEAD_DIM] for hh in range(r)], axis=0)
    qlane = lax.broadcasted_iota(jnp.int32, (1, rows), 1) & (Q_BLOCK - 1)
    qpos = i * Q_BLOCK + qlane

    cend = lax.broadcasted_iota(jnp.int32, (n_cmp, 1), 0) * CMP_STRIDE + (CMP_BLOCK - 1)
    vis = cend <= qpos
    lc = jnp.where(vis, _dot_nt(kc_ref[0, 0, 0], q) * SCALE, NEG_INF)
    mc = jnp.max(lc, axis=0, keepdims=True)
    ec = jnp.where(vis, jnp.exp(lc - mc), 0.0)
    den = jnp.sum(ec, axis=0, keepdims=True)
    pc = ec / jnp.where(den > 0.0, den, 1.0)
    ocmp_ref[...] = _dot(vct_ref[0, 0, 0], pc.astype(BF16))

    imp = pc[:, 0:Q_BLOCK]
    for hh in range(1, r):
        imp = imp + pc[:, hh * Q_BLOCK:(hh + 1) * Q_BLOCK]
    blk = sum(_dot(mimp_ref[...], part) for part in _split_bf16(imp, 3))
    ids = lax.broadcasted_iota(jnp.int32, (n_sb, Q_BLOCK), 0)
    cur = (i * Q_BLOCK + lax.broadcasted_iota(jnp.int32, (n_sb, Q_BLOCK), 1)) // SEL_BLOCK
    forced = (ids == 0) | (ids == cur) | (ids == cur - 1)
    score = jnp.where(forced, FORCE_SCORE, jnp.where(ids > cur, -FORCE_SCORE, blk))
    sub = 8
    tiles = [score[v * sub:(v + 1) * sub] for v in range(n_sb // sub)]
    cnt = [jnp.zeros((sub, Q_BLOCK), F32) for _ in tiles]
    sub_id = lax.broadcasted_iota(jnp.int32, (sub, Q_BLOCK), 0)
    for jp in range(n_sb):
        row = score[jp:jp + 1]
        for v, tile in enumerate(tiles):
            ge = lambda: jnp.where(row >= tile, 1.0, 0.0)
            gt = lambda: jnp.where(row > tile, 1.0, 0.0)
            if v * sub > jp:
                inc = ge()
            elif v * sub + sub - 1 <= jp:
                inc = gt()
            else:
                inc = jnp.where(sub_id + v * sub > jp, ge(), gt())
            cnt[v] = cnt[v] + inc
    notsel = jnp.where(jnp.concatenate(cnt, axis=0) < n_sel, 0.0, 1.0).astype(BF16)
    place = (lax.broadcasted_iota(jnp.int32, (n_sb, LANES), 0)
             == lax.broadcasted_iota(jnp.int32, (n_sb, LANES), 1)).astype(BF16)
    ext = lax.dot_general(notsel, place, (((0,), (0,)), ((), ())), preferred_element_type=F32)
    lane = lax.broadcasted_iota(jnp.int32, (Q_BLOCK, LANES), 1)
    ext = jnp.where(lane == LANES - 1, 1.0, ext).astype(BF16)
    qa = jnp.concatenate([q, jnp.concatenate([ext] * r, axis=0)], axis=1)
    half = rows // 2

    def chunk(k, vt, tab, first):
        z_all = _dot_nt(k, qa) * c2
        for hf in range(2):
            cs = slice(hf * half, (hf + 1) * half)
            z = z_all[:, cs]
            if tab is not None:
                z = z + tab(cs)
            zmax = jnp.max(z, axis=0, keepdims=True)
            if first:
                m_new = zmax
            else:
                m_old = m_ref[:, cs]
                m_new = jnp.maximum(m_old, zmax)
                alpha = jnp.exp2(m_old - m_new)
            p = jnp.exp2(z - m_new)
            psum = jnp.sum(p, axis=0, keepdims=True)
            pv = _dot(vt, p.astype(BF16))
            m_ref[:, cs] = m_new
            if first:
                l_ref[:, cs] = psum
                acc_ref[:, cs] = pv
            else:
                l_ref[:, cs] = alpha * l_ref[:, cs] + psum
                acc_ref[:, cs] = acc_ref[:, cs] * alpha + pv

    def kv(k_ref, vt_ref, off, n):
        off = pl.multiple_of(off, Q_BLOCK)
        return k_ref[0, 0, pl.ds(off, n), :], vt_ref[0, 0, :, pl.ds(off, n)]

    near = lambda cs: near_ref[0, :, cs]
    kl = lax.broadcasted_iota(jnp.int32, (Q_BLOCK, half), 0)
    qh = lax.broadcasted_iota(jnp.int32, (Q_BLOCK, half), 1) & (Q_BLOCK - 1)
    tri_tab = jnp.where(kl > qh, 0.0, NEG_INF)
    tri = lambda cs: tri_tab
    base = (i - 1) * Q_BLOCK + NSA_PAD

    chunk(*kv(ks_ref, vst_ref, base, 2 * Q_BLOCK), near, True)

    def sel_far(c, carry):
        chunk(*kv(ks_ref, vst_ref, base - (c + 1) * (4 * Q_BLOCK), 4 * Q_BLOCK), None, False)
        return carry

    lax.fori_loop(0, (i + 2) // 4, sel_far, 0)
    osel_ref[...] = acc_ref[...] * (1.0 / l_ref[...])

    chunk(*kv(kw_ref, vwt_ref, base, 2 * Q_BLOCK), near, True)
    chunk(*kv(kw_ref, vwt_ref, base - 2 * Q_BLOCK, 2 * Q_BLOCK), None, False)
    chunk(*kv(kw_ref, vwt_ref, base - 3 * Q_BLOCK, Q_BLOCK), tri, False)

    gates = _sigmoid(gate_ref[0])
    inv_l = 1.0 / l_ref[...]
    for hh in range(r):
        cs = slice(hh * Q_BLOCK, (hh + 1) * Q_BLOCK)
        gate = lambda branch: gate_ref_row(gates, branch * C_HEADS + g * r + hh)
        out = (gate(0) * ocmp_ref[:, cs] + gate(1) * osel_ref[:, cs]
               + gate(2) * (acc_ref[:, cs] * inv_l[:, cs]))
        o_ref[0, :, hh * HEAD_DIM:(hh + 1) * HEAD_DIM] = out.T.astype(o_ref.dtype)


def gate_ref_row(gates, idx):
    sel = lax.broadcasted_iota(jnp.int32, gates.shape, 0) == idx
    return jnp.sum(jnp.where(sel, gates, 0.0), axis=0, keepdims=True)


def nsa_attention(qkv, cmp_kv, small, rel_bias, bsz, t):
    assert C_WINDOW == 4 * Q_BLOCK and NSA_PAD >= C_WINDOW
    nb = t // Q_BLOCK
    n_sb = t // SEL_BLOCK
    assert n_sb < LANES and n_sb % 8 == 0
    n_sel = min(N_SELECT, n_sb)
    n_chunk = t // CMP_STRIDE
    g_, r = C_KV_GROUPS, C_HEADS // C_KV_GROUPS
    hd = HEAD_DIM
    rows = r * Q_BLOCK
    ql = jnp.arange(Q_BLOCK)[:, None]
    kl = jnp.arange(2 * Q_BLOCK)[None, :]
    dist = ql + Q_BLOCK - kl
    far = rel_bias.astype(F32)[_t5_bucket(jnp.int32(2 * Q_BLOCK))]
    near = (_bias_table(rel_bias, dist, dist >= 0) - far[:, None, None]) * LOG2E
    near = near.reshape(g_, r, Q_BLOCK, 2 * Q_BLOCK).transpose(0, 3, 1, 2).reshape(g_, 2 * Q_BLOCK, rows)
    tok = np.arange(n_chunk)[None, :]
    blk = np.arange(n_sb)[:, None]
    per = SEL_BLOCK // CMP_STRIDE
    mimp = ((tok // per == blk).astype(np.float32) + ((tok + 1) // per == blk).astype(np.float32))
    mimp[:, n_chunk - 1] = 0.0
    blk_cols = np.where(np.arange(t)[:, None] // SEL_BLOCK == np.arange(LANES)[None, :], -MASK_BIG, 0.0)
    pad_cols = np.zeros((NSA_PAD, 2 * LANES), np.float32)
    pad_cols[:, -1] = -MASK_BIG

    def heads(col):
        return qkv[:, :, col:col + g_ * hd].reshape(bsz, t, g_, hd).transpose(0, 2, 1, 3)

    def with_mask(k, cols):
        cols = jnp.broadcast_to(jnp.asarray(cols, BF16), (bsz, g_, t, LANES))
        pad = jnp.broadcast_to(jnp.asarray(pad_cols, BF16), (bsz, g_, NSA_PAD, 2 * LANES))
        return jnp.concatenate([pad, jnp.concatenate([k, cols], axis=-1)], axis=2)

    def padded_t(v):
        return jnp.pad(v.transpose(0, 1, 3, 2), ((0, 0), (0, 0), (0, 0), (NSA_PAD, 0)))

    c0 = C_HEADS * hd
    ks = with_mask(heads(c0), blk_cols)
    vst = padded_t(heads(c0 + g_ * hd))
    kw = with_mask(heads(c0 + 2 * g_ * hd), np.zeros((t, LANES), np.float32))
    vwt = padded_t(heads(c0 + 3 * g_ * hd))
    vct = cmp_kv[1].transpose(0, 1, 3, 2)[None]
    n_gate = 3 * C_HEADS
    gates_t = small[:, :, :n_gate].transpose(0, 2, 1)
    tp = t + NSA_PAD
    kspec = pl.BlockSpec((1, 1, tp, 2 * LANES), lambda b, g, i: (b, g, 0, 0))
    vspec = pl.BlockSpec((1, 1, hd, tp), lambda b, g, i: (b, g, 0, 0))
    return pl.pallas_call(
        functools.partial(_nsa_kernel, n_sel),
        grid=(bsz, g_, nb),
        in_specs=[pl.BlockSpec((1, Q_BLOCK, r * hd), lambda b, g, i: (b, i, g)),
                  pl.BlockSpec((1, 1, 1, n_chunk, hd), lambda b, g, i: (0, b, g, 0, 0)),
                  pl.BlockSpec((1, 1, 1, hd, n_chunk), lambda b, g, i: (0, b, g, 0, 0)),
                  kspec, vspec, kspec, vspec,
                  pl.BlockSpec((1, n_gate, Q_BLOCK), lambda b, g, i: (b, 0, i)),
                  pl.BlockSpec((1, 2 * Q_BLOCK, rows), lambda b, g, i: (g, 0, 0)),
                  pl.BlockSpec((n_sb, n_chunk), lambda b, g, i: (0, 0))],
        out_specs=pl.BlockSpec((1, Q_BLOCK, r * hd), lambda b, g, i: (b, i, g)),
        out_shape=jax.ShapeDtypeStruct((bsz, t, C_HEADS * hd), BF16),
        scratch_shapes=[pltpu.VMEM((1, rows), F32), pltpu.VMEM((1, rows), F32),
                        pltpu.VMEM((hd, rows), F32), pltpu.VMEM((hd, rows), F32), pltpu.VMEM((hd, rows), F32)],
        compiler_params=_cp(("parallel", "parallel", "arbitrary")),
        name="nsa_attention",
    )(qkv, cmp_kv, vct, ks, vst, kw, vwt, gates_t, near, jnp.asarray(mimp, BF16))


def _bmm(a, b):
    return jnp.einsum('cij,cjk->cik', a, b, preferred_element_type=F32)


def _bmm_nt(a, b):
    return jnp.einsum('cik,cjk->cij', a, b, preferred_element_type=F32)


def _bmm_hi(a, b):
    a1, a2 = _split_bf16(a, 2)
    b1, b2 = _split_bf16(b, 2)
    return _bmm(a1, b1) + (_bmm(a1, b2) + _bmm(a2, b1))


def _gdn_prep_kernel(al_ref, dt_ref, q_ref, qp_ref, k_ref, kp_ref, v_ref, vp_ref, wq_ref, wk_ref, wv_ref,
                     a_ref, beta_ref, arow_ref, u_ref, w_ref, qg_ref, kd_ref, attn_ref, eg_ref):
    h = pl.program_id(1)
    rb = pl.program_id(2)
    rows = q_ref.shape[1]
    c = GDN_CHUNK
    nc = rows // c
    a_log = al_ref[h]
    dt_b = dt_ref[h]

    def conv_silu(x_ref, xp_ref, w_ref):
        prev = jnp.where(rb == 0, 0.0, xp_ref[0])
        x = jnp.concatenate([prev, x_ref[0]], axis=0)
        w = w_ref[...]
        n0 = prev.shape[0] - (D_CONV - 1)
        y = sum(w[j:j + 1] * x[n0 + j:n0 + j + rows] for j in range(D_CONV))
        return _silu(y)

    def l2(x):
        return x * lax.rsqrt(jnp.sum(x * x, axis=-1, keepdims=True) + EPS)

    q = (l2(conv_silu(q_ref, qp_ref, wq_ref)) * SCALE).reshape(nc, c, HEAD_DIM)
    k = l2(conv_silu(k_ref, kp_ref, wk_ref)).reshape(nc, c, HEAD_DIM)
    v = conv_silu(v_ref, vp_ref, wv_ref).reshape(nc, c, HEAD_DIM)

    beta = _sigmoid(beta_ref[0, 0]).reshape(nc, c, HEAD_DIM)
    g_col = (-jnp.exp(a_log) * _softplus(a_ref[0, 0] + dt_b)).reshape(nc, c, HEAD_DIM)
    g_row = (-jnp.exp(a_log) * _softplus(arow_ref[0, 0] + dt_b)).reshape(nc, c, c)

    ii = lax.broadcasted_iota(jnp.int32, (nc, c, c), 1)
    jj = lax.broadcasted_iota(jnp.int32, (nc, c, c), 2)
    tril = (jj <= ii).astype(BF16)
    triu = (ii <= jj).astype(BF16)
    gam = sum(_bmm(tril, p) for p in _split_bf16(g_col, 3))
    gam_row = sum(_bmm(p, triu) for p in _split_bf16(g_row, 3))
    causal = jj <= ii
    decay = jnp.where(causal, jnp.exp(jnp.where(causal, gam[:, :, :c] - gam_row, 0.0)), 0.0)

    kb = k * beta
    kbf = k.astype(BF16)
    lmat = jnp.where(jj < ii, _bmm_nt(kb.astype(BF16), kbf) * decay, 0.0)
    eye = (ii == jj).astype(F32)
    inv = eye - lmat
    pw = lmat
    for _ in range(int(math.log2(c)) - 1):
        pw = _bmm_hi(pw, pw)
        inv = inv + _bmm_hi(inv, pw)
    u = _bmm_hi(inv, v * beta)
    w = _bmm_hi(inv, kb * jnp.exp(gam))
    attn = _bmm_nt(q.astype(BF16), kbf) * decay
    g_last = jnp.broadcast_to(gam[:, c - 1:c, :], gam.shape)

    u_ref[0, 0] = u.reshape(rows, HEAD_DIM)
    w_ref[0, 0] = w.reshape(rows, HEAD_DIM).astype(BF16)
    qg_ref[0, 0] = (q * jnp.exp(gam)).reshape(rows, HEAD_DIM).astype(BF16)
    kd_ref[0, 0] = (k * jnp.exp(g_last - gam)).reshape(rows, HEAD_DIM).astype(BF16)
    attn_ref[0, 0] = attn.reshape(rows, c).astype(BF16)
    eg_ref[0, 0] = jnp.exp(g_last[:, 0:8, :])


def _gdn_scan_kernel(u_ref, w_ref, qg_ref, kd_ref, attn_ref, eg_ref, z_ref, ng_ref, o_ref):
    c = GDN_CHUNK
    n = u_ref.shape[2] // c
    ng = ng_ref[...]

    def step(ci, state):
        off = pl.multiple_of(ci * c, c)
        sl = pl.ds(off, c)
        sb = state.astype(BF16)
        v_new = u_ref[0, 0, sl, :] - _dot(w_ref[0, 0, sl, :], sb)
        vb = v_new.astype(BF16)
        o = _dot(qg_ref[0, 0, sl, :], sb) + _dot(attn_ref[0, 0, sl, :], vb)
        eg = eg_ref[0, 0, ci]
        state = (state.reshape(HEAD_DIM // 8, 8, HEAD_DIM) * eg[None]).reshape(HEAD_DIM, HEAD_DIM)
        state = state + lax.dot_general(kd_ref[0, 0, sl, :], vb, (((0,), (0,)), ((), ())),
                                        preferred_element_type=F32)
        o_ref[0, sl, :] = (_rms(o, ng) * _silu(z_ref[0, sl, :])).astype(o_ref.dtype)
        return state

    lax.fori_loop(0, n, step, jnp.zeros((HEAD_DIM, HEAD_DIM), F32))


def gated_deltanet(wide, small, conv_w, a_log, dt_bias, norm_g, bsz, t, rows=512):
    c = GDN_CHUNK
    h = D_HEADS
    nblk = t // rows
    qcol = 2 * C_KV_GROUPS
    kcol, vcol, zcol = qcol + h, qcol + 2 * h, qcol + 3 * h
    gate_col = 3 * C_HEADS
    beta_t = small[:, :, gate_col:gate_col + h].transpose(0, 2, 1)
    a_t = small[:, :, gate_col + h:gate_col + 2 * h].transpose(0, 2, 1)
    beta_b = jnp.broadcast_to(beta_t[..., None], (bsz, h, t, HEAD_DIM))
    a_b = jnp.broadcast_to(a_t[..., None], (bsz, h, t, HEAD_DIM))
    a_row = jnp.broadcast_to(a_t.reshape(bsz, h, t // c, 1, c), (bsz, h, t // c, c, c)).reshape(bsz, h, t, c)
    hb = rows // 8
    main = lambda col: pl.BlockSpec((1, rows, HEAD_DIM), lambda b, hh, r: (b, r, col + hh))
    halo = lambda col: pl.BlockSpec((1, 8, HEAD_DIM), lambda b, hh, r: (b, jnp.maximum(r * hb - 1, 0), col + hh))
    cw = lambda off: pl.BlockSpec((D_CONV, HEAD_DIM), lambda b, hh, r: (0, off * h + hh))
    per_tok = lambda width: pl.BlockSpec((1, 1, rows, width), lambda b, hh, r: (b, hh, r, 0))
    smem = pl.BlockSpec(memory_space=pltpu.SMEM)
    shp = lambda width, dt: jax.ShapeDtypeStruct((bsz, h, t, width), dt)
    u, w, qg, kd, attn, eg = pl.pallas_call(
        _gdn_prep_kernel,
        grid=(bsz, h, nblk),
        in_specs=[smem, smem, main(qcol), halo(qcol), main(kcol), halo(kcol), main(vcol), halo(vcol),
                  cw(0), cw(1), cw(2), per_tok(HEAD_DIM), per_tok(HEAD_DIM), per_tok(c)],
        out_specs=[per_tok(HEAD_DIM), per_tok(HEAD_DIM), per_tok(HEAD_DIM), per_tok(HEAD_DIM), per_tok(c),
                   pl.BlockSpec((1, 1, rows // c, 8, HEAD_DIM), lambda b, hh, r: (b, hh, r, 0, 0))],
        out_shape=[shp(HEAD_DIM, F32), shp(HEAD_DIM, BF16), shp(HEAD_DIM, BF16), shp(HEAD_DIM, BF16), shp(c, BF16),
                   jax.ShapeDtypeStruct((bsz, h, t // c, 8, HEAD_DIM), F32)],
        compiler_params=_cp(("parallel", "parallel", "parallel")),
        name="gdn_prep",
    )(a_log.astype(F32), dt_bias.astype(F32), wide, wide, wide, wide, wide, wide,
      conv_w, conv_w, conv_w, a_b, beta_b, a_row)
    whole = lambda width: pl.BlockSpec((1, 1, t, width), lambda b, hh: (b, hh, 0, 0))
    return pl.pallas_call(
        _gdn_scan_kernel,
        grid=(bsz, h),
        in_specs=[whole(HEAD_DIM), whole(HEAD_DIM), whole(HEAD_DIM), whole(HEAD_DIM), whole(c),
                  pl.BlockSpec((1, 1, t // c, 8, HEAD_DIM), lambda b, hh: (b, hh, 0, 0, 0)),
                  pl.BlockSpec((1, t, HEAD_DIM), lambda b, hh: (b, 0, zcol + hh)),
                  pl.BlockSpec((1, HEAD_DIM), lambda b, hh: (0, 0))],
        out_specs=pl.BlockSpec((1, t, HEAD_DIM), lambda b, hh: (b, 0, hh)),
        out_shape=jax.ShapeDtypeStruct((bsz, t, h * HEAD_DIM), BF16),
        compiler_params=_cp(("parallel", "parallel")),
        name="gdn_scan",
    )(u, w, qg, kd, attn, eg, wide, norm_g.astype(F32).reshape(1, HEAD_DIM))


def _pad_cols(w, width):
    return jnp.pad(w, ((0, 0), (0, width - w.shape[1])))


def even_mixer(h, norm_g, w_in, b_forget, sinks, w_out, rel_bias, bsz, t):
    n_big = (A_HEADS + 2 * A_KV_HEADS + 3 * B_HEADS) * HEAD_DIM
    big = norm_matmul(h, norm_g, w_in[:, :n_big].astype(BF16), BF16, tn=768).reshape(bsz, t, n_big)
    small = norm_matmul(h, norm_g, _pad_cols(w_in[:, n_big:], LANES).astype(BF16), F32)
    o_a = swa_attention(big, sinks, rel_bias, bsz, t)
    f_t = small.reshape(bsz, t, LANES)[:, :, :B_HEADS].transpose(0, 2, 1)
    o_b = fox_attention(big, forget_cumsum(f_t, b_forget), bsz, t)
    return out_proj(o_a.reshape(bsz * t, -1), o_b.reshape(bsz * t, -1), w_out.astype(BF16), h)


def odd_mixer(h, norm_g, w_in, cmp_pos, cmp_w1, cmp_w2, conv_w, a_log, dt_bias, gdn_norm, w_out, rel_bias, bsz, t):
    hd = HEAD_DIM
    g = C_KV_GROUPS
    o_q = 0
    o_kcmp = C_HEADS * hd
    o_ksel = o_kcmp + 2 * g * hd
    o_gates = o_ksel + 4 * g * hd
    o_qd = o_gates + 3 * C_HEADS
    o_beta = o_qd + 3 * D_HEADS * hd
    o_z = o_beta + 2 * D_HEADS
    w_bf = jnp.concatenate([w_in[:, o_q:o_kcmp], w_in[:, o_ksel:o_gates]], axis=1)
    w_f32 = jnp.concatenate([w_in[:, o_kcmp:o_ksel], w_in[:, o_qd:o_beta], w_in[:, o_z:]], axis=1)
    w_small = _pad_cols(jnp.concatenate([w_in[:, o_gates:o_qd], w_in[:, o_beta:o_z]], axis=1), LANES)
    qkv = norm_matmul(h, norm_g, w_bf.astype(BF16), BF16).reshape(bsz, t, -1)
    wide = norm_matmul(h, norm_g, w_f32.astype(BF16), F32, tn=768).reshape(bsz, t, -1)
    small = norm_matmul(h, norm_g, w_small.astype(BF16), F32).reshape(bsz, t, LANES)

    n_chunk = t // CMP_STRIDE
    xflat = wide[:, :, :2 * g * hd].reshape(bsz, n_chunk, CMP_STRIDE, 2, g, hd)
    xflat = xflat.transpose(3, 0, 4, 1, 2, 5).reshape(2, bsz, g, n_chunk, CMP_STRIDE * hd)
    cmp_kv = compress_tokens(xflat, cmp_pos.reshape(2, 1, CMP_BLOCK * hd), cmp_w1.astype(BF16), cmp_w2.astype(BF16))
    o_c = nsa_attention(qkv, cmp_kv, small, rel_bias, bsz, t)
    o_d = gated_deltanet(wide, small, conv_w, a_log, dt_bias, gdn_norm, bsz, t)
    return out_proj(o_c.reshape(bsz * t, -1), o_d.reshape(bsz * t, -1), w_out.astype(BF16), h)


def kernel(x, rel_bias, norm_mix, norm_ffn, norm_final, ev_w_in, ev_b_forget, ev_sinks, ev_w_out, od_w_in,
           od_cmp_pos, od_cmp_w1, od_cmp_w2, od_conv_w, od_a_log, od_dt_bias, od_gdn_norm, od_w_out, ffn_w_up,
           ffn_conv_w, ffn_conv_b, ffn_w_down):
    bsz, t, d = x.shape
    h = x.reshape(bsz * t, d)
    for layer in range(norm_mix.shape[0]):
        j = layer // 2
        if layer % 2 == 0:
            h = even_mixer(h, norm_mix[layer], ev_w_in[j], ev_b_forget[j], ev_sinks[j], ev_w_out[j], rel_bias,
                           bsz, t)
        else:
            h = odd_mixer(h, norm_mix[layer], od_w_in[j], od_cmp_pos[j], od_cmp_w1[j], od_cmp_w2[j], od_conv_w[j],
                          od_a_log[j], od_dt_bias[j], od_gdn_norm[j], od_w_out[j], rel_bias, bsz, t)
        dff = ffn_w_down.shape[1]
        w_up = ffn_w_up[layer].astype(BF16)
        h = conv_ffn(h, norm_ffn[layer], w_up[:, :dff], w_up[:, dff:], ffn_conv_w[layer], ffn_conv_b[layer],
                     ffn_w_down[layer].astype(BF16), t)
    return final_norm(h, norm_final).reshape(bsz, t, d)
```

```python
import functools
import math

import jax
import jax.numpy as jnp
import numpy as np
from jax import lax
from jax.experimental import pallas as pl
from jax.experimental.pallas import tpu as pltpu

D_MODEL = 2048
DEPTH = 4
HEAD_DIM = 128
A_HEADS = 8
A_KV_HEADS = 2
A_WINDOW = 128
B_HEADS = 8
C_HEADS = 8
C_KV_GROUPS = 2
CMP_BLOCK = 32
CMP_STRIDE = 16
CMP_HIDDEN = 256
SEL_BLOCK = 64
N_SELECT = 8
C_WINDOW = 512
D_HEADS = 8
D_CONV = 4
GDN_CHUNK = 64
NUM_BUCKETS = 32
MAX_DISTANCE = 128
D_FF = 11 * D_MODEL // 4
FFN_CONV = 3
Q_BLOCK = 128
EPS = 1e-6
NEG_INF = -1e30
FORCE_SCORE = 1e9
SCALE = HEAD_DIM ** -0.5
LOG2E = math.log2(math.e)

F32 = jnp.float32
BF16 = jnp.bfloat16
LANES = 128
HALO = 16
VMEM_LIMIT = 52 * 1024 * 1024


def _cp(dims, vmem=VMEM_LIMIT):
    return pltpu.CompilerParams(dimension_semantics=dims, vmem_limit_bytes=vmem)


def _dot(a, b):
    return jnp.dot(a, b, preferred_element_type=F32)


def _dot_nt(a, b):
    return lax.dot_general(a, b, (((1,), (1,)), ((), ())), preferred_element_type=F32)


def _rms(x, g):
    return x * lax.rsqrt(jnp.mean(x * x, axis=-1, keepdims=True) + EPS) * g


def _sigmoid(x):
    return 1.0 / (1.0 + jnp.exp(-x))


def _silu(x):
    return x * _sigmoid(x)


def _softplus(x):
    return jnp.maximum(x, 0.0) + jnp.log1p(jnp.exp(-jnp.abs(x)))


def _split_bf16(x, parts):
    out = []
    for _ in range(parts - 1):
        p = x.astype(BF16)
        out.append(p)
        x = x - p.astype(F32)
    out.append(x.astype(BF16))
    return out


def _norm_matmul_kernel(has_side, x_ref, g_ref, w_ref, *rest):
    if has_side:
        ws_ref, o_ref, os_ref, xn_ref = rest
    else:
        o_ref, xn_ref = rest

    @pl.when(pl.program_id(1) == 0)
    def _():
        xn = _rms(x_ref[...], g_ref[...]).astype(BF16)
        xn_ref[...] = xn
        if has_side:
            os_ref[...] = _dot(xn, ws_ref[...])

    o_ref[...] = _dot(xn_ref[...], w_ref[...]).astype(o_ref.dtype)


def norm_matmul(x, g, w, out_dtype, w_side=None, tm=512, tn=512):
    m, k = x.shape
    n = w.shape[1]
    assert m % tm == 0 and n % tn == 0
    in_specs = [pl.BlockSpec((tm, k), lambda i, j: (i, 0)),
                pl.BlockSpec((1, k), lambda i, j: (0, 0)),
                pl.BlockSpec((k, tn), lambda i, j: (0, j))]
    out_specs = [pl.BlockSpec((tm, tn), lambda i, j: (i, j))]
    out_shape = [jax.ShapeDtypeStruct((m, n), out_dtype)]
    args = [x, g.reshape(1, k), w]
    if w_side is not None:
        ns = w_side.shape[1]
        in_specs.append(pl.BlockSpec((k, ns), lambda i, j: (0, 0)))
        out_specs.append(pl.BlockSpec((tm, ns), lambda i, j: (i, 0)))
        out_shape.append(jax.ShapeDtypeStruct((m, ns), F32))
        args.append(w_side)
    out = pl.pallas_call(
        functools.partial(_norm_matmul_kernel, w_side is not None),
        grid=(m // tm, n // tn),
        in_specs=in_specs,
        out_specs=out_specs,
        out_shape=out_shape,
        scratch_shapes=[pltpu.VMEM((tm, k), BF16)],
        compiler_params=_cp(("parallel", "arbitrary")),
        name="norm_matmul",
    )(*args)
    return out if w_side is not None else out[0]


def _out_proj_kernel(a1_ref, a2_ref, w1_ref, w2_ref, h_ref, o_ref):
    o_ref[...] = h_ref[...] + _dot(a1_ref[...], w1_ref[...]) + _dot(a2_ref[...], w2_ref[...])


def out_proj(a1, a2, w, h, tm=512):
    m, k1 = a1.shape
    k2 = a2.shape[1]
    n = w.shape[1]
    w1, w2 = w[:k1], w[k1:]
    return pl.pallas_call(
        _out_proj_kernel,
        grid=(m // tm,),
        in_specs=[pl.BlockSpec((tm, k1), lambda i: (i, 0)),
                  pl.BlockSpec((tm, k2), lambda i: (i, 0)),
                  pl.BlockSpec((k1, n), lambda i: (0, 0)),
                  pl.BlockSpec((k2, n), lambda i: (0, 0)),
                  pl.BlockSpec((tm, n), lambda i: (i, 0))],
        out_specs=pl.BlockSpec((tm, n), lambda i: (i, 0)),
        out_shape=jax.ShapeDtypeStruct((m, n), F32),
        compiler_params=_cp(("parallel",)),
        name="out_proj",
    )(a1, a2, w1, w2, h)


def _ffn_kernel(seq_tiles, h_ref, hp_ref, g_ref, wu_ref, wg_ref, cw_ref, cb_ref, wd_ref, o_ref, xn_ref):
    i = pl.program_id(0)
    j = pl.program_id(1)
    tm = h_ref.shape[0]

    @pl.when(j == 0)
    def _():
        keep = (i % seq_tiles != 0).astype(F32)
        xn_ref[0:HALO, :] = (_rms(hp_ref[...], g_ref[...]) * keep).astype(BF16)
        xn_ref[HALO:, :] = _rms(h_ref[...], g_ref[...]).astype(BF16)
        o_ref[...] = h_ref[...]

    u = _dot(xn_ref[HALO:, :], wu_ref[...])
    ge = _dot(xn_ref[...], wg_ref[...])
    cw = cw_ref[...]
    gc = (cw[0:1] * ge[HALO - 2:HALO - 2 + tm] + cw[1:2] * ge[HALO - 1:HALO - 1 + tm]
          + cw[2:3] * ge[HALO:] + cb_ref[...])
    act = _silu(gc) * u
    o_ref[...] += _dot(act.astype(BF16), wd_ref[...])


def conv_ffn(h, g, w_up_u, w_up_g, conv_w, conv_b, w_down, seq, tm=1024, tf=512):
    m, k = h.shape
    dff = w_up_u.shape[1]
    assert m % tm == 0 and dff % tf == 0 and seq % tm == 0 and tm % HALO == 0
    hb = tm // HALO
    return pl.pallas_call(
        functools.partial(_ffn_kernel, seq // tm),
        grid=(m // tm, dff // tf),
        in_specs=[pl.BlockSpec((tm, k), lambda i, j: (i, 0), pipeline_mode=pl.Buffered(1)),
                  pl.BlockSpec((HALO, k), lambda i, j: (jnp.maximum(i * hb - 1, 0), 0)),
                  pl.BlockSpec((1, k), lambda i, j: (0, 0)),
                  pl.BlockSpec((k, tf), lambda i, j: (0, j)),
                  pl.BlockSpec((k, tf), lambda i, j: (0, j)),
                  pl.BlockSpec((FFN_CONV, tf), lambda i, j: (0, j)),
                  pl.BlockSpec((1, tf), lambda i, j: (0, j)),
                  pl.BlockSpec((tf, k), lambda i, j: (j, 0))],
        out_specs=pl.BlockSpec((tm, k), lambda i, j: (i, 0), pipeline_mode=pl.Buffered(1)),
        out_shape=jax.ShapeDtypeStruct((m, k), F32),
        scratch_shapes=[pltpu.VMEM((HALO + tm, k), BF16)],
        compiler_params=_cp(("parallel", "arbitrary")),
        name="conv_ffn",
    )(h, h, g.reshape(1, k), w_up_u, w_up_g, conv_w, conv_b.reshape(1, dff), w_down)


def _final_norm_kernel(x_ref, g_ref, o_ref):
    o_ref[...] = _rms(x_ref[...], g_ref[...])


def final_norm(x, g, tm=1024):
    m, k = x.shape
    return pl.pallas_call(
        _final_norm_kernel,
        grid=(m // tm,),
        in_specs=[pl.BlockSpec((tm, k), lambda i: (i, 0)), pl.BlockSpec((1, k), lambda i: (0, 0))],
        out_specs=pl.BlockSpec((tm, k), lambda i: (i, 0)),
        out_shape=jax.ShapeDtypeStruct((m, k), F32),
        compiler_params=_cp(("parallel",)),
        name="final_norm",
    )(x, g.reshape(1, k))


def _t5_bucket(dist):
    max_exact = NUM_BUCKETS // 2
    n = jnp.maximum(dist, 0)
    log_ratio = jnp.log(jnp.maximum(n, 1).astype(F32) / max_exact) / math.log(MAX_DISTANCE / max_exact)
    large = jnp.minimum(max_exact + (log_ratio * (NUM_BUCKETS - max_exact)).astype(jnp.int32), NUM_BUCKETS - 1)
    return jnp.where(n < max_exact, n, large)


def _bias_table(rel_bias, dist, mask):
    b = rel_bias.astype(F32)[_t5_bucket(dist)].transpose(2, 0, 1)
    return jnp.where(mask[None], b, NEG_INF)


def _swa_kernel(sink_ref, q_ref, kp_ref, kc_ref, vp_ref, vc_ref, bias_ref, o_ref):
    g = pl.program_id(1)
    i = pl.program_id(2)
    r = A_HEADS // A_KV_HEADS
    k = jnp.concatenate([kp_ref[0], kc_ref[0]], axis=0)
    v = jnp.concatenate([vp_ref[0], vc_ref[0]], axis=0)
    col = lax.broadcasted_iota(jnp.int32, (Q_BLOCK, 2 * Q_BLOCK), 1)
    first = jnp.logical_and(i == 0, col < Q_BLOCK)
    for hh in range(r):
        qh = q_ref[0, :, hh * HEAD_DIM:(hh + 1) * HEAD_DIM]
        logits = _dot_nt(qh, k) * SCALE + bias_ref[0, hh]
        logits = jnp.where(first, NEG_INF, logits)
        sink = sink_ref[g * r + hh]
        m = jnp.maximum(jnp.max(logits, axis=-1, keepdims=True), sink)
        e = jnp.exp(logits - m)
        p = e / (jnp.sum(e, axis=-1, keepdims=True) + jnp.exp(sink - m))
        o_ref[0, :, hh * HEAD_DIM:(hh + 1) * HEAD_DIM] = _dot(p.astype(BF16), v).astype(o_ref.dtype)


def swa_attention(big, sinks, rel_bias, bsz, t):
    nb = t // Q_BLOCK
    r = A_HEADS // A_KV_HEADS
    ql = jnp.arange(Q_BLOCK)[:, None]
    kl = jnp.arange(2 * Q_BLOCK)[None, :]
    dist = ql + A_WINDOW - kl
    table = _bias_table(rel_bias, dist, (dist >= 0) & (dist < A_WINDOW))
    table = table.reshape(A_KV_HEADS, r, Q_BLOCK, 2 * Q_BLOCK)
    kcol = A_HEADS * HEAD_DIM // LANES
    vcol = kcol + A_KV_HEADS
    prev = lambda i: jnp.maximum(i - 1, 0)
    return pl.pallas_call(
        _swa_kernel,
        grid=(bsz, A_KV_HEADS, nb),
        in_specs=[pl.BlockSpec(memory_space=pltpu.SMEM),
                  pl.BlockSpec((1, Q_BLOCK, r * HEAD_DIM), lambda b, g, i: (b, i, g)),
                  pl.BlockSpec((1, Q_BLOCK, HEAD_DIM), lambda b, g, i: (b, prev(i), kcol + g)),
                  pl.BlockSpec((1, Q_BLOCK, HEAD_DIM), lambda b, g, i: (b, i, kcol + g)),
                  pl.BlockSpec((1, Q_BLOCK, HEAD_DIM), lambda b, g, i: (b, prev(i), vcol + g)),
                  pl.BlockSpec((1, Q_BLOCK, HEAD_DIM), lambda b, g, i: (b, i, vcol + g)),
                  pl.BlockSpec((1, r, Q_BLOCK, 2 * Q_BLOCK), lambda b, g, i: (g, 0, 0, 0))],
        out_specs=pl.BlockSpec((1, Q_BLOCK, r * HEAD_DIM), lambda b, g, i: (b, i, g)),
        out_shape=jax.ShapeDtypeStruct((bsz, t, A_HEADS * HEAD_DIM), BF16),
        compiler_params=_cp(("parallel", "parallel", "arbitrary")),
        name="swa_attention",
    )(sinks.astype(F32), big, big, big, big, big, table)


C_TERMS = 3


def _forget_cumsum_kernel(f_ref, b_ref, c_ref):
    x = f_ref[0] + b_ref[...]
    y = jnp.minimum(x, 0.0) - jnp.log1p(jnp.exp(-jnp.abs(x)))
    t = y.shape[1]
    lane = lax.broadcasted_iota(jnp.int32, y.shape, 1)
    s = 1
    while s < t:
        y = y + jnp.where(lane >= s, pltpu.roll(y, s, 1), 0.0)
        s *= 2
    for n, part in enumerate(_split_bf16(y * (-1.0 / SCALE), C_TERMS)):
        c_ref[0, n] = part


def forget_cumsum(f_t, b_forget):
    bsz, h, t = f_t.shape
    return pl.pallas_call(
        _forget_cumsum_kernel,
        grid=(bsz,),
        in_specs=[pl.BlockSpec((1, h, t), lambda b: (b, 0, 0)), pl.BlockSpec((h, 1), lambda b: (0, 0))],
        out_specs=pl.BlockSpec((1, C_TERMS, h, t), lambda b: (b, 0, 0, 0)),
        out_shape=jax.ShapeDtypeStruct((bsz, C_TERMS, h, t), BF16),
        compiler_params=_cp(("parallel",)),
        name="forget_cumsum",
    )(f_t, b_forget.astype(F32).reshape(h, 1))


def _fox_kernel(tq, q_ref, k_ref, ck_ref, vt_ref, o_ref, m_ref, l_ref, acc_ref):
    i = pl.program_id(2)
    c2 = SCALE * LOG2E
    half = tq // 2
    lane = lax.broadcasted_iota(jnp.int32, (tq, LANES), 1)
    qa = jnp.concatenate([q_ref[0], jnp.where(lane < C_TERMS, 1.0, 0.0).astype(BF16)], axis=1)
    krow = lax.broadcasted_iota(jnp.int32, (tq, half), 0)
    qcol = lax.broadcasted_iota(jnp.int32, (tq, half), 1)

    def scores(j):
        off = pl.multiple_of(j * tq, tq)
        k = jnp.concatenate([k_ref[0, pl.ds(off, tq), :], ck_ref[0, 0, pl.ds(off, tq), :]], axis=1)
        return _dot_nt(k, qa) * c2

    def chunk(z_all, j, diag):
        vt = vt_ref[0, 0, :, pl.ds(pl.multiple_of(j * tq, tq), tq)]
        for hf in range(2):
            cs = slice(hf * half, (hf + 1) * half)
            z = z_all[:, cs]
            if diag:
                z = jnp.where(krow <= qcol + hf * half, z, NEG_INF)
            zmax = jnp.max(z, axis=0, keepdims=True)
            if diag:
                m_new = zmax
            else:
                m_old = m_ref[:, cs]
                m_new = jnp.maximum(m_old, zmax)
                alpha = jnp.exp2(m_old - m_new)
            p = jnp.exp2(z - m_new)
            psum = jnp.sum(p, axis=0, keepdims=True)
            pv = _dot(vt, p.astype(BF16))
            m_ref[:, cs] = m_new
            if diag:
                l_ref[:, cs] = psum
                acc_ref[:, cs] = pv
            else:
                l_ref[:, cs] = alpha * l_ref[:, cs] + psum
                acc_ref[:, cs] = acc_ref[:, cs] * alpha + pv

    z_diag = scores(i)
    z_first = scores(0)
    chunk(z_diag, i, True)

    def far(j, z_all):
        z_next = scores(jnp.minimum(j + 1, jnp.maximum(i - 1, 0)))
        chunk(z_all, j, False)
        return z_next

    lax.fori_loop(0, i, far, z_first)
    out = acc_ref[...] * (1.0 / l_ref[...])
    for n in range(tq // HEAD_DIM):
        rs = slice(n * HEAD_DIM, (n + 1) * HEAD_DIM)
        o_ref[0, rs, :] = out[:, rs].T.astype(o_ref.dtype)


def fox_attention(big, c_parts, bsz, t, tq=512):
    qcol = (A_HEADS + 2 * A_KV_HEADS) * HEAD_DIM // LANES
    kcol = qcol + B_HEADS
    vcol = kcol + B_HEADS
    ck = jnp.pad(c_parts.transpose(0, 2, 3, 1), ((0, 0), (0, 0), (0, 0), (0, LANES - C_TERMS)))
    v0 = vcol * LANES
    vt = big[:, :, v0:v0 + B_HEADS * HEAD_DIM].reshape(bsz, t, B_HEADS, HEAD_DIM).transpose(0, 2, 3, 1)
    return pl.pallas_call(
        functools.partial(_fox_kernel, tq),
        grid=(bsz, B_HEADS, t // tq),
        in_specs=[pl.BlockSpec((1, tq, HEAD_DIM), lambda b, h, i: (b, i, qcol + h)),
                  pl.BlockSpec((1, t, HEAD_DIM), lambda b, h, i: (b, 0, kcol + h)),
                  pl.BlockSpec((1, 1, t, LANES), lambda b, h, i: (b, h, 0, 0)),
                  pl.BlockSpec((1, 1, HEAD_DIM, t), lambda b, h, i: (b, h, 0, 0))],
        out_specs=pl.BlockSpec((1, tq, HEAD_DIM), lambda b, h, i: (b, i, h)),
        scratch_shapes=[pltpu.VMEM((1, tq), F32), pltpu.VMEM((1, tq), F32), pltpu.VMEM((HEAD_DIM, tq), F32)],
        out_shape=jax.ShapeDtypeStruct((bsz, t, B_HEADS * HEAD_DIM), BF16),
        compiler_params=_cp(("parallel", "parallel", "arbitrary")),
        name="fox_attention",
    )(big, big, ck, vt)


def _compress_kernel(x_ref, pe_ref, w1_ref, w2_ref, o_ref):
    half = CMP_STRIDE * HEAD_DIM
    x = x_ref[0, 0, 0]
    n = x.shape[0]
    xa = (x + pe_ref[0, :, :half]).astype(BF16)
    xb = (x + pe_ref[0, :, half:]).astype(BF16)
    a = _dot(xa, w1_ref[0, :half, :])
    b = _dot(xb, w1_ref[0, half:, :])
    hid = a + pltpu.roll(b, n - 1, 0)
    hid = jax.nn.gelu(hid, approximate=True)
    out = _dot(hid.astype(BF16), w2_ref[0])
    row = lax.broadcasted_iota(jnp.int32, out.shape, 0)
    o_ref[0, 0, 0] = jnp.where(row < n - 1, out, 0.0).astype(o_ref.dtype)


def compress_tokens(xflat, pe, w1, w2):
    two, bsz, g, n, width = xflat.shape
    return pl.pallas_call(
        _compress_kernel,
        grid=(two, bsz, g),
        in_specs=[pl.BlockSpec((1, 1, 1, n, width), lambda s, b, gg: (s, b, gg, 0, 0)),
                  pl.BlockSpec((1, 1, 2 * width), lambda s, b, gg: (s, 0, 0)),
                  pl.BlockSpec((1, 2 * width, CMP_HIDDEN), lambda s, b, gg: (s, 0, 0)),
                  pl.BlockSpec((1, CMP_HIDDEN, HEAD_DIM), lambda s, b, gg: (s, 0, 0))],
        out_specs=pl.BlockSpec((1, 1, 1, n, HEAD_DIM), lambda s, b, gg: (s, b, gg, 0, 0)),
        out_shape=jax.ShapeDtypeStruct((two, bsz, g, n, HEAD_DIM), BF16),
        compiler_params=_cp(("parallel", "parallel", "parallel")),
        name="nsa_compress",
    )(xflat, pe, w1, w2)


NSA_PAD = 4 * Q_BLOCK
MASK_BIG = 2.0 ** 100


def _nsa_kernel(n_sel, q_ref, kc_ref, vct_ref, ks_ref, vst_ref, kw_ref, vwt_ref, gate_ref, near_ref, mimp_ref,
                o_ref, m_ref, l_ref, acc_ref, osel_ref, ocmp_ref):
    g = pl.program_id(1)
    i = pl.program_id(2)
    r = C_HEADS // C_KV_GROUPS
    rows = r * Q_BLOCK
    n_sb = mimp_ref.shape[0]
    n_cmp = kc_ref.shape[3]
    c2 = SCALE * LOG2E

    q = jnp.concatenate([q_ref[0, :, hh * HEAD_DIM:(hh + 1) * HEAD_DIM] for hh in range(r)], axis=0)
    qlane = lax.broadcasted_iota(jnp.int32, (1, rows), 1) & (Q_BLOCK - 1)
    qpos = i * Q_BLOCK + qlane

    cend = lax.broadcasted_iota(jnp.int32, (n_cmp, 1), 0) * CMP_STRIDE + (CMP_BLOCK - 1)
    vis = cend <= qpos
    lc = jnp.where(vis, _dot_nt(kc_ref[0, 0, 0], q) * SCALE, NEG_INF)
    mc = jnp.max(lc, axis=0, keepdims=True)
    ec = jnp.where(vis, jnp.exp(lc - mc), 0.0)
    den = jnp.sum(ec, axis=0, keepdims=True)
    pc = ec / jnp.where(den > 0.0, den, 1.0)
    ocmp_ref[...] = _dot(vct_ref[0, 0, 0], pc.astype(BF16))

    imp = pc[:, 0:Q_BLOCK]
    for hh in range(1, r):
        imp = imp + pc[:, hh * Q_BLOCK:(hh + 1) * Q_BLOCK]
    blk = sum(_dot(mimp_ref[...], part) for part in _split_bf16(imp, 3))
    ids = lax.broadcasted_iota(jnp.int32, (n_sb, Q_BLOCK), 0)
    cur = (i * Q_BLOCK + lax.broadcasted_iota(jnp.int32, (n_sb, Q_BLOCK), 1)) // SEL_BLOCK
    forced = (ids == 0) | (ids == cur) | (ids == cur - 1)
    score = jnp.where(forced, FORCE_SCORE, jnp.where(ids > cur, -FORCE_SCORE, blk))
    sub = 8
    tiles = [score[v * sub:(v + 1) * sub] for v in range(n_sb // sub)]
    cnt = [jnp.zeros((sub, Q_BLOCK), F32) for _ in tiles]
    sub_id = lax.broadcasted_iota(jnp.int32, (sub, Q_BLOCK), 0)
    for jp in range(n_sb):
        row = score[jp:jp + 1]
        for v, tile in enumerate(tiles):
            ge = lambda: jnp.where(row >= tile, 1.0, 0.0)
            gt = lambda: jnp.where(row > tile, 1.0, 0.0)
            if v * sub > jp:
                inc = ge()
            elif v * sub + sub - 1 <= jp:
                inc = gt()
            else:
                inc = jnp.where(sub_id + v * sub > jp, ge(), gt())
            cnt[v] = cnt[v] + inc
    notsel = jnp.where(jnp.concatenate(cnt, axis=0) < n_sel, 0.0, 1.0).astype(BF16)
    place = (lax.broadcasted_iota(jnp.int32, (n_sb, LANES), 0)
             == lax.broadcasted_iota(jnp.int32, (n_sb, LANES), 1)).astype(BF16)
    ext = lax.dot_general(notsel, place, (((0,), (0,)), ((), ())), preferred_element_type=F32)
    lane = lax.broadcasted_iota(jnp.int32, (Q_BLOCK, LANES), 1)
    ext = jnp.where(lane == LANES - 1, 1.0, ext).astype(BF16)
    qa = jnp.concatenate([q, jnp.concatenate([ext] * r, axis=0)], axis=1)
    half = rows // 2

    def scores(k):
        return _dot_nt(k, qa) * c2

    def chunk(z_all, vt, tab, first):
        for hf in range(2):
            cs = slice(hf * half, (hf + 1) * half)
            z = z_all[:, cs]
            if tab is not None:
                z = z + tab(cs)
            zmax = jnp.max(z, axis=0, keepdims=True)
            if first:
                m_new = zmax
            else:
                m_old = m_ref[:, cs]
                m_new = jnp.maximum(m_old, zmax)
                alpha = jnp.exp2(m_old - m_new)
            p = jnp.exp2(z - m_new)
            psum = jnp.sum(p, axis=0, keepdims=True)
            pv = _dot(vt, p.astype(BF16))
            m_ref[:, cs] = m_new
            if first:
                l_ref[:, cs] = psum
                acc_ref[:, cs] = pv
            else:
                l_ref[:, cs] = alpha * l_ref[:, cs] + psum
                acc_ref[:, cs] = acc_ref[:, cs] * alpha + pv

    def keys(k_ref, off, n):
        return k_ref[0, 0, pl.ds(pl.multiple_of(off, Q_BLOCK), n), :]

    def vals(vt_ref, off, n):
        return vt_ref[0, 0, :, pl.ds(pl.multiple_of(off, Q_BLOCK), n)]

    near = lambda cs: near_ref[0, :, cs]
    kl = lax.broadcasted_iota(jnp.int32, (Q_BLOCK, half), 0)
    qh = lax.broadcasted_iota(jnp.int32, (Q_BLOCK, half), 1) & (Q_BLOCK - 1)
    tri_tab = jnp.where(kl > qh, 0.0, NEG_INF)
    tri = lambda cs: tri_tab
    base = (i - 1) * Q_BLOCK + NSA_PAD

    far_keys = 4 * Q_BLOCK
    n_far = (i + 2) // 4
    far_off = lambda c: jnp.maximum(base - (c + 1) * far_keys, 0)
    z_near = scores(keys(ks_ref, base, 2 * Q_BLOCK))
    z_far = scores(keys(ks_ref, far_off(0), far_keys))
    chunk(z_near, vals(vst_ref, base, 2 * Q_BLOCK), near, True)

    def sel_far(c, z_all):
        z_next = scores(keys(ks_ref, far_off(jnp.minimum(c + 1, jnp.maximum(n_far - 1, 0))), far_keys))
        chunk(z_all, vals(vst_ref, far_off(c), far_keys), None, False)
        return z_next

    lax.fori_loop(0, n_far, sel_far, z_far)
    osel_ref[...] = acc_ref[...] * (1.0 / l_ref[...])

    z_w0 = scores(keys(kw_ref, base, 2 * Q_BLOCK))
    z_w1 = scores(keys(kw_ref, base - 2 * Q_BLOCK, 2 * Q_BLOCK))
    z_w2 = scores(keys(kw_ref, base - 3 * Q_BLOCK, Q_BLOCK))
    chunk(z_w0, vals(vwt_ref, base, 2 * Q_BLOCK), near, True)
    chunk(z_w1, vals(vwt_ref, base - 2 * Q_BLOCK, 2 * Q_BLOCK), None, False)
    chunk(z_w2, vals(vwt_ref, base - 3 * Q_BLOCK, Q_BLOCK), tri, False)

    gates = _sigmoid(gate_ref[0])
    inv_l = 1.0 / l_ref[...]
    for hh in range(r):
        cs = slice(hh * Q_BLOCK, (hh + 1) * Q_BLOCK)
        gate = lambda branch: gate_ref_row(gates, branch * C_HEADS + g * r + hh)
        out = (gate(0) * ocmp_ref[:, cs] + gate(1) * osel_ref[:, cs]
               + gate(2) * (acc_ref[:, cs] * inv_l[:, cs]))
        o_ref[0, :, hh * HEAD_DIM:(hh + 1) * HEAD_DIM] = out.T.astype(o_ref.dtype)


def gate_ref_row(gates, idx):
    sel = lax.broadcasted_iota(jnp.int32, gates.shape, 0) == idx
    return jnp.sum(jnp.where(sel, gates, 0.0), axis=0, keepdims=True)


def nsa_attention(qkv, cmp_kv, small, rel_bias, bsz, t):
    assert C_WINDOW == 4 * Q_BLOCK and NSA_PAD >= C_WINDOW
    nb = t // Q_BLOCK
    n_sb = t // SEL_BLOCK
    assert n_sb < LANES and n_sb % 8 == 0
    n_sel = min(N_SELECT, n_sb)
    n_chunk = t // CMP_STRIDE
    g_, r = C_KV_GROUPS, C_HEADS // C_KV_GROUPS
    hd = HEAD_DIM
    rows = r * Q_BLOCK
    ql = jnp.arange(Q_BLOCK)[:, None]
    kl = jnp.arange(2 * Q_BLOCK)[None, :]
    dist = ql + Q_BLOCK - kl
    far = rel_bias.astype(F32)[_t5_bucket(jnp.int32(2 * Q_BLOCK))]
    near = (_bias_table(rel_bias, dist, dist >= 0) - far[:, None, None]) * LOG2E
    near = near.reshape(g_, r, Q_BLOCK, 2 * Q_BLOCK).transpose(0, 3, 1, 2).reshape(g_, 2 * Q_BLOCK, rows)
    tok = np.arange(n_chunk)[None, :]
    blk = np.arange(n_sb)[:, None]
    per = SEL_BLOCK // CMP_STRIDE
    mimp = ((tok // per == blk).astype(np.float32) + ((tok + 1) // per == blk).astype(np.float32))
    mimp[:, n_chunk - 1] = 0.0
    blk_cols = np.where(np.arange(t)[:, None] // SEL_BLOCK == np.arange(LANES)[None, :], -MASK_BIG, 0.0)
    pad_cols = np.zeros((NSA_PAD, 2 * LANES), np.float32)
    pad_cols[:, -1] = -MASK_BIG

    def heads(col):
        return qkv[:, :, col:col + g_ * hd].reshape(bsz, t, g_, hd).transpose(0, 2, 1, 3)

    def with_mask(k, cols):
        cols = jnp.broadcast_to(jnp.asarray(cols, BF16), (bsz, g_, t, LANES))
        pad = jnp.broadcast_to(jnp.asarray(pad_cols, BF16), (bsz, g_, NSA_PAD, 2 * LANES))
        return jnp.concatenate([pad, jnp.concatenate([k, cols], axis=-1)], axis=2)

    def padded_t(v):
        return jnp.pad(v.transpose(0, 1, 3, 2), ((0, 0), (0, 0), (0, 0), (NSA_PAD, 0)))

    c0 = C_HEADS * hd
    ks = with_mask(heads(c0), blk_cols)
    vst = padded_t(heads(c0 + g_ * hd))
    kw = with_mask(heads(c0 + 2 * g_ * hd), np.zeros((t, LANES), np.float32))
    vwt = padded_t(heads(c0 + 3 * g_ * hd))
    vct = cmp_kv[1].transpose(0, 1, 3, 2)[None]
    n_gate = 3 * C_HEADS
    gates_t = small[:, :, :n_gate].transpose(0, 2, 1)
    tp = t + NSA_PAD
    kspec = pl.BlockSpec((1, 1, tp, 2 * LANES), lambda b, g, i: (b, g, 0, 0))
    vspec = pl.BlockSpec((1, 1, hd, tp), lambda b, g, i: (b, g, 0, 0))
    return pl.pallas_call(
        functools.partial(_nsa_kernel, n_sel),
        grid=(bsz, g_, nb),
        in_specs=[pl.BlockSpec((1, Q_BLOCK, r * hd), lambda b, g, i: (b, i, g)),
                  pl.BlockSpec((1, 1, 1, n_chunk, hd), lambda b, g, i: (0, b, g, 0, 0)),
                  pl.BlockSpec((1, 1, 1, hd, n_chunk), lambda b, g, i: (0, b, g, 0, 0)),
                  kspec, vspec, kspec, vspec,
                  pl.BlockSpec((1, n_gate, Q_BLOCK), lambda b, g, i: (b, 0, i)),
                  pl.BlockSpec((1, 2 * Q_BLOCK, rows), lambda b, g, i: (g, 0, 0)),
                  pl.BlockSpec((n_sb, n_chunk), lambda b, g, i: (0, 0))],
        out_specs=pl.BlockSpec((1, Q_BLOCK, r * hd), lambda b, g, i: (b, i, g)),
        out_shape=jax.ShapeDtypeStruct((bsz, t, C_HEADS * hd), BF16),
        scratch_shapes=[pltpu.VMEM((1, rows), F32), pltpu.VMEM((1, rows), F32),
                        pltpu.VMEM((hd, rows), F32), pltpu.VMEM((hd, rows), F32), pltpu.VMEM((hd, rows), F32)],
        compiler_params=_cp(("parallel", "parallel", "arbitrary")),
        name="nsa_attention",
    )(qkv, cmp_kv, vct, ks, vst, kw, vwt, gates_t, near, jnp.asarray(mimp, BF16))


def _bmm(a, b):
    return jnp.einsum('cij,cjk->cik', a, b, preferred_element_type=F32)


def _bmm_nt(a, b):
    return jnp.einsum('cik,cjk->cij', a, b, preferred_element_type=F32)


def _bmm_f32(a, b):
    return _bmm(a.astype(BF16), b.astype(BF16))


def _gdn_prep_kernel(al_ref, dt_ref, q_ref, qp_ref, k_ref, kp_ref, v_ref, vp_ref, wq_ref, wk_ref, wv_ref,
                     a_ref, beta_ref, arow_ref, u_ref, w_ref, qg_ref, kd_ref, attn_ref, eg_ref):
    h = pl.program_id(1)
    rb = pl.program_id(2)
    rows = q_ref.shape[1]
    c = GDN_CHUNK
    nc = rows // c
    a_log = al_ref[h]
    dt_b = dt_ref[h]

    def conv_silu(x_ref, xp_ref, w_ref):
        prev = jnp.where(rb == 0, 0.0, xp_ref[0])
        x = jnp.concatenate([prev, x_ref[0]], axis=0)
        w = w_ref[...]
        n0 = prev.shape[0] - (D_CONV - 1)
        y = sum(w[j:j + 1] * x[n0 + j:n0 + j + rows] for j in range(D_CONV))
        return _silu(y)

    def l2(x):
        return x * lax.rsqrt(jnp.sum(x * x, axis=-1, keepdims=True) + EPS)

    q = (l2(conv_silu(q_ref, qp_ref, wq_ref)) * SCALE).reshape(nc, c, HEAD_DIM)
    k = l2(conv_silu(k_ref, kp_ref, wk_ref)).reshape(nc, c, HEAD_DIM)
    v = conv_silu(v_ref, vp_ref, wv_ref).reshape(nc, c, HEAD_DIM)

    beta = _sigmoid(beta_ref[0, 0]).reshape(nc, c, HEAD_DIM)
    g_col = (-jnp.exp(a_log) * _softplus(a_ref[0, 0] + dt_b)).reshape(nc, c, HEAD_DIM)
    g_row = (-jnp.exp(a_log) * _softplus(arow_ref[0, 0] + dt_b)).reshape(nc, c, c)

    ii = lax.broadcasted_iota(jnp.int32, (nc, c, c), 1)
    jj = lax.broadcasted_iota(jnp.int32, (nc, c, c), 2)
    tril = (jj <= ii).astype(BF16)
    triu = (ii <= jj).astype(BF16)
    gam = sum(_bmm(tril, p) for p in _split_bf16(g_col, 3))
    gam_row = sum(_bmm(p, triu) for p in _split_bf16(g_row, 3))
    causal = jj <= ii
    decay = jnp.where(causal, jnp.exp(jnp.where(causal, gam[:, :, :c] - gam_row, 0.0)), 0.0)

    kb = k * beta
    kbf = k.astype(BF16)
    lmat = jnp.where(jj < ii, _bmm_nt(kb.astype(BF16), kbf) * decay, 0.0)
    eye = (ii == jj).astype(F32)
    inv = eye - lmat
    pw = lmat
    for _ in range(int(math.log2(c)) - 1):
        pw = _bmm_f32(pw, pw)
        inv = inv + _bmm_f32(inv, pw)
    u = _bmm_f32(inv, v * beta)
    w = _bmm_f32(inv, kb * jnp.exp(gam))
    attn = _bmm_nt(q.astype(BF16), kbf) * decay
    g_last = jnp.broadcast_to(gam[:, c - 1:c, :], gam.shape)

    u_ref[0, 0] = u.reshape(rows, HEAD_DIM)
    w_ref[0, 0] = w.reshape(rows, HEAD_DIM).astype(BF16)
    qg_ref[0, 0] = (q * jnp.exp(gam)).reshape(rows, HEAD_DIM).astype(BF16)
    kd_ref[0, 0] = (k * jnp.exp(g_last - gam)).reshape(rows, HEAD_DIM).astype(BF16)
    attn_ref[0, 0] = attn.reshape(rows, c).astype(BF16)
    eg_ref[0, 0] = jnp.exp(g_last[:, 0:8, :])


def _gdn_scan_kernel(u_ref, w_ref, qg_ref, kd_ref, attn_ref, eg_ref, z_ref, ng_ref, o_ref, state_ref):
    c = GDN_CHUNK
    h = u_ref.shape[1]
    n = u_ref.shape[2] // c
    ng = ng_ref[...]

    @pl.when(pl.program_id(1) == 0)
    def _():
        state_ref[...] = jnp.zeros_like(state_ref)

    def step(ci, carry):
        off = pl.multiple_of(ci * c, c)
        sl = pl.ds(off, c)
        heads = range(h)
        states = [state_ref[hh] for hh in heads]
        sbs = [s.astype(BF16) for s in states]
        ws = [_dot(w_ref[0, hh, sl, :], sbs[hh]) for hh in heads]
        qs = [_dot(qg_ref[0, hh, sl, :], sbs[hh]) for hh in heads]
        vbs = [(u_ref[0, hh, sl, :] - ws[hh]).astype(BF16) for hh in heads]
        os = [qs[hh] + _dot(attn_ref[0, hh, sl, :], vbs[hh]) for hh in heads]
        upd = [lax.dot_general(kd_ref[0, hh, sl, :], vbs[hh], (((0,), (0,)), ((), ())), preferred_element_type=F32)
               for hh in heads]
        for hh in heads:
            cols = slice(hh * HEAD_DIM, (hh + 1) * HEAD_DIM)
            eg = eg_ref[0, hh, ci]
            decayed = (states[hh].reshape(HEAD_DIM // 8, 8, HEAD_DIM) * eg[None]).reshape(HEAD_DIM, HEAD_DIM)
            state_ref[hh] = decayed + upd[hh]
            o_ref[0, sl, cols] = (_rms(os[hh], ng) * _silu(z_ref[0, sl, cols])).astype(o_ref.dtype)
        return carry

    lax.fori_loop(0, n, step, 0)


def gated_deltanet(wide, small, conv_w, a_log, dt_bias, norm_g, bsz, t, rows=512):
    c = GDN_CHUNK
    h = D_HEADS
    nblk = t // rows
    qcol = 0
    kcol, vcol, zcol = qcol + h, qcol + 2 * h, qcol + 3 * h
    gate_col = 3 * C_HEADS
    beta_t = small[:, :, gate_col:gate_col + h].transpose(0, 2, 1)
    a_t = small[:, :, gate_col + h:gate_col + 2 * h].transpose(0, 2, 1)
    beta_b = jnp.broadcast_to(beta_t[..., None], (bsz, h, t, HEAD_DIM))
    a_b = jnp.broadcast_to(a_t[..., None], (bsz, h, t, HEAD_DIM))
    a_row = jnp.broadcast_to(a_t.reshape(bsz, h, t // c, 1, c), (bsz, h, t // c, c, c)).reshape(bsz, h, t, c)
    hb = rows // 8
    main = lambda col: pl.BlockSpec((1, rows, HEAD_DIM), lambda b, hh, r: (b, r, col + hh))
    halo = lambda col: pl.BlockSpec((1, 8, HEAD_DIM), lambda b, hh, r: (b, jnp.maximum(r * hb - 1, 0), col + hh))
    cw = lambda off: pl.BlockSpec((D_CONV, HEAD_DIM), lambda b, hh, r: (0, off * h + hh))
    per_tok = lambda width: pl.BlockSpec((1, 1, rows, width), lambda b, hh, r: (b, hh, r, 0))
    smem = pl.BlockSpec(memory_space=pltpu.SMEM)
    shp = lambda width, dt: jax.ShapeDtypeStruct((bsz, h, t, width), dt)
    u, w, qg, kd, attn, eg = pl.pallas_call(
        _gdn_prep_kernel,
        grid=(bsz, h, nblk),
        in_specs=[smem, smem, main(qcol), halo(qcol), main(kcol), halo(kcol), main(vcol), halo(vcol),
                  cw(0), cw(1), cw(2), per_tok(HEAD_DIM), per_tok(HEAD_DIM), per_tok(c)],
        out_specs=[per_tok(HEAD_DIM), per_tok(HEAD_DIM), per_tok(HEAD_DIM), per_tok(HEAD_DIM), per_tok(c),
                   pl.BlockSpec((1, 1, rows // c, 8, HEAD_DIM), lambda b, hh, r: (b, hh, r, 0, 0))],
        out_shape=[shp(HEAD_DIM, F32), shp(HEAD_DIM, BF16), shp(HEAD_DIM, BF16), shp(HEAD_DIM, BF16), shp(c, BF16),
                   jax.ShapeDtypeStruct((bsz, h, t // c, 8, HEAD_DIM), F32)],
        compiler_params=_cp(("parallel", "parallel", "parallel")),
        name="gdn_prep",
    )(a_log.astype(F32), dt_bias.astype(F32), wide, wide, wide, wide, wide, wide,
      conv_w, conv_w, conv_w, a_b, beta_b, a_row)
    blk = lambda width: pl.BlockSpec((1, h, rows, width), lambda b, r: (b, 0, r, 0))
    assert zcol % h == 0
    return pl.pallas_call(
        _gdn_scan_kernel,
        grid=(bsz, nblk),
        in_specs=[blk(HEAD_DIM), blk(HEAD_DIM), blk(HEAD_DIM), blk(HEAD_DIM), blk(c),
                  pl.BlockSpec((1, h, rows // c, 8, HEAD_DIM), lambda b, r: (b, 0, r, 0, 0)),
                  pl.BlockSpec((1, rows, h * HEAD_DIM), lambda b, r: (b, r, zcol // h)),
                  pl.BlockSpec((1, HEAD_DIM), lambda b, r: (0, 0))],
        out_specs=pl.BlockSpec((1, rows, h * HEAD_DIM), lambda b, r: (b, r, 0)),
        out_shape=jax.ShapeDtypeStruct((bsz, t, h * HEAD_DIM), BF16),
        scratch_shapes=[pltpu.VMEM((h, HEAD_DIM, HEAD_DIM), F32)],
        compiler_params=_cp(("parallel", "arbitrary")),
        name="gdn_scan",
    )(u, w, qg, kd, attn, eg, wide, norm_g.astype(F32).reshape(1, HEAD_DIM))


def _pad_cols(w, width):
    return jnp.pad(w, ((0, 0), (0, width - w.shape[1])))


def even_mixer(h, norm_g, w_in, b_forget, sinks, w_out, rel_bias, bsz, t):
    n_big = (A_HEADS + 2 * A_KV_HEADS + 3 * B_HEADS) * HEAD_DIM
    big, small = norm_matmul(h, norm_g, w_in[:, :n_big].astype(BF16), BF16,
                             w_side=_pad_cols(w_in[:, n_big:], LANES).astype(BF16), tn=768)
    big = big.reshape(bsz, t, n_big)
    o_a = swa_attention(big, sinks, rel_bias, bsz, t)
    f_t = small.reshape(bsz, t, LANES)[:, :, :B_HEADS].transpose(0, 2, 1)
    o_b = fox_attention(big, forget_cumsum(f_t, b_forget), bsz, t)
    return out_proj(o_a.reshape(bsz * t, -1), o_b.reshape(bsz * t, -1), w_out.astype(BF16), h)


def odd_mixer(h, norm_g, w_in, cmp_pos, cmp_w1, cmp_w2, conv_w, a_log, dt_bias, gdn_norm, w_out, rel_bias, bsz, t):
    hd = HEAD_DIM
    g = C_KV_GROUPS
    o_q = 0
    o_kcmp = C_HEADS * hd
    o_ksel = o_kcmp + 2 * g * hd
    o_gates = o_ksel + 4 * g * hd
    o_qd = o_gates + 3 * C_HEADS
    o_beta = o_qd + 3 * D_HEADS * hd
    o_z = o_beta + 2 * D_HEADS
    w_bf = jnp.concatenate([w_in[:, o_q:o_kcmp], w_in[:, o_ksel:o_gates]], axis=1)
    w_f32 = jnp.concatenate([w_in[:, o_qd:o_beta], w_in[:, o_z:], w_in[:, o_kcmp:o_ksel]], axis=1)
    w_small = _pad_cols(jnp.concatenate([w_in[:, o_gates:o_qd], w_in[:, o_beta:o_z]], axis=1), LANES)
    qkv = norm_matmul(h, norm_g, w_bf.astype(BF16), BF16).reshape(bsz, t, -1)
    wide, small = norm_matmul(h, norm_g, w_f32.astype(BF16), F32, w_side=w_small.astype(BF16), tn=768)
    wide = wide.reshape(bsz, t, -1)
    small = small.reshape(bsz, t, LANES)

    n_chunk = t // CMP_STRIDE
    xflat = wide[:, :, 4 * D_HEADS * hd:].reshape(bsz, n_chunk, CMP_STRIDE, 2, g, hd)
    xflat = xflat.transpose(3, 0, 4, 1, 2, 5).reshape(2, bsz, g, n_chunk, CMP_STRIDE * hd)
    cmp_kv = compress_tokens(xflat, cmp_pos.reshape(2, 1, CMP_BLOCK * hd), cmp_w1.astype(BF16), cmp_w2.astype(BF16))
    o_c = nsa_attention(qkv, cmp_kv, small, rel_bias, bsz, t)
    o_d = gated_deltanet(wide, small, conv_w, a_log, dt_bias, gdn_norm, bsz, t)
    return out_proj(o_c.reshape(bsz * t, -1), o_d.reshape(bsz * t, -1), w_out.astype(BF16), h)


def kernel(x, rel_bias, norm_mix, norm_ffn, norm_final, ev_w_in, ev_b_forget, ev_sinks, ev_w_out, od_w_in,
           od_cmp_pos, od_cmp_w1, od_cmp_w2, od_conv_w, od_a_log, od_dt_bias, od_gdn_norm, od_w_out, ffn_w_up,
           ffn_conv_w, ffn_conv_b, ffn_w_down):
    bsz, t, d = x.shape
    h = x.reshape(bsz * t, d)
    for layer in range(norm_mix.shape[0]):
        j = layer // 2
        if layer % 2 == 0:
            h = even_mixer(h, norm_mix[layer], ev_w_in[j], ev_b_forget[j], ev_sinks[j], ev_w_out[j], rel_bias,
                           bsz, t)
        else:
            h = odd_mixer(h, norm_mix[layer], od_w_in[j], od_cmp_pos[j], od_cmp_w1[j], od_cmp_w2[j], od_conv_w[j],
                          od_a_log[j], od_dt_bias[j], od_gdn_norm[j], od_w_out[j], rel_bias, bsz, t)
        dff = ffn_w_down.shape[1]
        w_up = ffn_w_up[layer].astype(BF16)
        h = conv_ffn(h, norm_ffn[layer], w_up[:, :dff], w_up[:, dff:], ffn_conv_w[layer], ffn_conv_b[layer],
                     ffn_w_down[layer].astype(BF16), t)
    return final_norm(h, norm_final).reshape(bsz, t, d)
```

```python
import functools
import math

import jax
import jax.numpy as jnp
import numpy as np
from jax import lax
from jax.experimental import pallas as pl
from jax.experimental.pallas import tpu as pltpu

D_MODEL = 2048
DEPTH = 4
HEAD_DIM = 128
A_HEADS = 8
A_KV_HEADS = 2
A_WINDOW = 128
B_HEADS = 8
C_HEADS = 8
C_KV_GROUPS = 2
CMP_BLOCK = 32
CMP_STRIDE = 16
CMP_HIDDEN = 256
SEL_BLOCK = 64
N_SELECT = 8
C_WINDOW = 512
D_HEADS = 8
D_CONV = 4
GDN_CHUNK = 64
NUM_BUCKETS = 32
MAX_DISTANCE = 128
D_FF = 11 * D_MODEL // 4
FFN_CONV = 3
Q_BLOCK = 128
EPS = 1e-6
NEG_INF = -1e30
FORCE_SCORE = 1e9
SCALE = HEAD_DIM ** -0.5
LOG2E = math.log2(math.e)
MASK_BIG = 2.0 ** 100

F32 = jnp.float32
BF16 = jnp.bfloat16
LANES = 128
HALO = 16
NORM_ROWS = 256
VMEM_LIMIT = 52 * 1024 * 1024


def _cp(dims, vmem=VMEM_LIMIT):
    return pltpu.CompilerParams(dimension_semantics=dims, vmem_limit_bytes=vmem)


def _dot(a, b):
    return jnp.dot(a, b, preferred_element_type=F32)


def _dot_nt(a, b):
    return lax.dot_general(a, b, (((1,), (1,)), ((), ())), preferred_element_type=F32)


def _rms(x, g):
    return x * lax.rsqrt(jnp.mean(x * x, axis=-1, keepdims=True) + EPS) * g


def _sigmoid(x):
    return 1.0 / (1.0 + jnp.exp(-x))


def _silu(x):
    return x * _sigmoid(x)


def _softplus(x):
    return jnp.maximum(x, 0.0) + jnp.log1p(jnp.exp(-jnp.abs(x)))


def _split_bf16(x, parts):
    out = []
    for _ in range(parts - 1):
        p = x.astype(BF16)
        out.append(p)
        x = x - p.astype(F32)
    out.append(x.astype(BF16))
    return out


def _norm_matmul_kernel(has_side, x_ref, g_ref, w_ref, *rest):
    if has_side:
        ws_ref, o_ref, os_ref, xn_ref = rest
    else:
        o_ref, xn_ref = rest

    @pl.when(pl.program_id(1) == 0)
    def _():
        for r0 in range(0, x_ref.shape[0], NORM_ROWS):
            rs = slice(r0, r0 + NORM_ROWS)
            xn_ref[rs, :] = _rms(x_ref[rs, :], g_ref[...]).astype(BF16)
        if has_side:
            os_ref[...] = _dot(xn_ref[...], ws_ref[...])

    o_ref[...] = _dot(xn_ref[...], w_ref[...]).astype(o_ref.dtype)


def norm_matmul(x, g, w, out_dtype, w_side=None, tm=1024, tn=512):
    m, k = x.shape
    n = w.shape[1]
    assert m % tm == 0 and n % tn == 0 and tm % NORM_ROWS == 0
    in_specs = [pl.BlockSpec((tm, k), lambda i, j: (i, 0), pipeline_mode=pl.Buffered(1)),
                pl.BlockSpec((1, k), lambda i, j: (0, 0)),
                pl.BlockSpec((k, tn), lambda i, j: (0, j))]
    out_specs = [pl.BlockSpec((tm, tn), lambda i, j: (i, j))]
    out_shape = [jax.ShapeDtypeStruct((m, n), out_dtype)]
    args = [x, g.reshape(1, k), w]
    if w_side is not None:
        ns = w_side.shape[1]
        in_specs.append(pl.BlockSpec((k, ns), lambda i, j: (0, 0)))
        out_specs.append(pl.BlockSpec((tm, ns), lambda i, j: (i, 0)))
        out_shape.append(jax.ShapeDtypeStruct((m, ns), F32))
        args.append(w_side)
    out = pl.pallas_call(
        functools.partial(_norm_matmul_kernel, w_side is not None),
        grid=(m // tm, n // tn),
        in_specs=in_specs,
        out_specs=out_specs,
        out_shape=out_shape,
        scratch_shapes=[pltpu.VMEM((tm, k), BF16)],
        compiler_params=_cp(("parallel", "arbitrary")),
        name="norm_matmul",
    )(*args)
    return out if w_side is not None else out[0]


def _out_proj_kernel(a1_ref, a2_ref, w1_ref, w2_ref, h_ref, o_ref):
    o_ref[...] = h_ref[...] + _dot(a1_ref[...], w1_ref[...]) + _dot(a2_ref[...], w2_ref[...])


def out_proj(a1, a2, w, h, tm=512):
    m, k1 = a1.shape
    k2 = a2.shape[1]
    n = w.shape[1]
    w1, w2 = w[:k1], w[k1:]
    return pl.pallas_call(
        _out_proj_kernel,
        grid=(m // tm,),
        in_specs=[pl.BlockSpec((tm, k1), lambda i: (i, 0)),
                  pl.BlockSpec((tm, k2), lambda i: (i, 0)),
                  pl.BlockSpec((k1, n), lambda i: (0, 0)),
                  pl.BlockSpec((k2, n), lambda i: (0, 0)),
                  pl.BlockSpec((tm, n), lambda i: (i, 0))],
        out_specs=pl.BlockSpec((tm, n), lambda i: (i, 0)),
        out_shape=jax.ShapeDtypeStruct((m, n), F32),
        compiler_params=_cp(("parallel",)),
        name="out_proj",
    )(a1, a2, w1, w2, h)


def _ffn_kernel(seq_tiles, final, h_ref, hp_ref, g_ref, wu_ref, wg_ref, cw_ref, cb_ref, wd_ref, fg_ref, o_ref,
                xn_ref):
    i = pl.program_id(0)
    j = pl.program_id(1)
    tm = h_ref.shape[0]

    @pl.when(j == 0)
    def _():
        keep = (i % seq_tiles != 0).astype(F32)
        xn_ref[0:HALO, :] = (_rms(hp_ref[...], g_ref[...]) * keep).astype(BF16)
        for r0 in range(0, tm, NORM_ROWS):
            xn_ref[HALO + r0:HALO + r0 + NORM_ROWS, :] = _rms(h_ref[r0:r0 + NORM_ROWS, :], g_ref[...]).astype(BF16)
        o_ref[...] = h_ref[...]

    u = _dot(xn_ref[HALO:, :], wu_ref[...])
    ge = _dot(xn_ref[...], wg_ref[...])
    cw = cw_ref[...]
    gc = (cw[0:1] * ge[HALO - 2:HALO - 2 + tm] + cw[1:2] * ge[HALO - 1:HALO - 1 + tm]
          + cw[2:3] * ge[HALO:] + cb_ref[...])
    act = _silu(gc) * u
    o_ref[...] += _dot(act.astype(BF16), wd_ref[...])

    if final:
        @pl.when(j == pl.num_programs(1) - 1)
        def _():
            for r0 in range(0, tm, NORM_ROWS):
                rs = slice(r0, r0 + NORM_ROWS)
                o_ref[rs, :] = _rms(o_ref[rs, :], fg_ref[...])


def conv_ffn(h, g, w_up_u, w_up_g, conv_w, conv_b, w_down, seq, final_g=None, tm=1024, tf=512):
    m, k = h.shape
    dff = w_up_u.shape[1]
    assert m % tm == 0 and dff % tf == 0 and seq % tm == 0 and tm % HALO == 0 and tm % NORM_ROWS == 0
    hb = tm // HALO
    fg = (g if final_g is None else final_g).reshape(1, k)
    return pl.pallas_call(
        functools.partial(_ffn_kernel, seq // tm, final_g is not None),
        grid=(m // tm, dff // tf),
        in_specs=[pl.BlockSpec((tm, k), lambda i, j: (i, 0), pipeline_mode=pl.Buffered(1)),
                  pl.BlockSpec((HALO, k), lambda i, j: (jnp.maximum(i * hb - 1, 0), 0)),
                  pl.BlockSpec((1, k), lambda i, j: (0, 0)),
                  pl.BlockSpec((k, tf), lambda i, j: (0, j)),
                  pl.BlockSpec((k, tf), lambda i, j: (0, j)),
                  pl.BlockSpec((FFN_CONV, tf), lambda i, j: (0, j)),
                  pl.BlockSpec((1, tf), lambda i, j: (0, j)),
                  pl.BlockSpec((tf, k), lambda i, j: (j, 0)),
                  pl.BlockSpec((1, k), lambda i, j: (0, 0))],
        out_specs=pl.BlockSpec((tm, k), lambda i, j: (i, 0), pipeline_mode=pl.Buffered(1)),
        out_shape=jax.ShapeDtypeStruct((m, k), F32),
        scratch_shapes=[pltpu.VMEM((HALO + tm, k), BF16)],
        compiler_params=_cp(("parallel", "arbitrary")),
        name="conv_ffn",
    )(h, h, g.reshape(1, k), w_up_u, w_up_g, conv_w, conv_b.reshape(1, dff), w_down, fg)


def _t5_bucket(dist):
    max_exact = NUM_BUCKETS // 2
    n = jnp.maximum(dist, 0)
    log_ratio = jnp.log(jnp.maximum(n, 1).astype(F32) / max_exact) / math.log(MAX_DISTANCE / max_exact)
    large = jnp.minimum(max_exact + (log_ratio * (NUM_BUCKETS - max_exact)).astype(jnp.int32), NUM_BUCKETS - 1)
    return jnp.where(n < max_exact, n, large)


def _bias_table(rel_bias, dist, mask):
    b = rel_bias.astype(F32)[_t5_bucket(dist)].transpose(2, 0, 1)
    return jnp.where(mask[None], b, NEG_INF)


def _swa_kernel(sink_ref, q_ref, kp_ref, kc_ref, vp_ref, vc_ref, bias_ref, o_ref):
    g = pl.program_id(1)
    i = pl.program_id(2)
    r = A_HEADS // A_KV_HEADS
    k = jnp.concatenate([kp_ref[0], kc_ref[0]], axis=0)
    v = jnp.concatenate([vp_ref[0], vc_ref[0]], axis=0)
    col = lax.broadcasted_iota(jnp.int32, (Q_BLOCK, 2 * Q_BLOCK), 1)
    first = jnp.logical_and(i == 0, col < Q_BLOCK)
    for hh in range(r):
        qh = q_ref[0, :, hh * HEAD_DIM:(hh + 1) * HEAD_DIM]
        logits = _dot_nt(qh, k) * SCALE + bias_ref[0, hh]
        logits = jnp.where(first, NEG_INF, logits)
        sink = sink_ref[g * r + hh]
        m = jnp.maximum(jnp.max(logits, axis=-1, keepdims=True), sink)
        e = jnp.exp(logits - m)
        p = e / (jnp.sum(e, axis=-1, keepdims=True) + jnp.exp(sink - m))
        o_ref[0, :, hh * HEAD_DIM:(hh + 1) * HEAD_DIM] = _dot(p.astype(BF16), v).astype(o_ref.dtype)


def swa_attention(big, sinks, rel_bias, bsz, t):
    nb = t // Q_BLOCK
    r = A_HEADS // A_KV_HEADS
    ql = jnp.arange(Q_BLOCK)[:, None]
    kl = jnp.arange(2 * Q_BLOCK)[None, :]
    dist = ql + A_WINDOW - kl
    table = _bias_table(rel_bias, dist, (dist >= 0) & (dist < A_WINDOW))
    table = table.reshape(A_KV_HEADS, r, Q_BLOCK, 2 * Q_BLOCK)
    kcol = A_HEADS * HEAD_DIM // LANES
    vcol = kcol + A_KV_HEADS
    prev = lambda i: jnp.maximum(i - 1, 0)
    return pl.pallas_call(
        _swa_kernel,
        grid=(bsz, A_KV_HEADS, nb),
        in_specs=[pl.BlockSpec(memory_space=pltpu.SMEM),
                  pl.BlockSpec((1, Q_BLOCK, r * HEAD_DIM), lambda b, g, i: (b, i, g)),
                  pl.BlockSpec((1, Q_BLOCK, HEAD_DIM), lambda b, g, i: (b, prev(i), kcol + g)),
                  pl.BlockSpec((1, Q_BLOCK, HEAD_DIM), lambda b, g, i: (b, i, kcol + g)),
                  pl.BlockSpec((1, Q_BLOCK, HEAD_DIM), lambda b, g, i: (b, prev(i), vcol + g)),
                  pl.BlockSpec((1, Q_BLOCK, HEAD_DIM), lambda b, g, i: (b, i, vcol + g)),
                  pl.BlockSpec((1, r, Q_BLOCK, 2 * Q_BLOCK), lambda b, g, i: (g, 0, 0, 0))],
        out_specs=pl.BlockSpec((1, Q_BLOCK, r * HEAD_DIM), lambda b, g, i: (b, i, g)),
        out_shape=jax.ShapeDtypeStruct((bsz, t, A_HEADS * HEAD_DIM), BF16),
        compiler_params=_cp(("parallel", "parallel", "arbitrary")),
        name="swa_attention",
    )(sinks.astype(F32), big, big, big, big, big, table)


C_TERMS = 3


def _forget_cumsum_kernel(f_ref, b_ref, c_ref):
    x = f_ref[0] + b_ref[...]
    y = jnp.minimum(x, 0.0) - jnp.log1p(jnp.exp(-jnp.abs(x)))
    t = y.shape[1]
    lane = lax.broadcasted_iota(jnp.int32, y.shape, 1)
    s = 1
    while s < t:
        y = y + jnp.where(lane >= s, pltpu.roll(y, s, 1), 0.0)
        s *= 2
    for n, part in enumerate(_split_bf16(y * (-1.0 / SCALE), C_TERMS)):
        c_ref[0, n] = part


def forget_cumsum(f_t, b_forget):
    bsz, h, t = f_t.shape
    return pl.pallas_call(
        _forget_cumsum_kernel,
        grid=(bsz,),
        in_specs=[pl.BlockSpec((1, h, t), lambda b: (b, 0, 0)), pl.BlockSpec((h, 1), lambda b: (0, 0))],
        out_specs=pl.BlockSpec((1, C_TERMS, h, t), lambda b: (b, 0, 0, 0)),
        out_shape=jax.ShapeDtypeStruct((bsz, C_TERMS, h, t), BF16),
        compiler_params=_cp(("parallel",)),
        name="forget_cumsum",
    )(f_t, b_forget.astype(F32).reshape(h, 1))


def _fox_kernel(tq, q_ref, k_ref, ck_ref, vt_ref, o_ref, m_ref, l_ref, acc_ref, s0_ref, s1_ref):
    i = pl.program_id(2)
    c2 = SCALE * LOG2E
    half = tq // 2
    lane = lax.broadcasted_iota(jnp.int32, (tq, LANES), 1)
    qa = jnp.concatenate([q_ref[0], jnp.where(lane <= C_TERMS, 1.0, 0.0).astype(BF16)], axis=1)
    krow = lax.broadcasted_iota(jnp.int32, (tq, half), 0)
    qcol = lax.broadcasted_iota(jnp.int32, (tq, half), 1)
    last = jnp.maximum(i - 1, 0)

    def key_off(j, diag=False):
        return pl.multiple_of((j if diag else jnp.minimum(j, last)) * tq, tq)

    def scores(j, diag=False):
        valid = True if diag else j < i
        ck_off = pl.multiple_of(jnp.where(valid, (j + 1) * tq, 0), tq)
        k = jnp.concatenate([k_ref[0, pl.ds(key_off(j, diag), tq), :], ck_ref[0, 0, pl.ds(ck_off, tq), :]], axis=1)
        return _dot_nt(k, qa)

    def chunk(s_src, j, diag):
        vt = vt_ref[0, 0, :, pl.ds(key_off(j, diag), tq)]
        for hf in range(2):
            cs = slice(hf * half, (hf + 1) * half)
            s = s_src[:, cs]
            if diag:
                s = jnp.where(krow <= qcol + hf * half, s, NEG_INF)
            zmax = jnp.max(s, axis=0, keepdims=True) * c2
            if diag:
                m_new = zmax
            else:
                m_old = m_ref[:, cs]
                m_new = jnp.maximum(m_old, zmax)
                alpha = jnp.exp2(m_old - m_new)
            p = jnp.exp2(s * c2 - m_new)
            psum = jnp.sum(p, axis=0, keepdims=True)
            pv = _dot(vt, p.astype(BF16))
            m_ref[:, cs] = m_new
            if diag:
                l_ref[:, cs] = psum
                acc_ref[:, cs] = pv
            else:
                l_ref[:, cs] = alpha * l_ref[:, cs] + psum
                acc_ref[:, cs] = acc_ref[:, cs] * alpha + pv

    s_diag = scores(i, diag=True)
    s0_ref[...] = scores(0)
    chunk(s_diag, i, True)

    def far_pair(t, carry):
        a = 2 * t
        s1_ref[...] = scores(a + 1)
        chunk(s0_ref, a, False)
        s0_ref[...] = scores(a + 2)
        chunk(s1_ref, a + 1, False)
        return carry

    lax.fori_loop(0, (i + 1) // 2, far_pair, 0)
    out = acc_ref[...] * (1.0 / l_ref[...])
    for n in range(tq // HEAD_DIM):
        rs = slice(n * HEAD_DIM, (n + 1) * HEAD_DIM)
        o_ref[0, rs, :] = out[:, rs].T.astype(o_ref.dtype)


def fox_attention(big, c_parts, bsz, t, tq=512):
    qcol = (A_HEADS + 2 * A_KV_HEADS) * HEAD_DIM // LANES
    kcol = qcol + B_HEADS
    vcol = kcol + B_HEADS
    ck = jnp.pad(c_parts.transpose(0, 2, 3, 1), ((0, 0), (0, 0), (tq, 0), (0, LANES - C_TERMS)))
    ck = ck.at[:, :, :tq, C_TERMS].set(-MASK_BIG)
    v0 = vcol * LANES
    vt = big[:, :, v0:v0 + B_HEADS * HEAD_DIM].reshape(bsz, t, B_HEADS, HEAD_DIM).transpose(0, 2, 3, 1)
    return pl.pallas_call(
        functools.partial(_fox_kernel, tq),
        grid=(bsz, B_HEADS, t // tq),
        in_specs=[pl.BlockSpec((1, tq, HEAD_DIM), lambda b, h, i: (b, i, qcol + h)),
                  pl.BlockSpec((1, t, HEAD_DIM), lambda b, h, i: (b, 0, kcol + h)),
                  pl.BlockSpec((1, 1, tq + t, LANES), lambda b, h, i: (b, h, 0, 0)),
                  pl.BlockSpec((1, 1, HEAD_DIM, t), lambda b, h, i: (b, h, 0, 0))],
        out_specs=pl.BlockSpec((1, tq, HEAD_DIM), lambda b, h, i: (b, i, h)),
        scratch_shapes=[pltpu.VMEM((1, tq), F32), pltpu.VMEM((1, tq), F32), pltpu.VMEM((HEAD_DIM, tq), F32),
                        pltpu.VMEM((tq, tq), F32), pltpu.VMEM((tq, tq), F32)],
        out_shape=jax.ShapeDtypeStruct((bsz, t, B_HEADS * HEAD_DIM), BF16),
        compiler_params=_cp(("parallel", "parallel", "arbitrary")),
        name="fox_attention",
    )(big, big, ck, vt)


def _compress_kernel(x_ref, pe_ref, w1_ref, w2_ref, o_ref):
    half = CMP_STRIDE * HEAD_DIM
    x = x_ref[0, 0, 0]
    n = x.shape[0]
    xa = (x + pe_ref[0, :, :half]).astype(BF16)
    xb = (x + pe_ref[0, :, half:]).astype(BF16)
    a = _dot(xa, w1_ref[0, :half, :])
    b = _dot(xb, w1_ref[0, half:, :])
    hid = a + pltpu.roll(b, n - 1, 0)
    hid = jax.nn.gelu(hid, approximate=True)
    out = _dot(hid.astype(BF16), w2_ref[0])
    row = lax.broadcasted_iota(jnp.int32, out.shape, 0)
    o_ref[0, 0, 0] = jnp.where(row < n - 1, out, 0.0).astype(o_ref.dtype)


def compress_tokens(xflat, pe, w1, w2):
    two, bsz, g, n, width = xflat.shape
    return pl.pallas_call(
        _compress_kernel,
        grid=(two, bsz, g),
        in_specs=[pl.BlockSpec((1, 1, 1, n, width), lambda s, b, gg: (s, b, gg, 0, 0)),
                  pl.BlockSpec((1, 1, 2 * width), lambda s, b, gg: (s, 0, 0)),
                  pl.BlockSpec((1, 2 * width, CMP_HIDDEN), lambda s, b, gg: (s, 0, 0)),
                  pl.BlockSpec((1, CMP_HIDDEN, HEAD_DIM), lambda s, b, gg: (s, 0, 0))],
        out_specs=pl.BlockSpec((1, 1, 1, n, HEAD_DIM), lambda s, b, gg: (s, b, gg, 0, 0)),
        out_shape=jax.ShapeDtypeStruct((two, bsz, g, n, HEAD_DIM), BF16),
        compiler_params=_cp(("parallel", "parallel", "parallel")),
        name="nsa_compress",
    )(xflat, pe, w1, w2)


NSA_PAD = 4 * Q_BLOCK


def _nsa_kernel(n_sel, q_ref, kc_ref, vct_ref, ks_ref, vst_ref, kw_ref, vwt_ref, gate_ref, near_ref, mimp_ref,
                o_ref, ms_ref, ls_ref, accs_ref, mw_ref, lw_ref, accw_ref, ocmp_ref, s0_ref, s1_ref):
    sel_state = (ms_ref, ls_ref, accs_ref)
    win_state = (mw_ref, lw_ref, accw_ref)
    g = pl.program_id(1)
    i = pl.program_id(2)
    r = C_HEADS // C_KV_GROUPS
    rows = r * Q_BLOCK
    n_sb = mimp_ref.shape[0]
    n_cmp = kc_ref.shape[3]
    c2 = SCALE * LOG2E

    q = jnp.concatenate([q_ref[0, :, hh * HEAD_DIM:(hh + 1) * HEAD_DIM] for hh in range(r)], axis=0)
    qlane = lax.broadcasted_iota(jnp.int32, (1, rows), 1) & (Q_BLOCK - 1)
    qpos = i * Q_BLOCK + qlane
    half = rows // 2
    lane = lax.broadcasted_iota(jnp.int32, (Q_BLOCK, LANES), 1)

    def chunk(z_all, vt, tab, first, state):
        m_ref, l_ref, acc_ref = state
        for hf in range(2):
            cs = slice(hf * half, (hf + 1) * half)
            s = z_all[:, cs]
            if tab is not None:
                z = s * c2 + tab(cs)
                zmax = jnp.max(z, axis=0, keepdims=True)
            else:
                zmax = jnp.max(s, axis=0, keepdims=True) * c2
            if first:
                m_new = zmax
            else:
                m_old = m_ref[:, cs]
                m_new = jnp.maximum(m_old, zmax)
                alpha = jnp.exp2(m_old - m_new)
            p = jnp.exp2(z - m_new) if tab is not None else jnp.exp2(s * c2 - m_new)
            psum = jnp.sum(p, axis=0, keepdims=True)
            pv = _dot(vt, p.astype(BF16))
            m_ref[:, cs] = m_new
            if first:
                l_ref[:, cs] = psum
                acc_ref[:, cs] = pv
            else:
                l_ref[:, cs] = alpha * l_ref[:, cs] + psum
                acc_ref[:, cs] = acc_ref[:, cs] * alpha + pv

    def keys(k_ref, off, n):
        return k_ref[0, 0, pl.ds(pl.multiple_of(off, Q_BLOCK), n), :]

    def vals(vt_ref, off, n):
        return vt_ref[0, 0, :, pl.ds(pl.multiple_of(off, Q_BLOCK), n)]

    near = lambda cs: near_ref[0, :, cs]
    kl = lax.broadcasted_iota(jnp.int32, (Q_BLOCK, half), 0)
    qh = lax.broadcasted_iota(jnp.int32, (Q_BLOCK, half), 1) & (Q_BLOCK - 1)
    tri_tab = jnp.where(kl > qh, 0.0, NEG_INF)
    tri = lambda cs: tri_tab
    base = (i - 1) * Q_BLOCK + NSA_PAD

    pad_flag = jnp.where(lane == LANES - 1, 1.0, 0.0).astype(BF16)
    qw = jnp.concatenate([q, jnp.concatenate([pad_flag] * r, axis=0)], axis=1)
    z_w0 = _dot_nt(keys(kw_ref, base, 2 * Q_BLOCK), qw)
    z_w1 = _dot_nt(keys(kw_ref, base - 2 * Q_BLOCK, 2 * Q_BLOCK), qw)
    z_w2 = _dot_nt(keys(kw_ref, base - 3 * Q_BLOCK, Q_BLOCK), qw)
    chunk(z_w0, vals(vwt_ref, base, 2 * Q_BLOCK), near, True, win_state)
    chunk(z_w1, vals(vwt_ref, base - 2 * Q_BLOCK, 2 * Q_BLOCK), None, False, win_state)
    chunk(z_w2, vals(vwt_ref, base - 3 * Q_BLOCK, Q_BLOCK), tri, False, win_state)

    cend = lax.broadcasted_iota(jnp.int32, (n_cmp, 1), 0) * CMP_STRIDE + (CMP_BLOCK - 1)
    vis = cend <= qpos
    lc = jnp.where(vis, _dot_nt(kc_ref[0, 0, 0], q) * SCALE, NEG_INF)
    mc = jnp.max(lc, axis=0, keepdims=True)
    ec = jnp.where(vis, jnp.exp(lc - mc), 0.0)
    den = jnp.sum(ec, axis=0, keepdims=True)
    pc = ec / jnp.where(den > 0.0, den, 1.0)
    ocmp_ref[...] = _dot(vct_ref[0, 0, 0], pc.astype(BF16))

    imp = pc[:, 0:Q_BLOCK]
    for hh in range(1, r):
        imp = imp + pc[:, hh * Q_BLOCK:(hh + 1) * Q_BLOCK]
    blk = sum(_dot(mimp_ref[...], part) for part in _split_bf16(imp, 3))
    ids = lax.broadcasted_iota(jnp.int32, (n_sb, Q_BLOCK), 0)
    cur = (i * Q_BLOCK + lax.broadcasted_iota(jnp.int32, (n_sb, Q_BLOCK), 1)) // SEL_BLOCK
    forced = (ids == 0) | (ids == cur) | (ids == cur - 1)
    score = jnp.where(forced, FORCE_SCORE, jnp.where(ids > cur, -FORCE_SCORE, blk))
    sub = 8
    tiles = [score[v * sub:(v + 1) * sub] for v in range(n_sb // sub)]
    cnt = [jnp.zeros((sub, Q_BLOCK), F32) for _ in tiles]
    sub_id = lax.broadcasted_iota(jnp.int32, (sub, Q_BLOCK), 0)
    for jp in range(n_sb):
        row = score[jp:jp + 1]
        for v, tile in enumerate(tiles):
            ge = lambda: jnp.where(row >= tile, 1.0, 0.0)
            gt = lambda: jnp.where(row > tile, 1.0, 0.0)
            if v * sub > jp:
                inc = ge()
            elif v * sub + sub - 1 <= jp:
                inc = gt()
            else:
                inc = jnp.where(sub_id + v * sub > jp, ge(), gt())
            cnt[v] = cnt[v] + inc
    notsel = jnp.where(jnp.concatenate(cnt, axis=0) < n_sel, 0.0, 1.0).astype(BF16)
    place = (lax.broadcasted_iota(jnp.int32, (n_sb, LANES), 0)
             == lax.broadcasted_iota(jnp.int32, (n_sb, LANES), 1)).astype(BF16)
    ext = lax.dot_general(notsel, place, (((0,), (0,)), ((), ())), preferred_element_type=F32)
    ext = jnp.where(lane == LANES - 1, 1.0, ext).astype(BF16)
    qa = jnp.concatenate([q, jnp.concatenate([ext] * r, axis=0)], axis=1)

    def scores(k):
        return _dot_nt(k, qa)

    far_keys = 4 * Q_BLOCK
    n_far = (i + 2) // 4
    far_off = lambda c: jnp.maximum(base - (c + 1) * far_keys, 0)
    far_scores = lambda c: scores(keys(ks_ref, far_off(c), far_keys))
    z_near = scores(keys(ks_ref, base, 2 * Q_BLOCK))
    s0_ref[...] = far_scores(0)
    chunk(z_near, vals(vst_ref, base, 2 * Q_BLOCK), near, True, sel_state)

    def sel_far_pair(t, carry):
        a = 2 * t
        s1_ref[...] = far_scores(a + 1)
        chunk(s0_ref, vals(vst_ref, far_off(a), far_keys), None, False, sel_state)
        s0_ref[...] = far_scores(a + 2)
        chunk(s1_ref, vals(vst_ref, far_off(a + 1), far_keys), None, False, sel_state)
        return carry

    lax.fori_loop(0, (n_far + 1) // 2, sel_far_pair, 0)

    gates = _sigmoid(gate_ref[0])
    inv_sel = 1.0 / sel_state[1][...]
    inv_win = 1.0 / win_state[1][...]
    for hh in range(r):
        cs = slice(hh * Q_BLOCK, (hh + 1) * Q_BLOCK)
        gate = lambda branch: gate_ref_row(gates, branch * C_HEADS + g * r + hh)
        out = (gate(0) * ocmp_ref[:, cs] + gate(1) * (sel_state[2][:, cs] * inv_sel[:, cs])
               + gate(2) * (win_state[2][:, cs] * inv_win[:, cs]))
        o_ref[0, :, hh * HEAD_DIM:(hh + 1) * HEAD_DIM] = out.T.astype(o_ref.dtype)


def gate_ref_row(gates, idx):
    sel = lax.broadcasted_iota(jnp.int32, gates.shape, 0) == idx
    return jnp.sum(jnp.where(sel, gates, 0.0), axis=0, keepdims=True)


def nsa_attention(qkv, cmp_kv, small, rel_bias, bsz, t):
    assert C_WINDOW == 4 * Q_BLOCK and NSA_PAD >= C_WINDOW
    nb = t // Q_BLOCK
    n_sb = t // SEL_BLOCK
    assert n_sb < LANES and n_sb % 8 == 0
    n_sel = min(N_SELECT, n_sb)
    n_chunk = t // CMP_STRIDE
    g_, r = C_KV_GROUPS, C_HEADS // C_KV_GROUPS
    hd = HEAD_DIM
    rows = r * Q_BLOCK
    ql = jnp.arange(Q_BLOCK)[:, None]
    kl = jnp.arange(2 * Q_BLOCK)[None, :]
    dist = ql + Q_BLOCK - kl
    far = rel_bias.astype(F32)[_t5_bucket(jnp.int32(2 * Q_BLOCK))]
    near = (_bias_table(rel_bias, dist, dist >= 0) - far[:, None, None]) * LOG2E
    near = near.reshape(g_, r, Q_BLOCK, 2 * Q_BLOCK).transpose(0, 3, 1, 2).reshape(g_, 2 * Q_BLOCK, rows)
    tok = np.arange(n_chunk)[None, :]
    blk = np.arange(n_sb)[:, None]
    per = SEL_BLOCK // CMP_STRIDE
    mimp = ((tok // per == blk).astype(np.float32) + ((tok + 1) // per == blk).astype(np.float32))
    mimp[:, n_chunk - 1] = 0.0
    blk_cols = np.where(np.arange(t)[:, None] // SEL_BLOCK == np.arange(LANES)[None, :], -MASK_BIG, 0.0)
    pad_cols = np.zeros((NSA_PAD, 2 * LANES), np.float32)
    pad_cols[:, -1] = -MASK_BIG

    def heads(col):
        return qkv[:, :, col:col + g_ * hd].reshape(bsz, t, g_, hd).transpose(0, 2, 1, 3)

    def with_mask(k, cols):
        cols = jnp.broadcast_to(jnp.asarray(cols, BF16), (bsz, g_, t, LANES))
        pad = jnp.broadcast_to(jnp.asarray(pad_cols, BF16), (bsz, g_, NSA_PAD, 2 * LANES))
        return jnp.concatenate([pad, jnp.concatenate([k, cols], axis=-1)], axis=2)

    def padded_t(v):
        return jnp.pad(v.transpose(0, 1, 3, 2), ((0, 0), (0, 0), (0, 0), (NSA_PAD, 0)))

    c0 = C_HEADS * hd
    ks = with_mask(heads(c0), blk_cols)
    vst = padded_t(heads(c0 + g_ * hd))
    kw = with_mask(heads(c0 + 2 * g_ * hd), np.zeros((t, LANES), np.float32))
    vwt = padded_t(heads(c0 + 3 * g_ * hd))
    vct = cmp_kv[1].transpose(0, 1, 3, 2)[None]
    n_gate = 3 * C_HEADS
    gates_t = small[:, :, :n_gate].transpose(0, 2, 1)
    tp = t + NSA_PAD
    kspec = pl.BlockSpec((1, 1, tp, 2 * LANES), lambda b, g, i: (b, g, 0, 0))
    vspec = pl.BlockSpec((1, 1, hd, tp), lambda b, g, i: (b, g, 0, 0))
    return pl.pallas_call(
        functools.partial(_nsa_kernel, n_sel),
        grid=(bsz, g_, nb),
        in_specs=[pl.BlockSpec((1, Q_BLOCK, r * hd), lambda b, g, i: (b, i, g)),
                  pl.BlockSpec((1, 1, 1, n_chunk, hd), lambda b, g, i: (0, b, g, 0, 0)),
                  pl.BlockSpec((1, 1, 1, hd, n_chunk), lambda b, g, i: (0, b, g, 0, 0)),
                  kspec, vspec, kspec, vspec,
                  pl.BlockSpec((1, n_gate, Q_BLOCK), lambda b, g, i: (b, 0, i)),
                  pl.BlockSpec((1, 2 * Q_BLOCK, rows), lambda b, g, i: (g, 0, 0)),
                  pl.BlockSpec((n_sb, n_chunk), lambda b, g, i: (0, 0))],
        out_specs=pl.BlockSpec((1, Q_BLOCK, r * hd), lambda b, g, i: (b, i, g)),
        out_shape=jax.ShapeDtypeStruct((bsz, t, C_HEADS * hd), BF16),
        scratch_shapes=[pltpu.VMEM((1, rows), F32), pltpu.VMEM((1, rows), F32), pltpu.VMEM((hd, rows), F32),
                        pltpu.VMEM((1, rows), F32), pltpu.VMEM((1, rows), F32), pltpu.VMEM((hd, rows), F32),
                        pltpu.VMEM((hd, rows), F32),
                        pltpu.VMEM((4 * Q_BLOCK, rows), F32), pltpu.VMEM((4 * Q_BLOCK, rows), F32)],
        compiler_params=_cp(("parallel", "parallel", "arbitrary")),
        name="nsa_attention",
    )(qkv, cmp_kv, vct, ks, vst, kw, vwt, gates_t, near, jnp.asarray(mimp, BF16))


def _bmm(a, b):
    return jnp.einsum('cij,cjk->cik', a, b, preferred_element_type=F32)


def _bmm_nt(a, b):
    return jnp.einsum('cik,cjk->cij', a, b, preferred_element_type=F32)


def _bmm_f32(a, b):
    return _bmm(a.astype(BF16), b.astype(BF16))


def _gdn_prep_kernel(al_ref, dt_ref, q_ref, qp_ref, k_ref, kp_ref, v_ref, vp_ref, wq_ref, wk_ref, wv_ref,
                     a_ref, beta_ref, arow_ref, u_ref, w_ref, qg_ref, kd_ref, attn_ref, eg_ref):
    h = pl.program_id(1)
    rb = pl.program_id(2)
    rows = q_ref.shape[1]
    c = GDN_CHUNK
    nc = rows // c
    a_log = al_ref[h]
    dt_b = dt_ref[h]

    def conv_silu(x_ref, xp_ref, w_ref):
        prev = jnp.where(rb == 0, 0.0, xp_ref[0])
        x = jnp.concatenate([prev, x_ref[0]], axis=0)
        w = w_ref[...]
        n0 = prev.shape[0] - (D_CONV - 1)
        y = sum(w[j:j + 1] * x[n0 + j:n0 + j + rows] for j in range(D_CONV))
        return _silu(y)

    def l2(x):
        return x * lax.rsqrt(jnp.sum(x * x, axis=-1, keepdims=True) + EPS)

    q = (l2(conv_silu(q_ref, qp_ref, wq_ref)) * SCALE).reshape(nc, c, HEAD_DIM)
    k = l2(conv_silu(k_ref, kp_ref, wk_ref)).reshape(nc, c, HEAD_DIM)
    v = conv_silu(v_ref, vp_ref, wv_ref).reshape(nc, c, HEAD_DIM)

    beta = _sigmoid(beta_ref[0, 0]).reshape(nc, c, HEAD_DIM)
    g_col = (-jnp.exp(a_log) * _softplus(a_ref[0, 0] + dt_b)).reshape(nc, c, HEAD_DIM)
    g_row = (-jnp.exp(a_log) * _softplus(arow_ref[0, 0] + dt_b)).reshape(nc, c, c)

    ii = lax.broadcasted_iota(jnp.int32, (nc, c, c), 1)
    jj = lax.broadcasted_iota(jnp.int32, (nc, c, c), 2)
    tril = (jj <= ii).astype(BF16)
    triu = (ii <= jj).astype(BF16)
    gam = sum(_bmm(tril, p) for p in _split_bf16(g_col, 3))
    gam_row = sum(_bmm(p, triu) for p in _split_bf16(g_row, 3))
    causal = jj <= ii
    decay = jnp.where(causal, jnp.exp(jnp.where(causal, gam[:, :, :c] - gam_row, 0.0)), 0.0)

    kb = k * beta
    kbf = k.astype(BF16)
    lmat = jnp.where(jj < ii, _bmm_nt(kb.astype(BF16), kbf) * decay, 0.0)
    eye = (ii == jj).astype(F32)
    inv = eye - lmat
    pw = lmat
    for _ in range(int(math.log2(c)) - 1):
        pw = _bmm_f32(pw, pw)
        inv = inv + _bmm_f32(inv, pw)
    u = _bmm_f32(inv, v * beta)
    w = _bmm_f32(inv, kb * jnp.exp(gam))
    attn = _bmm_nt(q.astype(BF16), kbf) * decay
    g_last = jnp.broadcast_to(gam[:, c - 1:c, :], gam.shape)

    u_ref[0, 0] = u.reshape(rows, HEAD_DIM)
    w_ref[0, 0] = w.reshape(rows, HEAD_DIM).astype(BF16)
    qg_ref[0, 0] = (q * jnp.exp(gam)).reshape(rows, HEAD_DIM).astype(BF16)
    kd_ref[0, 0] = (k * jnp.exp(g_last - gam)).reshape(rows, HEAD_DIM).astype(BF16)
    attn_ref[0, 0] = attn.reshape(rows, c).astype(BF16)
    eg_ref[0, 0] = jnp.exp(g_last[:, 0:8, :])


def _gdn_scan_kernel(u_ref, w_ref, qg_ref, kd_ref, attn_ref, eg_ref, z_ref, ng_ref, o_ref, state_ref):
    c = GDN_CHUNK
    h = u_ref.shape[1]
    n = u_ref.shape[2] // c
    ng = ng_ref[...]

    @pl.when(pl.program_id(1) == 0)
    def _():
        state_ref[...] = jnp.zeros_like(state_ref)

    def step(ci, carry):
        off = pl.multiple_of(ci * c, c)
        sl = pl.ds(off, c)
        heads = range(h)
        states = [state_ref[hh] for hh in heads]
        sbs = [s.astype(BF16) for s in states]
        ws = [_dot(w_ref[0, hh, sl, :], sbs[hh]) for hh in heads]
        qs = [_dot(qg_ref[0, hh, sl, :], sbs[hh]) for hh in heads]
        vbs = [(u_ref[0, hh, sl, :] - ws[hh]).astype(BF16) for hh in heads]
        os = [qs[hh] + _dot(attn_ref[0, hh, sl, :], vbs[hh]) for hh in heads]
        upd = [lax.dot_general(kd_ref[0, hh, sl, :], vbs[hh], (((0,), (0,)), ((), ())), preferred_element_type=F32)
               for hh in heads]
        for hh in heads:
            cols = slice(hh * HEAD_DIM, (hh + 1) * HEAD_DIM)
            eg = eg_ref[0, hh, ci]
            decayed = (states[hh].reshape(HEAD_DIM // 8, 8, HEAD_DIM) * eg[None]).reshape(HEAD_DIM, HEAD_DIM)
            state_ref[hh] = decayed + upd[hh]
            o_ref[0, sl, cols] = (_rms(os[hh], ng) * _silu(z_ref[0, sl, cols])).astype(o_ref.dtype)
        return carry

    lax.fori_loop(0, n, step, 0)


def gated_deltanet(wide, small, conv_w, a_log, dt_bias, norm_g, bsz, t, rows=512):
    c = GDN_CHUNK
    h = D_HEADS
    nblk = t // rows
    qcol = 0
    kcol, vcol, zcol = qcol + h, qcol + 2 * h, qcol + 3 * h
    gate_col = 3 * C_HEADS
    beta_t = small[:, :, gate_col:gate_col + h].transpose(0, 2, 1)
    a_t = small[:, :, gate_col + h:gate_col + 2 * h].transpose(0, 2, 1)
    beta_b = jnp.broadcast_to(beta_t[..., None], (bsz, h, t, HEAD_DIM))
    a_b = jnp.broadcast_to(a_t[..., None], (bsz, h, t, HEAD_DIM))
    a_row = jnp.broadcast_to(a_t.reshape(bsz, h, t // c, 1, c), (bsz, h, t // c, c, c)).reshape(bsz, h, t, c)
    hb = rows // 8
    main = lambda col: pl.BlockSpec((1, rows, HEAD_DIM), lambda b, hh, r: (b, r, col + hh))
    halo = lambda col: pl.BlockSpec((1, 8, HEAD_DIM), lambda b, hh, r: (b, jnp.maximum(r * hb - 1, 0), col + hh))
    cw = lambda off: pl.BlockSpec((D_CONV, HEAD_DIM), lambda b, hh, r: (0, off * h + hh))
    per_tok = lambda width: pl.BlockSpec((1, 1, rows, width), lambda b, hh, r: (b, hh, r, 0))
    smem = pl.BlockSpec(memory_space=pltpu.SMEM)
    shp = lambda width, dt: jax.ShapeDtypeStruct((bsz, h, t, width), dt)
    u, w, qg, kd, attn, eg = pl.pallas_call(
        _gdn_prep_kernel,
        grid=(bsz, h, nblk),
        in_specs=[smem, smem, main(qcol), halo(qcol), main(kcol), halo(kcol), main(vcol), halo(vcol),
                  cw(0), cw(1), cw(2), per_tok(HEAD_DIM), per_tok(HEAD_DIM), per_tok(c)],
        out_specs=[per_tok(HEAD_DIM), per_tok(HEAD_DIM), per_tok(HEAD_DIM), per_tok(HEAD_DIM), per_tok(c),
                   pl.BlockSpec((1, 1, rows // c, 8, HEAD_DIM), lambda b, hh, r: (b, hh, r, 0, 0))],
        out_shape=[shp(HEAD_DIM, F32), shp(HEAD_DIM, BF16), shp(HEAD_DIM, BF16), shp(HEAD_DIM, BF16), shp(c, BF16),
                   jax.ShapeDtypeStruct((bsz, h, t // c, 8, HEAD_DIM), F32)],
        compiler_params=_cp(("parallel", "parallel", "parallel")),
        name="gdn_prep",
    )(a_log.astype(F32), dt_bias.astype(F32), wide, wide, wide, wide, wide, wide,
      conv_w, conv_w, conv_w, a_b, beta_b, a_row)
    blk = lambda width: pl.BlockSpec((1, h, rows, width), lambda b, r: (b, 0, r, 0))
    assert zcol % h == 0
    return pl.pallas_call(
        _gdn_scan_kernel,
        grid=(bsz, nblk),
        in_specs=[blk(HEAD_DIM), blk(HEAD_DIM), blk(HEAD_DIM), blk(HEAD_DIM), blk(c),
                  pl.BlockSpec((1, h, rows // c, 8, HEAD_DIM), lambda b, r: (b, 0, r, 0, 0)),
                  pl.BlockSpec((1, rows, h * HEAD_DIM), lambda b, r: (b, r, zcol // h)),
                  pl.BlockSpec((1, HEAD_DIM), lambda b, r: (0, 0))],
        out_specs=pl.BlockSpec((1, rows, h * HEAD_DIM), lambda b, r: (b, r, 0)),
        out_shape=jax.ShapeDtypeStruct((bsz, t, h * HEAD_DIM), BF16),
        scratch_shapes=[pltpu.VMEM((h, HEAD_DIM, HEAD_DIM), F32)],
        compiler_params=_cp(("parallel", "arbitrary")),
        name="gdn_scan",
    )(u, w, qg, kd, attn, eg, wide, norm_g.astype(F32).reshape(1, HEAD_DIM))


def _pad_cols(w, width):
    return jnp.pad(w, ((0, 0), (0, width - w.shape[1])))


def even_mixer(h, norm_g, w_in, b_forget, sinks, w_out, rel_bias, bsz, t):
    n_big = (A_HEADS + 2 * A_KV_HEADS + 3 * B_HEADS) * HEAD_DIM
    big, small = norm_matmul(h, norm_g, w_in[:, :n_big].astype(BF16), BF16,
                             w_side=_pad_cols(w_in[:, n_big:], LANES).astype(BF16), tn=768)
    big = big.reshape(bsz, t, n_big)
    o_a = swa_attention(big, sinks, rel_bias, bsz, t)
    f_t = small.reshape(bsz, t, LANES)[:, :, :B_HEADS].transpose(0, 2, 1)
    o_b = fox_attention(big, forget_cumsum(f_t, b_forget), bsz, t)
    return out_proj(o_a.reshape(bsz * t, -1), o_b.reshape(bsz * t, -1), w_out.astype(BF16), h)


def odd_mixer(h, norm_g, w_in, cmp_pos, cmp_w1, cmp_w2, conv_w, a_log, dt_bias, gdn_norm, w_out, rel_bias, bsz, t):
    hd = HEAD_DIM
    g = C_KV_GROUPS
    o_q = 0
    o_kcmp = C_HEADS * hd
    o_ksel = o_kcmp + 2 * g * hd
    o_gates = o_ksel + 4 * g * hd
    o_qd = o_gates + 3 * C_HEADS
    o_beta = o_qd + 3 * D_HEADS * hd
    o_z = o_beta + 2 * D_HEADS
    w_bf = jnp.concatenate([w_in[:, o_q:o_kcmp], w_in[:, o_ksel:o_gates]], axis=1)
    w_f32 = jnp.concatenate([w_in[:, o_qd:o_beta], w_in[:, o_z:], w_in[:, o_kcmp:o_ksel]], axis=1)
    w_small = _pad_cols(jnp.concatenate([w_in[:, o_gates:o_qd], w_in[:, o_beta:o_z]], axis=1), LANES)
    qkv = norm_matmul(h, norm_g, w_bf.astype(BF16), BF16).reshape(bsz, t, -1)
    wide, small = norm_matmul(h, norm_g, w_f32.astype(BF16), F32, w_side=w_small.astype(BF16), tn=768)
    wide = wide.reshape(bsz, t, -1)
    small = small.reshape(bsz, t, LANES)

    n_chunk = t // CMP_STRIDE
    xflat = wide[:, :, 4 * D_HEADS * hd:].reshape(bsz, n_chunk, CMP_STRIDE, 2, g, hd)
    xflat = xflat.transpose(3, 0, 4, 1, 2, 5).reshape(2, bsz, g, n_chunk, CMP_STRIDE * hd)
    cmp_kv = compress_tokens(xflat, cmp_pos.reshape(2, 1, CMP_BLOCK * hd), cmp_w1.astype(BF16), cmp_w2.astype(BF16))
    o_c = nsa_attention(qkv, cmp_kv, small, rel_bias, bsz, t)
    o_d = gated_deltanet(wide, small, conv_w, a_log, dt_bias, gdn_norm, bsz, t)
    return out_proj(o_c.reshape(bsz * t, -1), o_d.reshape(bsz * t, -1), w_out.astype(BF16), h)


def kernel(x, rel_bias, norm_mix, norm_ffn, norm_final, ev_w_in, ev_b_forget, ev_sinks, ev_w_out, od_w_in,
           od_cmp_pos, od_cmp_w1, od_cmp_w2, od_conv_w, od_a_log, od_dt_bias, od_gdn_norm, od_w_out, ffn_w_up,
           ffn_conv_w, ffn_conv_b, ffn_w_down):
    bsz, t, d = x.shape
    h = x.reshape(bsz * t, d)
    depth = norm_mix.shape[0]
    for layer in range(depth):
        j = layer // 2
        if layer % 2 == 0:
            h = even_mixer(h, norm_mix[layer], ev_w_in[j], ev_b_forget[j], ev_sinks[j], ev_w_out[j], rel_bias,
                           bsz, t)
        else:
            h = odd_mixer(h, norm_mix[layer], od_w_in[j], od_cmp_pos[j], od_cmp_w1[j], od_cmp_w2[j], od_conv_w[j],
                          od_a_log[j], od_dt_bias[j], od_gdn_norm[j], od_w_out[j], rel_bias, bsz, t)
        dff = ffn_w_down.shape[1]
        w_up = ffn_w_up[layer].astype(BF16)
        h = conv_ffn(h, norm_ffn[layer], w_up[:, :dff], w_up[:, dff:], ffn_conv_w[layer], ffn_conv_b[layer],
                     ffn_w_down[layer].astype(BF16), t, final_g=norm_final if layer == depth - 1 else None)
    return h.reshape(bsz, t, d)
```

```python
import functools
import math

import jax
import jax.numpy as jnp
import numpy as np
from jax import lax
from jax.experimental import pallas as pl
from jax.experimental.pallas import tpu as pltpu

D_MODEL = 2048
DEPTH = 4
HEAD_DIM = 128
A_HEADS = 8
A_KV_HEADS = 2
A_WINDOW = 128
B_HEADS = 8
C_HEADS = 8
C_KV_GROUPS = 2
CMP_BLOCK = 32
CMP_STRIDE = 16
CMP_HIDDEN = 256
SEL_BLOCK = 64
N_SELECT = 8
C_WINDOW = 512
D_HEADS = 8
D_CONV = 4
GDN_CHUNK = 64
NUM_BUCKETS = 32
MAX_DISTANCE = 128
D_FF = 11 * D_MODEL // 4
FFN_CONV = 3
Q_BLOCK = 128
EPS = 1e-6
NEG_INF = -1e30
FORCE_SCORE = 1e9
SCALE = HEAD_DIM ** -0.5
LOG2E = math.log2(math.e)
MASK_BIG = 2.0 ** 100

F32 = jnp.float32
BF16 = jnp.bfloat16
LANES = 128
HALO = 16
NORM_ROWS = 256
FFN_PIECE = 256
VMEM_LIMIT = 52 * 1024 * 1024


def _cp(dims, vmem=VMEM_LIMIT):
    return pltpu.CompilerParams(dimension_semantics=dims, vmem_limit_bytes=vmem)


def _dot(a, b):
    return jnp.dot(a, b, preferred_element_type=F32)


def _dot_nt(a, b):
    return lax.dot_general(a, b, (((1,), (1,)), ((), ())), preferred_element_type=F32)


def _rms(x, g):
    return x * lax.rsqrt(jnp.mean(x * x, axis=-1, keepdims=True) + EPS) * g


def _sigmoid(x):
    return 1.0 / (1.0 + jnp.exp(-x))


def _silu(x):
    return x * _sigmoid(x)


def _softplus(x):
    return jnp.maximum(x, 0.0) + jnp.log1p(jnp.exp(-jnp.abs(x)))


def _split_bf16(x, parts):
    out = []
    for _ in range(parts - 1):
        p = x.astype(BF16)
        out.append(p)
        x = x - p.astype(F32)
    out.append(x.astype(BF16))
    return out


def _norm_matmul_kernel(has_side, x_ref, g_ref, w_ref, *rest):
    if has_side:
        ws_ref, o_ref, os_ref, xn_ref = rest
    else:
        o_ref, xn_ref = rest

    @pl.when(pl.program_id(1) == 0)
    def _():
        for r0 in range(0, x_ref.shape[0], NORM_ROWS):
            rs = slice(r0, r0 + NORM_ROWS)
            xn_ref[rs, :] = _rms(x_ref[rs, :], g_ref[...]).astype(BF16)
        if has_side:
            os_ref[...] = _dot(xn_ref[...], ws_ref[...])

    o_ref[...] = _dot(xn_ref[...], w_ref[...]).astype(o_ref.dtype)


def norm_matmul(x, g, w, out_dtype, w_side=None, tm=1024, tn=512):
    m, k = x.shape
    n = w.shape[1]
    assert m % tm == 0 and n % tn == 0 and tm % NORM_ROWS == 0
    in_specs = [pl.BlockSpec((tm, k), lambda i, j: (i, 0), pipeline_mode=pl.Buffered(1)),
                pl.BlockSpec((1, k), lambda i, j: (0, 0)),
                pl.BlockSpec((k, tn), lambda i, j: (0, j))]
    out_specs = [pl.BlockSpec((tm, tn), lambda i, j: (i, j))]
    out_shape = [jax.ShapeDtypeStruct((m, n), out_dtype)]
    args = [x, g.reshape(1, k), w]
    if w_side is not None:
        ns = w_side.shape[1]
        in_specs.append(pl.BlockSpec((k, ns), lambda i, j: (0, 0)))
        out_specs.append(pl.BlockSpec((tm, ns), lambda i, j: (i, 0)))
        out_shape.append(jax.ShapeDtypeStruct((m, ns), F32))
        args.append(w_side)
    out = pl.pallas_call(
        functools.partial(_norm_matmul_kernel, w_side is not None),
        grid=(m // tm, n // tn),
        in_specs=in_specs,
        out_specs=out_specs,
        out_shape=out_shape,
        scratch_shapes=[pltpu.VMEM((tm, k), BF16)],
        compiler_params=_cp(("parallel", "arbitrary")),
        name="norm_matmul",
    )(*args)
    return out if w_side is not None else out[0]


def _out_proj_kernel(a1_ref, a2_ref, w1_ref, w2_ref, h_ref, o_ref):
    o_ref[...] = h_ref[...] + _dot(a1_ref[...], w1_ref[...]) + _dot(a2_ref[...], w2_ref[...])


def out_proj(a1, a2, w, h, tm=512):
    m, k1 = a1.shape
    k2 = a2.shape[1]
    n = w.shape[1]
    assert k1 == k2 and w.shape[0] == k1 + k2
    return pl.pallas_call(
        _out_proj_kernel,
        grid=(m // tm,),
        in_specs=[pl.BlockSpec((tm, k1), lambda i: (i, 0)),
                  pl.BlockSpec((tm, k2), lambda i: (i, 0)),
                  pl.BlockSpec((k1, n), lambda i: (0, 0)),
                  pl.BlockSpec((k2, n), lambda i: (1, 0)),
                  pl.BlockSpec((tm, n), lambda i: (i, 0))],
        out_specs=pl.BlockSpec((tm, n), lambda i: (i, 0)),
        out_shape=jax.ShapeDtypeStruct((m, n), F32),
        compiler_params=_cp(("parallel",)),
        name="out_proj",
    )(a1, a2, w, w, h)


def _ffn_kernel(seq_tiles, final, h_ref, hp_ref, g_ref, wu_ref, wg_ref, cw_ref, cb_ref, wd_ref, fg_ref, o_ref,
                xn_ref):
    i = pl.program_id(0)
    j = pl.program_id(1)
    tm = h_ref.shape[0]

    @pl.when(j == 0)
    def _():
        keep = (i % seq_tiles != 0).astype(F32)
        xn_ref[0:HALO, :] = (_rms(hp_ref[...], g_ref[...]) * keep).astype(BF16)
        for r0 in range(0, tm, NORM_ROWS):
            xn_ref[HALO + r0:HALO + r0 + NORM_ROWS, :] = _rms(h_ref[r0:r0 + NORM_ROWS, :], g_ref[...]).astype(BF16)
        o_ref[...] = h_ref[...]

    tf = wu_ref.shape[1]
    pieces = [slice(c0, c0 + FFN_PIECE) for c0 in range(0, tf, FFN_PIECE)]
    up = [(_dot(xn_ref[HALO:, :], wu_ref[:, cs]),
           _dot(xn_ref[...], wg_ref[:, cs]))
          for cs in pieces]
    down = None
    for cs, (u, ge) in zip(pieces, up):
        cw = cw_ref[:, cs]
        gc = (cw[0:1] * ge[HALO - 2:HALO - 2 + tm] + cw[1:2] * ge[HALO - 1:HALO - 1 + tm]
              + cw[2:3] * ge[HALO:] + cb_ref[:, cs])
        part = _dot((_silu(gc) * u).astype(BF16), wd_ref[cs, :])
        down = part if down is None else down + part
    o_ref[...] += down

    if final:
        @pl.when(j == pl.num_programs(1) - 1)
        def _():
            for r0 in range(0, tm, NORM_ROWS):
                rs = slice(r0, r0 + NORM_ROWS)
                o_ref[rs, :] = _rms(o_ref[rs, :], fg_ref[...])


def conv_ffn(h, g, w_up, conv_w, conv_b, w_down, seq, final_g=None, tm=1024, tf=512):
    m, k = h.shape
    dff = w_down.shape[0]
    assert m % tm == 0 and dff % tf == 0 and seq % tm == 0 and tm % HALO == 0 and tm % NORM_ROWS == 0
    hb = tm // HALO
    nf = dff // tf
    fg = (g if final_g is None else final_g).reshape(1, k)
    return pl.pallas_call(
        functools.partial(_ffn_kernel, seq // tm, final_g is not None),
        grid=(m // tm, dff // tf),
        in_specs=[pl.BlockSpec((tm, k), lambda i, j: (i, 0), pipeline_mode=pl.Buffered(1)),
                  pl.BlockSpec((HALO, k), lambda i, j: (jnp.maximum(i * hb - 1, 0), 0)),
                  pl.BlockSpec((1, k), lambda i, j: (0, 0)),
                  pl.BlockSpec((k, tf), lambda i, j: (0, j)),
                  pl.BlockSpec((k, tf), lambda i, j: (0, nf + j)),
                  pl.BlockSpec((FFN_CONV, tf), lambda i, j: (0, j)),
                  pl.BlockSpec((1, tf), lambda i, j: (0, j)),
                  pl.BlockSpec((tf, k), lambda i, j: (j, 0)),
                  pl.BlockSpec((1, k), lambda i, j: (0, 0))],
        out_specs=pl.BlockSpec((tm, k), lambda i, j: (i, 0), pipeline_mode=pl.Buffered(1)),
        out_shape=jax.ShapeDtypeStruct((m, k), F32),
        scratch_shapes=[pltpu.VMEM((HALO + tm, k), BF16)],
        compiler_params=_cp(("parallel", "arbitrary")),
        name="conv_ffn",
    )(h, h, g.reshape(1, k), w_up, w_up, conv_w, conv_b.reshape(1, dff), w_down, fg)


def _t5_bucket(dist):
    max_exact = NUM_BUCKETS // 2
    n = jnp.maximum(dist, 0)
    log_ratio = jnp.log(jnp.maximum(n, 1).astype(F32) / max_exact) / math.log(MAX_DISTANCE / max_exact)
    large = jnp.minimum(max_exact + (log_ratio * (NUM_BUCKETS - max_exact)).astype(jnp.int32), NUM_BUCKETS - 1)
    return jnp.where(n < max_exact, n, large)


def _bias_table(rel_bias, dist, mask):
    b = rel_bias.astype(F32)[_t5_bucket(dist)].transpose(2, 0, 1)
    return jnp.where(mask[None], b, NEG_INF)


def _swa_kernel(sink_ref, q_ref, kp_ref, kc_ref, vp_ref, vc_ref, bias_ref, o_ref):
    g = pl.program_id(1)
    i = pl.program_id(2)
    r = A_HEADS // A_KV_HEADS
    k = jnp.concatenate([kp_ref[0], kc_ref[0]], axis=0)
    v = jnp.concatenate([vp_ref[0], vc_ref[0]], axis=0)
    col = lax.broadcasted_iota(jnp.int32, (Q_BLOCK, 2 * Q_BLOCK), 1)
    first = jnp.logical_and(i == 0, col < Q_BLOCK)
    for hh in range(r):
        qh = q_ref[0, :, hh * HEAD_DIM:(hh + 1) * HEAD_DIM]
        logits = _dot_nt(qh, k) * SCALE + bias_ref[0, hh]
        logits = jnp.where(first, NEG_INF, logits)
        sink = sink_ref[g * r + hh]
        m = jnp.maximum(jnp.max(logits, axis=-1, keepdims=True), sink)
        e = jnp.exp(logits - m)
        p = e / (jnp.sum(e, axis=-1, keepdims=True) + jnp.exp(sink - m))
        o_ref[0, :, hh * HEAD_DIM:(hh + 1) * HEAD_DIM] = _dot(p.astype(BF16), v).astype(o_ref.dtype)


def swa_attention(big, sinks, rel_bias, bsz, t):
    nb = t // Q_BLOCK
    r = A_HEADS // A_KV_HEADS
    ql = jnp.arange(Q_BLOCK)[:, None]
    kl = jnp.arange(2 * Q_BLOCK)[None, :]
    dist = ql + A_WINDOW - kl
    table = _bias_table(rel_bias, dist, (dist >= 0) & (dist < A_WINDOW))
    table = table.reshape(A_KV_HEADS, r, Q_BLOCK, 2 * Q_BLOCK)
    kcol = A_HEADS * HEAD_DIM // LANES
    vcol = kcol + A_KV_HEADS
    prev = lambda i: jnp.maximum(i - 1, 0)
    return pl.pallas_call(
        _swa_kernel,
        grid=(bsz, A_KV_HEADS, nb),
        in_specs=[pl.BlockSpec(memory_space=pltpu.SMEM),
                  pl.BlockSpec((1, Q_BLOCK, r * HEAD_DIM), lambda b, g, i: (b, i, g)),
                  pl.BlockSpec((1, Q_BLOCK, HEAD_DIM), lambda b, g, i: (b, prev(i), kcol + g)),
                  pl.BlockSpec((1, Q_BLOCK, HEAD_DIM), lambda b, g, i: (b, i, kcol + g)),
                  pl.BlockSpec((1, Q_BLOCK, HEAD_DIM), lambda b, g, i: (b, prev(i), vcol + g)),
                  pl.BlockSpec((1, Q_BLOCK, HEAD_DIM), lambda b, g, i: (b, i, vcol + g)),
                  pl.BlockSpec((1, r, Q_BLOCK, 2 * Q_BLOCK), lambda b, g, i: (g, 0, 0, 0))],
        out_specs=pl.BlockSpec((1, Q_BLOCK, r * HEAD_DIM), lambda b, g, i: (b, i, g)),
        out_shape=jax.ShapeDtypeStruct((bsz, t, A_HEADS * HEAD_DIM), BF16),
        compiler_params=_cp(("parallel", "parallel", "arbitrary")),
        name="swa_attention",
    )(sinks.astype(F32), big, big, big, big, big, table)


C_TERMS = 3


def _forget_cumsum_kernel(f_ref, b_ref, c_ref):
    x = f_ref[0] + b_ref[...]
    y = jnp.minimum(x, 0.0) - jnp.log1p(jnp.exp(-jnp.abs(x)))
    t = y.shape[1]
    lane = lax.broadcasted_iota(jnp.int32, y.shape, 1)
    s = 1
    while s < t:
        y = y + jnp.where(lane >= s, pltpu.roll(y, s, 1), 0.0)
        s *= 2
    for n, part in enumerate(_split_bf16(y * (-1.0 / SCALE), C_TERMS)):
        c_ref[0, n] = part


def forget_cumsum(f_t, b_forget):
    bsz, h, t = f_t.shape
    return pl.pallas_call(
        _forget_cumsum_kernel,
        grid=(bsz,),
        in_specs=[pl.BlockSpec((1, h, t), lambda b: (b, 0, 0)), pl.BlockSpec((h, 1), lambda b: (0, 0))],
        out_specs=pl.BlockSpec((1, C_TERMS, h, t), lambda b: (b, 0, 0, 0)),
        out_shape=jax.ShapeDtypeStruct((bsz, C_TERMS, h, t), BF16),
        compiler_params=_cp(("parallel",)),
        name="forget_cumsum",
    )(f_t, b_forget.astype(F32).reshape(h, 1))


def _fox_kernel(tq, q_ref, k_ref, ck_ref, vt_ref, o_ref, m_ref, l_ref, acc_ref, s0_ref, s1_ref):
    i = pl.program_id(2)
    c2 = SCALE * LOG2E
    half = tq // 2
    lane = lax.broadcasted_iota(jnp.int32, (tq, LANES), 1)
    qa = jnp.concatenate([q_ref[0], jnp.where(lane <= C_TERMS, 1.0, 0.0).astype(BF16)], axis=1)
    krow = lax.broadcasted_iota(jnp.int32, (tq, half), 0)
    qcol = lax.broadcasted_iota(jnp.int32, (tq, half), 1)
    last = jnp.maximum(i - 1, 0)

    def key_off(j, diag=False):
        return pl.multiple_of((j if diag else jnp.minimum(j, last)) * tq, tq)

    def scores(j, diag=False):
        valid = True if diag else j < i
        ck_off = pl.multiple_of(jnp.where(valid, (j + 1) * tq, 0), tq)
        k = jnp.concatenate([k_ref[0, pl.ds(key_off(j, diag), tq), :], ck_ref[0, 0, pl.ds(ck_off, tq), :]], axis=1)
        return _dot_nt(k, qa)

    def chunk(s_src, j, diag):
        vt = vt_ref[0, 0, :, pl.ds(key_off(j, diag), tq)]
        for hf in range(2):
            cs = slice(hf * half, (hf + 1) * half)
            s = s_src[:, cs]
            if diag:
                s = jnp.where(krow <= qcol + hf * half, s, NEG_INF)
            zmax = jnp.max(s, axis=0, keepdims=True) * c2
            if diag:
                m_new = zmax
            else:
                m_old = m_ref[:, cs]
                m_new = jnp.maximum(m_old, zmax)
                alpha = jnp.exp2(m_old - m_new)
            p = jnp.exp2(s * c2 - m_new)
            psum = jnp.sum(p, axis=0, keepdims=True)
            pv = _dot(vt, p.astype(BF16))
            m_ref[:, cs] = m_new
            if diag:
                l_ref[:, cs] = psum
                acc_ref[:, cs] = pv
            else:
                l_ref[:, cs] = alpha * l_ref[:, cs] + psum
                acc_ref[:, cs] = acc_ref[:, cs] * alpha + pv

    s_diag = scores(i, diag=True)
    s0_ref[...] = scores(0)
    chunk(s_diag, i, True)

    def far_pair(t, carry):
        a = 2 * t
        s1_ref[...] = scores(a + 1)
        chunk(s0_ref, a, False)
        s0_ref[...] = scores(a + 2)
        chunk(s1_ref, a + 1, False)
        return carry

    lax.fori_loop(0, (i + 1) // 2, far_pair, 0)
    out = acc_ref[...] * (1.0 / l_ref[...])
    for n in range(tq // HEAD_DIM):
        rs = slice(n * HEAD_DIM, (n + 1) * HEAD_DIM)
        o_ref[0, rs, :] = out[:, rs].T.astype(o_ref.dtype)


def fox_attention(big, c_parts, bsz, t, tq=512):
    qcol = (A_HEADS + 2 * A_KV_HEADS) * HEAD_DIM // LANES
    kcol = qcol + B_HEADS
    vcol = kcol + B_HEADS
    mask_rows = np.zeros((tq, LANES), np.float32)
    mask_rows[:, C_TERMS] = -MASK_BIG
    ck = jnp.pad(c_parts.transpose(0, 2, 3, 1), ((0, 0), (0, 0), (0, 0), (0, LANES - C_TERMS)))
    ck = jnp.concatenate([jnp.broadcast_to(jnp.asarray(mask_rows, BF16), (bsz, B_HEADS, tq, LANES)), ck], axis=2)
    v0 = vcol * LANES
    vt = big[:, :, v0:v0 + B_HEADS * HEAD_DIM].reshape(bsz, t, B_HEADS, HEAD_DIM).transpose(0, 2, 3, 1)
    return pl.pallas_call(
        functools.partial(_fox_kernel, tq),
        grid=(bsz, B_HEADS, t // tq),
        in_specs=[pl.BlockSpec((1, tq, HEAD_DIM), lambda b, h, i: (b, i, qcol + h)),
                  pl.BlockSpec((1, t, HEAD_DIM), lambda b, h, i: (b, 0, kcol + h)),
                  pl.BlockSpec((1, 1, tq + t, LANES), lambda b, h, i: (b, h, 0, 0)),
                  pl.BlockSpec((1, 1, HEAD_DIM, t), lambda b, h, i: (b, h, 0, 0))],
        out_specs=pl.BlockSpec((1, tq, HEAD_DIM), lambda b, h, i: (b, i, h)),
        scratch_shapes=[pltpu.VMEM((1, tq), F32), pltpu.VMEM((1, tq), F32), pltpu.VMEM((HEAD_DIM, tq), F32),
                        pltpu.VMEM((tq, tq), F32), pltpu.VMEM((tq, tq), F32)],
        out_shape=jax.ShapeDtypeStruct((bsz, t, B_HEADS * HEAD_DIM), BF16),
        compiler_params=_cp(("parallel", "parallel", "arbitrary")),
        name="fox_attention",
    )(big, big, ck, vt)


def _compress_kernel(x_ref, pe_ref, w1_ref, w2_ref, o_ref):
    n = x_ref.shape[1] // CMP_STRIDE
    a = jnp.zeros((n, CMP_HIDDEN), F32)
    b = jnp.zeros((n, CMP_HIDDEN), F32)
    for m in range(CMP_STRIDE):
        x = x_ref[0, pl.ds(m, n, stride=CMP_STRIDE), :]
        rows = slice(m * HEAD_DIM, (m + 1) * HEAD_DIM)
        rows_b = slice((CMP_STRIDE + m) * HEAD_DIM, (CMP_STRIDE + m + 1) * HEAD_DIM)
        a = a + _dot((x + pe_ref[0, m:m + 1, :]).astype(BF16), w1_ref[0, rows, :])
        b = b + _dot((x + pe_ref[0, CMP_STRIDE + m:CMP_STRIDE + m + 1, :]).astype(BF16), w1_ref[0, rows_b, :])
    hid = a + pltpu.roll(b, n - 1, 0)
    hid = jax.nn.gelu(hid, approximate=True)
    out = _dot(hid.astype(BF16), w2_ref[0])
    row = lax.broadcasted_iota(jnp.int32, out.shape, 0)
    o_ref[0, 0, 0] = jnp.where(row < n - 1, out, 0.0).astype(o_ref.dtype)


def compress_tokens(wide, col0, pe, w1, w2, bsz, t):
    g = C_KV_GROUPS
    n = t // CMP_STRIDE
    return pl.pallas_call(
        _compress_kernel,
        grid=(2, bsz, g),
        in_specs=[pl.BlockSpec((1, t, HEAD_DIM), lambda s, b, gg: (b, 0, col0 + s * g + gg)),
                  pl.BlockSpec((1, CMP_BLOCK, HEAD_DIM), lambda s, b, gg: (s, 0, 0)),
                  pl.BlockSpec((1, CMP_BLOCK * HEAD_DIM, CMP_HIDDEN), lambda s, b, gg: (s, 0, 0)),
                  pl.BlockSpec((1, CMP_HIDDEN, HEAD_DIM), lambda s, b, gg: (s, 0, 0))],
        out_specs=pl.BlockSpec((1, 1, 1, n, HEAD_DIM), lambda s, b, gg: (s, b, gg, 0, 0)),
        out_shape=jax.ShapeDtypeStruct((2, bsz, g, n, HEAD_DIM), BF16),
        compiler_params=_cp(("parallel", "parallel", "parallel")),
        name="nsa_compress",
    )(wide, pe, w1, w2)


NSA_PAD = 4 * Q_BLOCK


def _nsa_kernel(n_sel, q_ref, kc_ref, vct_ref, ks_ref, vst_ref, kw_ref, vwt_ref, gate_ref, near_ref, mimp_ref,
                o_ref, ms_ref, ls_ref, accs_ref, mw_ref, lw_ref, accw_ref, ocmp_ref, s0_ref, s1_ref):
    sel_state = (ms_ref, ls_ref, accs_ref)
    win_state = (mw_ref, lw_ref, accw_ref)
    g = pl.program_id(1)
    i = pl.program_id(2)
    r = C_HEADS // C_KV_GROUPS
    rows = r * Q_BLOCK
    n_sb = mimp_ref.shape[0]
    n_cmp = kc_ref.shape[3]
    c2 = SCALE * LOG2E

    q = jnp.concatenate([q_ref[0, :, hh * HEAD_DIM:(hh + 1) * HEAD_DIM] for hh in range(r)], axis=0)
    qlane = lax.broadcasted_iota(jnp.int32, (1, rows), 1) & (Q_BLOCK - 1)
    qpos = i * Q_BLOCK + qlane
    half = rows // 2
    lane = lax.broadcasted_iota(jnp.int32, (Q_BLOCK, LANES), 1)

    def chunk(z_all, vt, tab, first, state):
        m_ref, l_ref, acc_ref = state
        for hf in range(2):
            cs = slice(hf * half, (hf + 1) * half)
            s = z_all[:, cs]
            if tab is not None:
                z = s * c2 + tab(cs)
                zmax = jnp.max(z, axis=0, keepdims=True)
            else:
                zmax = jnp.max(s, axis=0, keepdims=True) * c2
            if first:
                m_new = zmax
            else:
                m_old = m_ref[:, cs]
                m_new = jnp.maximum(m_old, zmax)
                alpha = jnp.exp2(m_old - m_new)
            p = jnp.exp2(z - m_new) if tab is not None else jnp.exp2(s * c2 - m_new)
            psum = jnp.sum(p, axis=0, keepdims=True)
            pv = _dot(vt, p.astype(BF16))
            m_ref[:, cs] = m_new
            if first:
                l_ref[:, cs] = psum
                acc_ref[:, cs] = pv
            else:
                l_ref[:, cs] = alpha * l_ref[:, cs] + psum
                acc_ref[:, cs] = acc_ref[:, cs] * alpha + pv

    def keys(k_ref, off, n):
        return k_ref[0, 0, pl.ds(pl.multiple_of(off, Q_BLOCK), n), :]

    def vals(vt_ref, off, n):
        return vt_ref[0, 0, :, pl.ds(pl.multiple_of(off, Q_BLOCK), n)]

    near = lambda cs: near_ref[0, :, cs]
    kl = lax.broadcasted_iota(jnp.int32, (Q_BLOCK, half), 0)
    qh = lax.broadcasted_iota(jnp.int32, (Q_BLOCK, half), 1) & (Q_BLOCK - 1)
    tri_tab = jnp.where(kl > qh, 0.0, NEG_INF)
    tri = lambda cs: tri_tab
    base = (i - 1) * Q_BLOCK + NSA_PAD

    pad_flag = jnp.where(lane == LANES - 1, 1.0, 0.0).astype(BF16)
    qw = jnp.concatenate([q, jnp.concatenate([pad_flag] * r, axis=0)], axis=1)
    z_w0 = _dot_nt(keys(kw_ref, base, 2 * Q_BLOCK), qw)
    z_w1 = _dot_nt(keys(kw_ref, base - 2 * Q_BLOCK, 2 * Q_BLOCK), qw)
    z_w2 = _dot_nt(keys(kw_ref, base - 3 * Q_BLOCK, Q_BLOCK), qw)
    chunk(z_w0, vals(vwt_ref, base, 2 * Q_BLOCK), near, True, win_state)
    chunk(z_w1, vals(vwt_ref, base - 2 * Q_BLOCK, 2 * Q_BLOCK), None, False, win_state)
    chunk(z_w2, vals(vwt_ref, base - 3 * Q_BLOCK, Q_BLOCK), tri, False, win_state)

    cend = lax.broadcasted_iota(jnp.int32, (n_cmp, 1), 0) * CMP_STRIDE + (CMP_BLOCK - 1)
    vis = cend <= qpos
    lc = jnp.where(vis, _dot_nt(kc_ref[0, 0, 0], q) * SCALE, NEG_INF)
    mc = jnp.max(lc, axis=0, keepdims=True)
    ec = jnp.where(vis, jnp.exp(lc - mc), 0.0)
    den = jnp.sum(ec, axis=0, keepdims=True)
    pc = ec / jnp.where(den > 0.0, den, 1.0)
    ocmp_ref[...] = _dot(vct_ref[0, 0, 0], pc.astype(BF16))

    imp = pc[:, 0:Q_BLOCK]
    for hh in range(1, r):
        imp = imp + pc[:, hh * Q_BLOCK:(hh + 1) * Q_BLOCK]
    blk = sum(_dot(mimp_ref[...], part) for part in _split_bf16(imp, 3))
    ids = lax.broadcasted_iota(jnp.int32, (n_sb, Q_BLOCK), 0)
    cur = (i * Q_BLOCK + lax.broadcasted_iota(jnp.int32, (n_sb, Q_BLOCK), 1)) // SEL_BLOCK
    forced = (ids == 0) | (ids == cur) | (ids == cur - 1)
    score = jnp.where(forced, FORCE_SCORE, jnp.where(ids > cur, -FORCE_SCORE, blk))
    sub = 8
    tiles = [score[v * sub:(v + 1) * sub] for v in range(n_sb // sub)]
    cnt = [jnp.zeros((sub, Q_BLOCK), F32) for _ in tiles]
    sub_id = lax.broadcasted_iota(jnp.int32, (sub, Q_BLOCK), 0)
    for jp in range(n_sb):
        row = score[jp:jp + 1]
        for v, tile in enumerate(tiles):
            ge = lambda: jnp.where(row >= tile, 1.0, 0.0)
            gt = lambda: jnp.where(row > tile, 1.0, 0.0)
            if v * sub > jp:
                inc = ge()
            elif v * sub + sub - 1 <= jp:
                inc = gt()
            else:
                inc = jnp.where(sub_id + v * sub > jp, ge(), gt())
            cnt[v] = cnt[v] + inc
    notsel = jnp.where(jnp.concatenate(cnt, axis=0) < n_sel, 0.0, 1.0).astype(BF16)
    place = (lax.broadcasted_iota(jnp.int32, (n_sb, LANES), 0)
             == lax.broadcasted_iota(jnp.int32, (n_sb, LANES), 1)).astype(BF16)
    ext = lax.dot_general(notsel, place, (((0,), (0,)), ((), ())), preferred_element_type=F32)
    ext = jnp.where(lane == LANES - 1, 1.0, ext).astype(BF16)
    qa = jnp.concatenate([q, jnp.concatenate([ext] * r, axis=0)], axis=1)

    def scores(k):
        return _dot_nt(k, qa)

    far_keys = 4 * Q_BLOCK
    n_far = (i + 2) // 4
    far_off = lambda c: jnp.maximum(base - (c + 1) * far_keys, 0)
    far_scores = lambda c: scores(keys(ks_ref, far_off(c), far_keys))
    z_near = scores(keys(ks_ref, base, 2 * Q_BLOCK))
    s0_ref[...] = far_scores(0)
    chunk(z_near, vals(vst_ref, base, 2 * Q_BLOCK), near, True, sel_state)

    def sel_far_pair(t, carry):
        a = 2 * t
        s1_ref[...] = far_scores(a + 1)
        chunk(s0_ref, vals(vst_ref, far_off(a), far_keys), None, False, sel_state)
        s0_ref[...] = far_scores(a + 2)
        chunk(s1_ref, vals(vst_ref, far_off(a + 1), far_keys), None, False, sel_state)
        return carry

    lax.fori_loop(0, (n_far + 1) // 2, sel_far_pair, 0)

    gates = _sigmoid(gate_ref[0])
    inv_sel = 1.0 / sel_state[1][...]
    inv_win = 1.0 / win_state[1][...]
    for hh in range(r):
        cs = slice(hh * Q_BLOCK, (hh + 1) * Q_BLOCK)
        gate = lambda branch: gate_ref_row(gates, branch * C_HEADS + g * r + hh)
        out = (gate(0) * ocmp_ref[:, cs] + gate(1) * (sel_state[2][:, cs] * inv_sel[:, cs])
               + gate(2) * (win_state[2][:, cs] * inv_win[:, cs]))
        o_ref[0, :, hh * HEAD_DIM:(hh + 1) * HEAD_DIM] = out.T.astype(o_ref.dtype)


def gate_ref_row(gates, idx):
    sel = lax.broadcasted_iota(jnp.int32, gates.shape, 0) == idx
    return jnp.sum(jnp.where(sel, gates, 0.0), axis=0, keepdims=True)


def nsa_attention(qkv, cmp_kv, small, rel_bias, bsz, t):
    assert C_WINDOW == 4 * Q_BLOCK and NSA_PAD >= C_WINDOW
    nb = t // Q_BLOCK
    n_sb = t // SEL_BLOCK
    assert n_sb < LANES and n_sb % 8 == 0
    n_sel = min(N_SELECT, n_sb)
    n_chunk = t // CMP_STRIDE
    g_, r = C_KV_GROUPS, C_HEADS // C_KV_GROUPS
    hd = HEAD_DIM
    rows = r * Q_BLOCK
    ql = jnp.arange(Q_BLOCK)[:, None]
    kl = jnp.arange(2 * Q_BLOCK)[None, :]
    dist = ql + Q_BLOCK - kl
    far = rel_bias.astype(F32)[_t5_bucket(jnp.int32(2 * Q_BLOCK))]
    near = (_bias_table(rel_bias, dist, dist >= 0) - far[:, None, None]) * LOG2E
    near = near.reshape(g_, r, Q_BLOCK, 2 * Q_BLOCK).transpose(0, 3, 1, 2).reshape(g_, 2 * Q_BLOCK, rows)
    tok = np.arange(n_chunk)[None, :]
    blk = np.arange(n_sb)[:, None]
    per = SEL_BLOCK // CMP_STRIDE
    mimp = ((tok // per == blk).astype(np.float32) + ((tok + 1) // per == blk).astype(np.float32))
    mimp[:, n_chunk - 1] = 0.0
    blk_cols = np.where(np.arange(t)[:, None] // SEL_BLOCK == np.arange(LANES)[None, :], -MASK_BIG, 0.0)
    pad_cols = np.zeros((NSA_PAD, 2 * LANES), np.float32)
    pad_cols[:, -1] = -MASK_BIG

    def heads(col):
        return qkv[:, :, col:col + g_ * hd].reshape(bsz, t, g_, hd).transpose(0, 2, 1, 3)

    def with_mask(k, cols):
        cols = jnp.broadcast_to(jnp.asarray(cols, BF16), (bsz, g_, t, LANES))
        pad = jnp.broadcast_to(jnp.asarray(pad_cols, BF16), (bsz, g_, NSA_PAD, 2 * LANES))
        return jnp.concatenate([pad, jnp.concatenate([k, cols], axis=-1)], axis=2)

    def padded_t(v):
        return jnp.pad(v.transpose(0, 1, 3, 2), ((0, 0), (0, 0), (0, 0), (NSA_PAD, 0)))

    c0 = C_HEADS * hd
    ks = with_mask(heads(c0), blk_cols)
    vst = padded_t(heads(c0 + g_ * hd))
    kw = with_mask(heads(c0 + 2 * g_ * hd), np.zeros((t, LANES), np.float32))
    vwt = padded_t(heads(c0 + 3 * g_ * hd))
    vct = cmp_kv[1].transpose(0, 1, 3, 2)[None]
    n_gate = 3 * C_HEADS
    gates_t = small[:, :, :n_gate].transpose(0, 2, 1)
    tp = t + NSA_PAD
    kspec = pl.BlockSpec((1, 1, tp, 2 * LANES), lambda b, g, i: (b, g, 0, 0))
    vspec = pl.BlockSpec((1, 1, hd, tp), lambda b, g, i: (b, g, 0, 0))
    return pl.pallas_call(
        functools.partial(_nsa_kernel, n_sel),
        grid=(bsz, g_, nb),
        in_specs=[pl.BlockSpec((1, Q_BLOCK, r * hd), lambda b, g, i: (b, i, g)),
                  pl.BlockSpec((1, 1, 1, n_chunk, hd), lambda b, g, i: (0, b, g, 0, 0)),
                  pl.BlockSpec((1, 1, 1, hd, n_chunk), lambda b, g, i: (0, b, g, 0, 0)),
                  kspec, vspec, kspec, vspec,
                  pl.BlockSpec((1, n_gate, Q_BLOCK), lambda b, g, i: (b, 0, i)),
                  pl.BlockSpec((1, 2 * Q_BLOCK, rows), lambda b, g, i: (g, 0, 0)),
                  pl.BlockSpec((n_sb, n_chunk), lambda b, g, i: (0, 0))],
        out_specs=pl.BlockSpec((1, Q_BLOCK, r * hd), lambda b, g, i: (b, i, g)),
        out_shape=jax.ShapeDtypeStruct((bsz, t, C_HEADS * hd), BF16),
        scratch_shapes=[pltpu.VMEM((1, rows), F32), pltpu.VMEM((1, rows), F32), pltpu.VMEM((hd, rows), F32),
                        pltpu.VMEM((1, rows), F32), pltpu.VMEM((1, rows), F32), pltpu.VMEM((hd, rows), F32),
                        pltpu.VMEM((hd, rows), F32),
                        pltpu.VMEM((4 * Q_BLOCK, rows), F32), pltpu.VMEM((4 * Q_BLOCK, rows), F32)],
        compiler_params=_cp(("parallel", "parallel", "arbitrary")),
        name="nsa_attention",
    )(qkv, cmp_kv, vct, ks, vst, kw, vwt, gates_t, near, jnp.asarray(mimp, BF16))


def _bmm(a, b):
    return jnp.einsum('cij,cjk->cik', a, b, preferred_element_type=F32)


def _bmm_nt(a, b):
    return jnp.einsum('cik,cjk->cij', a, b, preferred_element_type=F32)


def _bmm_f32(a, b):
    return _bmm(a.astype(BF16), b.astype(BF16))


def _gdn_prep_kernel(al_ref, dt_ref, q_ref, qp_ref, k_ref, kp_ref, v_ref, vp_ref, wq_ref, wk_ref, wv_ref,
                     a_ref, beta_ref, arow_ref, u_ref, w_ref, qg_ref, kd_ref, attn_ref, eg_ref):
    h = pl.program_id(1)
    rb = pl.program_id(2)
    rows = q_ref.shape[1]
    c = GDN_CHUNK
    nc = rows // c
    a_log = al_ref[h]
    dt_b = dt_ref[h]

    def conv_silu(x_ref, xp_ref, w_ref):
        prev = jnp.where(rb == 0, 0.0, xp_ref[0])
        x = jnp.concatenate([prev, x_ref[0]], axis=0)
        w = w_ref[...]
        n0 = prev.shape[0] - (D_CONV - 1)
        y = sum(w[j:j + 1] * x[n0 + j:n0 + j + rows] for j in range(D_CONV))
        return _silu(y)

    def l2(x):
        return x * lax.rsqrt(jnp.sum(x * x, axis=-1, keepdims=True) + EPS)

    q = (l2(conv_silu(q_ref, qp_ref, wq_ref)) * SCALE).reshape(nc, c, HEAD_DIM)
    k = l2(conv_silu(k_ref, kp_ref, wk_ref)).reshape(nc, c, HEAD_DIM)
    v = conv_silu(v_ref, vp_ref, wv_ref).reshape(nc, c, HEAD_DIM)

    beta = _sigmoid(beta_ref[0, 0]).reshape(nc, c, HEAD_DIM)
    g_col = (-jnp.exp(a_log) * _softplus(a_ref[0, 0] + dt_b)).reshape(nc, c, HEAD_DIM)
    g_row = (-jnp.exp(a_log) * _softplus(arow_ref[0, 0] + dt_b)).reshape(nc, c, c)

    ii = lax.broadcasted_iota(jnp.int32, (nc, c, c), 1)
    jj = lax.broadcasted_iota(jnp.int32, (nc, c, c), 2)
    tril = (jj <= ii).astype(BF16)
    triu = (ii <= jj).astype(BF16)
    gam = sum(_bmm(tril, p) for p in _split_bf16(g_col, 3))
    gam_row = sum(_bmm(p, triu) for p in _split_bf16(g_row, 3))
    causal = jj <= ii
    decay = jnp.where(causal, jnp.exp(jnp.where(causal, gam[:, :, :c] - gam_row, 0.0)), 0.0)

    kb = k * beta
    kbf = k.astype(BF16)
    lmat = jnp.where(jj < ii, _bmm_nt(kb.astype(BF16), kbf) * decay, 0.0)
    eye = (ii == jj).astype(F32)
    inv = eye - lmat
    pw = lmat
    for _ in range(int(math.log2(c)) - 1):
        pw = _bmm_f32(pw, pw)
        inv = inv + _bmm_f32(inv, pw)
    u = _bmm_f32(inv, v * beta)
    w = _bmm_f32(inv, kb * jnp.exp(gam))
    attn = _bmm_nt(q.astype(BF16), kbf) * decay
    g_last = jnp.broadcast_to(gam[:, c - 1:c, :], gam.shape)

    u_ref[0, 0] = u.reshape(rows, HEAD_DIM)
    w_ref[0, 0] = w.reshape(rows, HEAD_DIM).astype(BF16)
    qg_ref[0, 0] = (q * jnp.exp(gam)).reshape(rows, HEAD_DIM).astype(BF16)
    kd_ref[0, 0] = (k * jnp.exp(g_last - gam)).reshape(rows, HEAD_DIM).astype(BF16)
    attn_ref[0, 0] = attn.reshape(rows, c).astype(BF16)
    eg_ref[0, 0] = jnp.exp(g_last[:, 0:8, :])


def _gdn_scan_kernel(u_ref, w_ref, qg_ref, kd_ref, attn_ref, eg_ref, z_ref, ng_ref, o_ref, state_ref):
    c = GDN_CHUNK
    h = u_ref.shape[1]
    n = u_ref.shape[2] // c
    ng = ng_ref[...]

    @pl.when(pl.program_id(1) == 0)
    def _():
        state_ref[...] = jnp.zeros_like(state_ref)

    def step(ci, carry):
        off = pl.multiple_of(ci * c, c)
        sl = pl.ds(off, c)
        heads = range(h)
        states = [state_ref[hh] for hh in heads]
        sbs = [s.astype(BF16) for s in states]
        ws = [_dot(w_ref[0, hh, sl, :], sbs[hh]) for hh in heads]
        qs = [_dot(qg_ref[0, hh, sl, :], sbs[hh]) for hh in heads]
        vbs = [(u_ref[0, hh, sl, :] - ws[hh]).astype(BF16) for hh in heads]
        os = [qs[hh] + _dot(attn_ref[0, hh, sl, :], vbs[hh]) for hh in heads]
        upd = [lax.dot_general(kd_ref[0, hh, sl, :], vbs[hh], (((0,), (0,)), ((), ())), preferred_element_type=F32)
               for hh in heads]
        for hh in heads:
            cols = slice(hh * HEAD_DIM, (hh + 1) * HEAD_DIM)
            eg = eg_ref[0, hh, ci]
            decayed = (states[hh].reshape(HEAD_DIM // 8, 8, HEAD_DIM) * eg[None]).reshape(HEAD_DIM, HEAD_DIM)
            state_ref[hh] = decayed + upd[hh]
            o_ref[0, sl, cols] = (_rms(os[hh], ng) * _silu(z_ref[0, sl, cols])).astype(o_ref.dtype)
        return carry

    lax.fori_loop(0, n, step, 0)


def gated_deltanet(wide, small, conv_w, a_log, dt_bias, norm_g, bsz, t, rows=512):
    c = GDN_CHUNK
    h = D_HEADS
    nblk = t // rows
    qcol = 0
    kcol, vcol, zcol = qcol + h, qcol + 2 * h, qcol + 3 * h
    gate_col = 3 * C_HEADS
    beta_t = small[:, :, gate_col:gate_col + h].transpose(0, 2, 1)
    a_t = small[:, :, gate_col + h:gate_col + 2 * h].transpose(0, 2, 1)
    beta_b = jnp.broadcast_to(beta_t[..., None], (bsz, h, t, HEAD_DIM))
    a_b = jnp.broadcast_to(a_t[..., None], (bsz, h, t, HEAD_DIM))
    a_row = jnp.broadcast_to(a_t.reshape(bsz, h, t // c, 1, c), (bsz, h, t // c, c, c)).reshape(bsz, h, t, c)
    hb = rows // 8
    main = lambda col: pl.BlockSpec((1, rows, HEAD_DIM), lambda b, hh, r: (b, r, col + hh))
    halo = lambda col: pl.BlockSpec((1, 8, HEAD_DIM), lambda b, hh, r: (b, jnp.maximum(r * hb - 1, 0), col + hh))
    cw = lambda off: pl.BlockSpec((D_CONV, HEAD_DIM), lambda b, hh, r: (0, off * h + hh))
    per_tok = lambda width: pl.BlockSpec((1, 1, rows, width), lambda b, hh, r: (b, hh, r, 0))
    smem = pl.BlockSpec(memory_space=pltpu.SMEM)
    shp = lambda width, dt: jax.ShapeDtypeStruct((bsz, h, t, width), dt)
    u, w, qg, kd, attn, eg = pl.pallas_call(
        _gdn_prep_kernel,
        grid=(bsz, h, nblk),
        in_specs=[smem, smem, main(qcol), halo(qcol), main(kcol), halo(kcol), main(vcol), halo(vcol),
                  cw(0), cw(1), cw(2), per_tok(HEAD_DIM), per_tok(HEAD_DIM), per_tok(c)],
        out_specs=[per_tok(HEAD_DIM), per_tok(HEAD_DIM), per_tok(HEAD_DIM), per_tok(HEAD_DIM), per_tok(c),
                   pl.BlockSpec((1, 1, rows // c, 8, HEAD_DIM), lambda b, hh, r: (b, hh, r, 0, 0))],
        out_shape=[shp(HEAD_DIM, F32), shp(HEAD_DIM, BF16), shp(HEAD_DIM, BF16), shp(HEAD_DIM, BF16), shp(c, BF16),
                   jax.ShapeDtypeStruct((bsz, h, t // c, 8, HEAD_DIM), F32)],
        compiler_params=_cp(("parallel", "parallel", "parallel")),
        name="gdn_prep",
    )(a_log.astype(F32), dt_bias.astype(F32), wide, wide, wide, wide, wide, wide,
      conv_w, conv_w, conv_w, a_b, beta_b, a_row)
    blk = lambda width: pl.BlockSpec((1, h, rows, width), lambda b, r: (b, 0, r, 0))
    assert zcol % h == 0
    return pl.pallas_call(
        _gdn_scan_kernel,
        grid=(bsz, nblk),
        in_specs=[blk(HEAD_DIM), blk(HEAD_DIM), blk(HEAD_DIM), blk(HEAD_DIM), blk(c),
                  pl.BlockSpec((1, h, rows // c, 8, HEAD_DIM), lambda b, r: (b, 0, r, 0, 0)),
                  pl.BlockSpec((1, rows, h * HEAD_DIM), lambda b, r: (b, r, zcol // h)),
                  pl.BlockSpec((1, HEAD_DIM), lambda b, r: (0, 0))],
        out_specs=pl.BlockSpec((1, rows, h * HEAD_DIM), lambda b, r: (b, r, 0)),
        out_shape=jax.ShapeDtypeStruct((bsz, t, h * HEAD_DIM), BF16),
        scratch_shapes=[pltpu.VMEM((h, HEAD_DIM, HEAD_DIM), F32)],
        compiler_params=_cp(("parallel", "arbitrary")),
        name="gdn_scan",
    )(u, w, qg, kd, attn, eg, wide, norm_g.astype(F32).reshape(1, HEAD_DIM))


def _pad_cols(w, width):
    return jnp.pad(w, ((0, 0), (0, width - w.shape[1])))


def even_mixer(h, norm_g, w_in, b_forget, sinks, w_out, rel_bias, bsz, t):
    n_big = (A_HEADS + 2 * A_KV_HEADS + 3 * B_HEADS) * HEAD_DIM
    big, small = norm_matmul(h, norm_g, w_in[:, :n_big].astype(BF16), BF16,
                             w_side=_pad_cols(w_in[:, n_big:], LANES).astype(BF16), tn=768)
    big = big.reshape(bsz, t, n_big)
    o_a = swa_attention(big, sinks, rel_bias, bsz, t)
    f_t = small.reshape(bsz, t, LANES)[:, :, :B_HEADS].transpose(0, 2, 1)
    o_b = fox_attention(big, forget_cumsum(f_t, b_forget), bsz, t)
    return out_proj(o_a.reshape(bsz * t, -1), o_b.reshape(bsz * t, -1), w_out.astype(BF16), h)


def odd_mixer(h, norm_g, w_in, cmp_pos, cmp_w1, cmp_w2, conv_w, a_log, dt_bias, gdn_norm, w_out, rel_bias, bsz, t):
    hd = HEAD_DIM
    g = C_KV_GROUPS
    o_q = 0
    o_kcmp = C_HEADS * hd
    o_ksel = o_kcmp + 2 * g * hd
    o_gates = o_ksel + 4 * g * hd
    o_qd = o_gates + 3 * C_HEADS
    o_beta = o_qd + 3 * D_HEADS * hd
    o_z = o_beta + 2 * D_HEADS
    w_bf = jnp.concatenate([w_in[:, o_q:o_kcmp], w_in[:, o_ksel:o_gates]], axis=1)
    w_f32 = jnp.concatenate([w_in[:, o_qd:o_beta], w_in[:, o_z:], w_in[:, o_kcmp:o_ksel]], axis=1)
    w_small = _pad_cols(jnp.concatenate([w_in[:, o_gates:o_qd], w_in[:, o_beta:o_z]], axis=1), LANES)
    qkv = norm_matmul(h, norm_g, w_bf.astype(BF16), BF16).reshape(bsz, t, -1)
    wide, small = norm_matmul(h, norm_g, w_f32.astype(BF16), F32, w_side=w_small.astype(BF16), tn=768)
    wide = wide.reshape(bsz, t, -1)
    small = small.reshape(bsz, t, LANES)

    n_chunk = t // CMP_STRIDE
    cmp_kv = compress_tokens(wide, 4 * D_HEADS, cmp_pos, cmp_w1.astype(BF16), cmp_w2.astype(BF16), bsz, t)
    o_c = nsa_attention(qkv, cmp_kv, small, rel_bias, bsz, t)
    o_d = gated_deltanet(wide, small, conv_w, a_log, dt_bias, gdn_norm, bsz, t)
    return out_proj(o_c.reshape(bsz * t, -1), o_d.reshape(bsz * t, -1), w_out.astype(BF16), h)


def kernel(x, rel_bias, norm_mix, norm_ffn, norm_final, ev_w_in, ev_b_forget, ev_sinks, ev_w_out, od_w_in,
           od_cmp_pos, od_cmp_w1, od_cmp_w2, od_conv_w, od_a_log, od_dt_bias, od_gdn_norm, od_w_out, ffn_w_up,
           ffn_conv_w, ffn_conv_b, ffn_w_down):
    bsz, t, d = x.shape
    h = x.reshape(bsz * t, d)
    depth = norm_mix.shape[0]
    for layer in range(depth):
        j = layer // 2
        if layer % 2 == 0:
            h = even_mixer(h, norm_mix[layer], ev_w_in[j], ev_b_forget[j], ev_sinks[j], ev_w_out[j], rel_bias,
                           bsz, t)
        else:
            h = odd_mixer(h, norm_mix[layer], od_w_in[j], od_cmp_pos[j], od_cmp_w1[j], od_cmp_w2[j], od_conv_w[j],
                          od_a_log[j], od_dt_bias[j], od_gdn_norm[j], od_w_out[j], rel_bias, bsz, t)
        h = conv_ffn(h, norm_ffn[layer], ffn_w_up[layer].astype(BF16), ffn_conv_w[layer], ffn_conv_b[layer],
                     ffn_w_down[layer].astype(BF16), t, final_g=norm_final if layer == depth - 1 else None)
    return h.reshape(bsz, t, d)
```

```python
import functools
import math

import jax
import jax.numpy as jnp
import numpy as np
from jax import lax
from jax.experimental import pallas as pl
from jax.experimental.pallas import tpu as pltpu

D_MODEL = 2048
DEPTH = 4
HEAD_DIM = 128
A_HEADS = 8
A_KV_HEADS = 2
A_WINDOW = 128
B_HEADS = 8
C_HEADS = 8
C_KV_GROUPS = 2
CMP_BLOCK = 32
CMP_STRIDE = 16
CMP_HIDDEN = 256
SEL_BLOCK = 64
N_SELECT = 8
C_WINDOW = 512
D_HEADS = 8
D_CONV = 4
GDN_CHUNK = 64
NUM_BUCKETS = 32
MAX_DISTANCE = 128
D_FF = 11 * D_MODEL // 4
FFN_CONV = 3
Q_BLOCK = 128
EPS = 1e-6
NEG_INF = -1e30
FORCE_SCORE = 1e9
SCALE = HEAD_DIM ** -0.5
LOG2E = math.log2(math.e)
MASK_BIG = 2.0 ** 100

F32 = jnp.float32
BF16 = jnp.bfloat16
LANES = 128
HALO = 16
NORM_ROWS = 256
FFN_PIECE = 256
VMEM_LIMIT = 52 * 1024 * 1024


def _cp(dims, vmem=VMEM_LIMIT):
    return pltpu.CompilerParams(dimension_semantics=dims, vmem_limit_bytes=vmem)


def _dot(a, b):
    return jnp.dot(a, b, preferred_element_type=F32)


def _dot_nt(a, b):
    return lax.dot_general(a, b, (((1,), (1,)), ((), ())), preferred_element_type=F32)


def _rms(x, g):
    return x * lax.rsqrt(jnp.mean(x * x, axis=-1, keepdims=True) + EPS) * g


def _sigmoid(x):
    return 1.0 / (1.0 + jnp.exp(-x))


def _silu(x):
    return x * _sigmoid(x)


def _softplus(x):
    return jnp.maximum(x, 0.0) + jnp.log1p(jnp.exp(-jnp.abs(x)))


def _split_bf16(x, parts):
    out = []
    for _ in range(parts - 1):
        p = x.astype(BF16)
        out.append(p)
        x = x - p.astype(F32)
    out.append(x.astype(BF16))
    return out


def _norm_matmul_kernel(has_side, x_ref, g_ref, w_ref, *rest):
    if has_side:
        ws_ref, o_ref, os_ref, xn_ref = rest
    else:
        o_ref, xn_ref = rest

    @pl.when(pl.program_id(1) == 0)
    def _():
        for r0 in range(0, x_ref.shape[0], NORM_ROWS):
            rs = slice(r0, r0 + NORM_ROWS)
            xn_ref[rs, :] = _rms(x_ref[rs, :], g_ref[...]).astype(BF16)
        if has_side:
            os_ref[...] = _dot(xn_ref[...], ws_ref[...])

    o_ref[...] = _dot(xn_ref[...], w_ref[...]).astype(o_ref.dtype)


def norm_matmul(x, g, w, out_dtype, w_side=None, tm=1024, tn=512):
    m, k = x.shape
    n = w.shape[1]
    assert m % tm == 0 and n % tn == 0 and tm % NORM_ROWS == 0
    in_specs = [pl.BlockSpec((tm, k), lambda i, j: (i, 0), pipeline_mode=pl.Buffered(1)),
                pl.BlockSpec((1, k), lambda i, j: (0, 0)),
                pl.BlockSpec((k, tn), lambda i, j: (0, j))]
    out_specs = [pl.BlockSpec((tm, tn), lambda i, j: (i, j))]
    out_shape = [jax.ShapeDtypeStruct((m, n), out_dtype)]
    args = [x, g.reshape(1, k), w]
    if w_side is not None:
        ns = w_side.shape[1]
        in_specs.append(pl.BlockSpec((k, ns), lambda i, j: (0, 0)))
        out_specs.append(pl.BlockSpec((tm, ns), lambda i, j: (i, 0)))
        out_shape.append(jax.ShapeDtypeStruct((m, ns), F32))
        args.append(w_side)
    out = pl.pallas_call(
        functools.partial(_norm_matmul_kernel, w_side is not None),
        grid=(m // tm, n // tn),
        in_specs=in_specs,
        out_specs=out_specs,
        out_shape=out_shape,
        scratch_shapes=[pltpu.VMEM((tm, k), BF16)],
        compiler_params=_cp(("parallel", "arbitrary")),
        name="norm_matmul",
    )(*args)
    return out if w_side is not None else out[0]


def _out_proj_kernel(a1_ref, a2_ref, w1_ref, w2_ref, h_ref, o_ref):
    o_ref[...] = h_ref[...] + _dot(a1_ref[...], w1_ref[...]) + _dot(a2_ref[...], w2_ref[...])


def out_proj(a1, a2, w, h, tm=512):
    m, k1 = a1.shape
    k2 = a2.shape[1]
    n = w.shape[1]
    assert k1 == k2 and w.shape[0] == k1 + k2
    return pl.pallas_call(
        _out_proj_kernel,
        grid=(m // tm,),
        in_specs=[pl.BlockSpec((tm, k1), lambda i: (i, 0)),
                  pl.BlockSpec((tm, k2), lambda i: (i, 0)),
                  pl.BlockSpec((k1, n), lambda i: (0, 0)),
                  pl.BlockSpec((k2, n), lambda i: (1, 0)),
                  pl.BlockSpec((tm, n), lambda i: (i, 0))],
        out_specs=pl.BlockSpec((tm, n), lambda i: (i, 0)),
        out_shape=jax.ShapeDtypeStruct((m, n), F32),
        compiler_params=_cp(("parallel",)),
        name="out_proj",
    )(a1, a2, w, w, h)


def _ffn_kernel(seq_tiles, final, h_ref, hp_ref, g_ref, wu_ref, wg_ref, cw_ref, cb_ref, wd_ref, fg_ref, o_ref,
                xn_ref):
    i = pl.program_id(0)
    j = pl.program_id(1)
    tm = h_ref.shape[0]

    @pl.when(j == 0)
    def _():
        keep = (i % seq_tiles != 0).astype(F32)
        xn_ref[0:HALO, :] = (_rms(hp_ref[...], g_ref[...]) * keep).astype(BF16)
        for r0 in range(0, tm, NORM_ROWS):
            xn_ref[HALO + r0:HALO + r0 + NORM_ROWS, :] = _rms(h_ref[r0:r0 + NORM_ROWS, :], g_ref[...]).astype(BF16)
        o_ref[...] = h_ref[...]

    tf = wu_ref.shape[1]
    pieces = [slice(c0, c0 + FFN_PIECE) for c0 in range(0, tf, FFN_PIECE)]
    up = [(_dot(xn_ref[HALO:, :], wu_ref[:, cs]),
           _dot(xn_ref[...], wg_ref[:, cs]))
          for cs in pieces]
    down = None
    for cs, (u, ge) in zip(pieces, up):
        cw = cw_ref[:, cs]
        gc = (cw[0:1] * ge[HALO - 2:HALO - 2 + tm] + cw[1:2] * ge[HALO - 1:HALO - 1 + tm]
              + cw[2:3] * ge[HALO:] + cb_ref[:, cs])
        part = _dot((_silu(gc) * u).astype(BF16), wd_ref[cs, :])
        down = part if down is None else down + part
    o_ref[...] += down

    if final:
        @pl.when(j == pl.num_programs(1) - 1)
        def _():
            for r0 in range(0, tm, NORM_ROWS):
                rs = slice(r0, r0 + NORM_ROWS)
                o_ref[rs, :] = _rms(o_ref[rs, :], fg_ref[...])


def conv_ffn(h, g, w_up, conv_w, conv_b, w_down, seq, final_g=None, tm=1024, tf=512):
    m, k = h.shape
    dff = w_down.shape[0]
    assert m % tm == 0 and dff % tf == 0 and seq % tm == 0 and tm % HALO == 0 and tm % NORM_ROWS == 0
    hb = tm // HALO
    nf = dff // tf
    fg = (g if final_g is None else final_g).reshape(1, k)
    return pl.pallas_call(
        functools.partial(_ffn_kernel, seq // tm, final_g is not None),
        grid=(m // tm, dff // tf),
        in_specs=[pl.BlockSpec((tm, k), lambda i, j: (i, 0), pipeline_mode=pl.Buffered(1)),
                  pl.BlockSpec((HALO, k), lambda i, j: (jnp.maximum(i * hb - 1, 0), 0)),
                  pl.BlockSpec((1, k), lambda i, j: (0, 0)),
                  pl.BlockSpec((k, tf), lambda i, j: (0, j)),
                  pl.BlockSpec((k, tf), lambda i, j: (0, nf + j)),
                  pl.BlockSpec((FFN_CONV, tf), lambda i, j: (0, j)),
                  pl.BlockSpec((1, tf), lambda i, j: (0, j)),
                  pl.BlockSpec((tf, k), lambda i, j: (j, 0)),
                  pl.BlockSpec((1, k), lambda i, j: (0, 0))],
        out_specs=pl.BlockSpec((tm, k), lambda i, j: (i, 0), pipeline_mode=pl.Buffered(1)),
        out_shape=jax.ShapeDtypeStruct((m, k), F32),
        scratch_shapes=[pltpu.VMEM((HALO + tm, k), BF16)],
        compiler_params=_cp(("parallel", "arbitrary")),
        name="conv_ffn",
    )(h, h, g.reshape(1, k), w_up, w_up, conv_w, conv_b.reshape(1, dff), w_down, fg)


def _t5_bucket(dist):
    max_exact = NUM_BUCKETS // 2
    n = jnp.maximum(dist, 0)
    log_ratio = jnp.log(jnp.maximum(n, 1).astype(F32) / max_exact) / math.log(MAX_DISTANCE / max_exact)
    large = jnp.minimum(max_exact + (log_ratio * (NUM_BUCKETS - max_exact)).astype(jnp.int32), NUM_BUCKETS - 1)
    return jnp.where(n < max_exact, n, large)


def _bias_table(rel_bias, dist, mask):
    b = rel_bias.astype(F32)[_t5_bucket(dist)].transpose(2, 0, 1)
    return jnp.where(mask[None], b, NEG_INF)


def _swa_kernel(sink_ref, q_ref, kp_ref, kc_ref, vp_ref, vc_ref, bias_ref, o_ref):
    g = pl.program_id(1)
    i = pl.program_id(2)
    r = A_HEADS // A_KV_HEADS
    k = jnp.concatenate([kp_ref[0], kc_ref[0]], axis=0)
    v = jnp.concatenate([vp_ref[0], vc_ref[0]], axis=0)
    col = lax.broadcasted_iota(jnp.int32, (Q_BLOCK, 2 * Q_BLOCK), 1)
    first = jnp.logical_and(i == 0, col < Q_BLOCK)
    scores = [_dot_nt(q_ref[0, :, hh * HEAD_DIM:(hh + 1) * HEAD_DIM], k) for hh in range(r)]
    probs = []
    for hh in range(r):
        logits = jnp.where(first, NEG_INF, scores[hh] * SCALE + bias_ref[0, hh])
        sink = sink_ref[g * r + hh]
        m = jnp.maximum(jnp.max(logits, axis=-1, keepdims=True), sink)
        e = jnp.exp(logits - m)
        p = e / (jnp.sum(e, axis=-1, keepdims=True) + jnp.exp(sink - m))
        probs.append(p.astype(BF16))
    for hh in range(r):
        o_ref[0, :, hh * HEAD_DIM:(hh + 1) * HEAD_DIM] = _dot(probs[hh], v).astype(o_ref.dtype)


def swa_attention(big, sinks, rel_bias, bsz, t):
    nb = t // Q_BLOCK
    r = A_HEADS // A_KV_HEADS
    ql = jnp.arange(Q_BLOCK)[:, None]
    kl = jnp.arange(2 * Q_BLOCK)[None, :]
    dist = ql + A_WINDOW - kl
    table = _bias_table(rel_bias, dist, (dist >= 0) & (dist < A_WINDOW))
    table = table.reshape(A_KV_HEADS, r, Q_BLOCK, 2 * Q_BLOCK)
    kcol = A_HEADS * HEAD_DIM // LANES
    vcol = kcol + A_KV_HEADS
    prev = lambda i: jnp.maximum(i - 1, 0)
    return pl.pallas_call(
        _swa_kernel,
        grid=(bsz, A_KV_HEADS, nb),
        in_specs=[pl.BlockSpec(memory_space=pltpu.SMEM),
                  pl.BlockSpec((1, Q_BLOCK, r * HEAD_DIM), lambda b, g, i: (b, i, g)),
                  pl.BlockSpec((1, Q_BLOCK, HEAD_DIM), lambda b, g, i: (b, prev(i), kcol + g)),
                  pl.BlockSpec((1, Q_BLOCK, HEAD_DIM), lambda b, g, i: (b, i, kcol + g)),
                  pl.BlockSpec((1, Q_BLOCK, HEAD_DIM), lambda b, g, i: (b, prev(i), vcol + g)),
                  pl.BlockSpec((1, Q_BLOCK, HEAD_DIM), lambda b, g, i: (b, i, vcol + g)),
                  pl.BlockSpec((1, r, Q_BLOCK, 2 * Q_BLOCK), lambda b, g, i: (g, 0, 0, 0))],
        out_specs=pl.BlockSpec((1, Q_BLOCK, r * HEAD_DIM), lambda b, g, i: (b, i, g)),
        out_shape=jax.ShapeDtypeStruct((bsz, t, A_HEADS * HEAD_DIM), BF16),
        compiler_params=_cp(("parallel", "parallel", "arbitrary")),
        name="swa_attention",
    )(sinks.astype(F32), big, big, big, big, big, table)


C_TERMS = 3


def _forget_cumsum_kernel(f_ref, b_ref, c_ref):
    x = f_ref[0] + b_ref[...]
    y = jnp.minimum(x, 0.0) - jnp.log1p(jnp.exp(-jnp.abs(x)))
    t = y.shape[1]
    lane = lax.broadcasted_iota(jnp.int32, y.shape, 1)
    s = 1
    while s < t:
        y = y + jnp.where(lane >= s, pltpu.roll(y, s, 1), 0.0)
        s *= 2
    for n, part in enumerate(_split_bf16(y * (-1.0 / SCALE), C_TERMS)):
        c_ref[0, n] = part


def forget_cumsum(f_t, b_forget):
    bsz, h, t = f_t.shape
    return pl.pallas_call(
        _forget_cumsum_kernel,
        grid=(bsz,),
        in_specs=[pl.BlockSpec((1, h, t), lambda b: (b, 0, 0)), pl.BlockSpec((h, 1), lambda b: (0, 0))],
        out_specs=pl.BlockSpec((1, C_TERMS, h, t), lambda b: (b, 0, 0, 0)),
        out_shape=jax.ShapeDtypeStruct((bsz, C_TERMS, h, t), BF16),
        compiler_params=_cp(("parallel",)),
        name="forget_cumsum",
    )(f_t, b_forget.astype(F32).reshape(h, 1))


def _fox_kernel(tq, q_ref, k_ref, ck_ref, vt_ref, o_ref, m_ref, l_ref, acc_ref, s0_ref, s1_ref):
    i = pl.program_id(2)
    c2 = SCALE * LOG2E
    half = tq // 2
    lane = lax.broadcasted_iota(jnp.int32, (tq, LANES), 1)
    qa = jnp.concatenate([q_ref[0], jnp.where(lane <= C_TERMS, 1.0, 0.0).astype(BF16)], axis=1)
    krow = lax.broadcasted_iota(jnp.int32, (tq, half), 0)
    qcol = lax.broadcasted_iota(jnp.int32, (tq, half), 1)
    last = jnp.maximum(i - 1, 0)

    def key_off(j, diag=False):
        return pl.multiple_of((j if diag else jnp.minimum(j, last)) * tq, tq)

    def scores(j, diag=False):
        valid = True if diag else j < i
        ck_off = pl.multiple_of(jnp.where(valid, (j + 1) * tq, 0), tq)
        k = jnp.concatenate([k_ref[0, pl.ds(key_off(j, diag), tq), :], ck_ref[0, 0, pl.ds(ck_off, tq), :]], axis=1)
        return _dot_nt(k, qa)

    def chunk(s_src, j, diag):
        vt = vt_ref[0, 0, :, pl.ds(key_off(j, diag), tq)]
        for hf in range(2):
            cs = slice(hf * half, (hf + 1) * half)
            s = s_src[:, cs]
            if diag:
                s = jnp.where(krow <= qcol + hf * half, s, NEG_INF)
            zmax = jnp.max(s, axis=0, keepdims=True) * c2
            if diag:
                m_new = zmax
            else:
                m_old = m_ref[:, cs]
                m_new = jnp.maximum(m_old, zmax)
                alpha = jnp.exp2(m_old - m_new)
            p = jnp.exp2(s * c2 - m_new)
            psum = jnp.sum(p, axis=0, keepdims=True)
            pv = _dot(vt, p.astype(BF16))
            m_ref[:, cs] = m_new
            if diag:
                l_ref[:, cs] = psum
                acc_ref[:, cs] = pv
            else:
                l_ref[:, cs] = alpha * l_ref[:, cs] + psum
                acc_ref[:, cs] = acc_ref[:, cs] * alpha + pv

    s_diag = scores(i, diag=True)
    s0_ref[...] = scores(0)
    chunk(s_diag, i, True)

    def far_pair(t, carry):
        a = 2 * t
        s1_ref[...] = scores(a + 1)
        chunk(s0_ref, a, False)
        s0_ref[...] = scores(a + 2)
        chunk(s1_ref, a + 1, False)
        return carry

    lax.fori_loop(0, (i + 1) // 2, far_pair, 0)
    out = acc_ref[...] * (1.0 / l_ref[...])
    for n in range(tq // HEAD_DIM):
        rs = slice(n * HEAD_DIM, (n + 1) * HEAD_DIM)
        o_ref[0, rs, :] = out[:, rs].T.astype(o_ref.dtype)


def fox_attention(big, c_parts, bsz, t, tq=512):
    qcol = (A_HEADS + 2 * A_KV_HEADS) * HEAD_DIM // LANES
    kcol = qcol + B_HEADS
    vcol = kcol + B_HEADS
    mask_rows = np.zeros((tq, LANES), np.float32)
    mask_rows[:, C_TERMS] = -MASK_BIG
    ck = jnp.pad(c_parts.transpose(0, 2, 3, 1), ((0, 0), (0, 0), (0, 0), (0, LANES - C_TERMS)))
    ck = jnp.concatenate([jnp.broadcast_to(jnp.asarray(mask_rows, BF16), (bsz, B_HEADS, tq, LANES)), ck], axis=2)
    v0 = vcol * LANES
    vt = big[:, :, v0:v0 + B_HEADS * HEAD_DIM].reshape(bsz, t, B_HEADS, HEAD_DIM).transpose(0, 2, 3, 1)
    return pl.pallas_call(
        functools.partial(_fox_kernel, tq),
        grid=(bsz, B_HEADS, t // tq),
        in_specs=[pl.BlockSpec((1, tq, HEAD_DIM), lambda b, h, i: (b, i, qcol + h)),
                  pl.BlockSpec((1, t, HEAD_DIM), lambda b, h, i: (b, 0, kcol + h)),
                  pl.BlockSpec((1, 1, tq + t, LANES), lambda b, h, i: (b, h, 0, 0)),
                  pl.BlockSpec((1, 1, HEAD_DIM, t), lambda b, h, i: (b, h, 0, 0))],
        out_specs=pl.BlockSpec((1, tq, HEAD_DIM), lambda b, h, i: (b, i, h)),
        scratch_shapes=[pltpu.VMEM((1, tq), F32), pltpu.VMEM((1, tq), F32), pltpu.VMEM((HEAD_DIM, tq), F32),
                        pltpu.VMEM((tq, tq), F32), pltpu.VMEM((tq, tq), F32)],
        out_shape=jax.ShapeDtypeStruct((bsz, t, B_HEADS * HEAD_DIM), BF16),
        compiler_params=_cp(("parallel", "parallel", "arbitrary")),
        name="fox_attention",
    )(big, big, ck, vt)


def _compress_kernel(x_ref, pe_ref, w1_ref, w2_ref, o_ref):
    n = x_ref.shape[1] // CMP_STRIDE
    a = jnp.zeros((n, CMP_HIDDEN), F32)
    b = jnp.zeros((n, CMP_HIDDEN), F32)
    for m in range(CMP_STRIDE):
        x = x_ref[0, pl.ds(m, n, stride=CMP_STRIDE), :]
        rows = slice(m * HEAD_DIM, (m + 1) * HEAD_DIM)
        rows_b = slice((CMP_STRIDE + m) * HEAD_DIM, (CMP_STRIDE + m + 1) * HEAD_DIM)
        a = a + _dot((x + pe_ref[0, m:m + 1, :]).astype(BF16), w1_ref[0, rows, :])
        b = b + _dot((x + pe_ref[0, CMP_STRIDE + m:CMP_STRIDE + m + 1, :]).astype(BF16), w1_ref[0, rows_b, :])
    hid = a + pltpu.roll(b, n - 1, 0)
    hid = jax.nn.gelu(hid, approximate=True)
    out = _dot(hid.astype(BF16), w2_ref[0])
    row = lax.broadcasted_iota(jnp.int32, out.shape, 0)
    o_ref[0, 0, 0] = jnp.where(row < n - 1, out, 0.0).astype(o_ref.dtype)


def compress_tokens(wide, col0, pe, w1, w2, bsz, t):
    g = C_KV_GROUPS
    n = t // CMP_STRIDE
    return pl.pallas_call(
        _compress_kernel,
        grid=(2, bsz, g),
        in_specs=[pl.BlockSpec((1, t, HEAD_DIM), lambda s, b, gg: (b, 0, col0 + s * g + gg)),
                  pl.BlockSpec((1, CMP_BLOCK, HEAD_DIM), lambda s, b, gg: (s, 0, 0)),
                  pl.BlockSpec((1, CMP_BLOCK * HEAD_DIM, CMP_HIDDEN), lambda s, b, gg: (s, 0, 0)),
                  pl.BlockSpec((1, CMP_HIDDEN, HEAD_DIM), lambda s, b, gg: (s, 0, 0))],
        out_specs=pl.BlockSpec((1, 1, 1, n, HEAD_DIM), lambda s, b, gg: (s, b, gg, 0, 0)),
        out_shape=jax.ShapeDtypeStruct((2, bsz, g, n, HEAD_DIM), BF16),
        compiler_params=_cp(("parallel", "parallel", "parallel")),
        name="nsa_compress",
    )(wide, pe, w1, w2)


NSA_PAD = 4 * Q_BLOCK


def _nsa_kernel(n_sel, q_ref, kc_ref, vct_ref, ks_ref, vst_ref, kw_ref, vwt_ref, gate_ref, near_ref, mimp_ref,
                o_ref, ms_ref, ls_ref, accs_ref, mw_ref, lw_ref, accw_ref, ocmp_ref, s0_ref, s1_ref):
    sel_state = (ms_ref, ls_ref, accs_ref)
    win_state = (mw_ref, lw_ref, accw_ref)
    g = pl.program_id(1)
    i = pl.program_id(2)
    r = C_HEADS // C_KV_GROUPS
    rows = r * Q_BLOCK
    n_sb = mimp_ref.shape[0]
    n_cmp = kc_ref.shape[3]
    c2 = SCALE * LOG2E

    q = jnp.concatenate([q_ref[0, :, hh * HEAD_DIM:(hh + 1) * HEAD_DIM] for hh in range(r)], axis=0)
    qlane = lax.broadcasted_iota(jnp.int32, (1, rows), 1) & (Q_BLOCK - 1)
    qpos = i * Q_BLOCK + qlane
    half = rows // 2
    lane = lax.broadcasted_iota(jnp.int32, (Q_BLOCK, LANES), 1)

    def chunk(z_all, vt, tab, first, state):
        m_ref, l_ref, acc_ref = state
        for hf in range(2):
            cs = slice(hf * half, (hf + 1) * half)
            s = z_all[:, cs]
            if tab is not None:
                z = s * c2 + tab(cs)
                zmax = jnp.max(z, axis=0, keepdims=True)
            else:
                zmax = jnp.max(s, axis=0, keepdims=True) * c2
            if first:
                m_new = zmax
            else:
                m_old = m_ref[:, cs]
                m_new = jnp.maximum(m_old, zmax)
                alpha = jnp.exp2(m_old - m_new)
            p = jnp.exp2(z - m_new) if tab is not None else jnp.exp2(s * c2 - m_new)
            psum = jnp.sum(p, axis=0, keepdims=True)
            pv = _dot(vt, p.astype(BF16))
            m_ref[:, cs] = m_new
            if first:
                l_ref[:, cs] = psum
                acc_ref[:, cs] = pv
            else:
                l_ref[:, cs] = alpha * l_ref[:, cs] + psum
                acc_ref[:, cs] = acc_ref[:, cs] * alpha + pv

    def keys(k_ref, off, n):
        return k_ref[0, 0, pl.ds(pl.multiple_of(off, Q_BLOCK), n), :]

    def vals(vt_ref, off, n):
        return vt_ref[0, 0, :, pl.ds(pl.multiple_of(off, Q_BLOCK), n)]

    near = lambda cs: near_ref[0, :, cs]
    kl = lax.broadcasted_iota(jnp.int32, (Q_BLOCK, half), 0)
    qh = lax.broadcasted_iota(jnp.int32, (Q_BLOCK, half), 1) & (Q_BLOCK - 1)
    tri_tab = jnp.where(kl > qh, 0.0, NEG_INF)
    tri = lambda cs: tri_tab
    base = (i - 1) * Q_BLOCK + NSA_PAD

    pad_flag = jnp.where(lane == LANES - 1, 1.0, 0.0).astype(BF16)
    qw = jnp.concatenate([q, jnp.concatenate([pad_flag] * r, axis=0)], axis=1)
    s_cmp = _dot_nt(kc_ref[0, 0, 0], q)
    z_w0 = _dot_nt(keys(kw_ref, base, 2 * Q_BLOCK), qw)
    z_w1 = _dot_nt(keys(kw_ref, base - 2 * Q_BLOCK, 2 * Q_BLOCK), qw)
    z_w2 = _dot_nt(keys(kw_ref, base - 3 * Q_BLOCK, Q_BLOCK), qw)

    cend = lax.broadcasted_iota(jnp.int32, (n_cmp, 1), 0) * CMP_STRIDE + (CMP_BLOCK - 1)
    vis = cend <= qpos
    lc = jnp.where(vis, s_cmp * SCALE, NEG_INF)
    mc = jnp.max(lc, axis=0, keepdims=True)
    ec = jnp.where(vis, jnp.exp(lc - mc), 0.0)
    den = jnp.sum(ec, axis=0, keepdims=True)
    pc = ec / jnp.where(den > 0.0, den, 1.0)
    ocmp_ref[...] = _dot(vct_ref[0, 0, 0], pc.astype(BF16))

    imp = pc[:, 0:Q_BLOCK]
    for hh in range(1, r):
        imp = imp + pc[:, hh * Q_BLOCK:(hh + 1) * Q_BLOCK]
    blk = sum(_dot(mimp_ref[...], part) for part in _split_bf16(imp, 3))
    ids = lax.broadcasted_iota(jnp.int32, (n_sb, Q_BLOCK), 0)
    cur = (i * Q_BLOCK + lax.broadcasted_iota(jnp.int32, (n_sb, Q_BLOCK), 1)) // SEL_BLOCK
    forced = (ids == 0) | (ids == cur) | (ids == cur - 1)
    score = jnp.where(forced, FORCE_SCORE, jnp.where(ids > cur, -FORCE_SCORE, blk))
    sub = 8
    tiles = [score[v * sub:(v + 1) * sub] for v in range(n_sb // sub)]
    cnt = [jnp.zeros((sub, Q_BLOCK), F32) for _ in tiles]
    sub_id = lax.broadcasted_iota(jnp.int32, (sub, Q_BLOCK), 0)
    for jp in range(n_sb):
        row = score[jp:jp + 1]
        for v, tile in enumerate(tiles):
            ge = lambda: jnp.where(row >= tile, 1.0, 0.0)
            gt = lambda: jnp.where(row > tile, 1.0, 0.0)
            if v * sub > jp:
                inc = ge()
            elif v * sub + sub - 1 <= jp:
                inc = gt()
            else:
                inc = jnp.where(sub_id + v * sub > jp, ge(), gt())
            cnt[v] = cnt[v] + inc
    notsel = jnp.where(jnp.concatenate(cnt, axis=0) < n_sel, 0.0, 1.0).astype(BF16)
    place = (lax.broadcasted_iota(jnp.int32, (n_sb, LANES), 0)
             == lax.broadcasted_iota(jnp.int32, (n_sb, LANES), 1)).astype(BF16)
    ext = lax.dot_general(notsel, place, (((0,), (0,)), ((), ())), preferred_element_type=F32)
    ext = jnp.where(lane == LANES - 1, 1.0, ext).astype(BF16)
    qa = jnp.concatenate([q, jnp.concatenate([ext] * r, axis=0)], axis=1)

    chunk(z_w0, vals(vwt_ref, base, 2 * Q_BLOCK), near, True, win_state)
    chunk(z_w1, vals(vwt_ref, base - 2 * Q_BLOCK, 2 * Q_BLOCK), None, False, win_state)
    chunk(z_w2, vals(vwt_ref, base - 3 * Q_BLOCK, Q_BLOCK), tri, False, win_state)

    def scores(k):
        return _dot_nt(k, qa)

    far_keys = 4 * Q_BLOCK
    n_far = (i + 2) // 4
    far_off = lambda c: jnp.maximum(base - (c + 1) * far_keys, 0)
    far_scores = lambda c: scores(keys(ks_ref, far_off(c), far_keys))
    z_near = scores(keys(ks_ref, base, 2 * Q_BLOCK))
    s0_ref[...] = far_scores(0)
    chunk(z_near, vals(vst_ref, base, 2 * Q_BLOCK), near, True, sel_state)

    def sel_far_pair(t, carry):
        a = 2 * t
        s1_ref[...] = far_scores(a + 1)
        chunk(s0_ref, vals(vst_ref, far_off(a), far_keys), None, False, sel_state)
        s0_ref[...] = far_scores(a + 2)
        chunk(s1_ref, vals(vst_ref, far_off(a + 1), far_keys), None, False, sel_state)
        return carry

    lax.fori_loop(0, (n_far + 1) // 2, sel_far_pair, 0)

    gates = _sigmoid(gate_ref[0])
    inv_sel = 1.0 / sel_state[1][...]
    inv_win = 1.0 / win_state[1][...]
    for hh in range(r):
        cs = slice(hh * Q_BLOCK, (hh + 1) * Q_BLOCK)
        gate = lambda branch: gate_ref_row(gates, branch * C_HEADS + g * r + hh)
        out = (gate(0) * ocmp_ref[:, cs] + gate(1) * (sel_state[2][:, cs] * inv_sel[:, cs])
               + gate(2) * (win_state[2][:, cs] * inv_win[:, cs]))
        o_ref[0, :, hh * HEAD_DIM:(hh + 1) * HEAD_DIM] = out.T.astype(o_ref.dtype)


def gate_ref_row(gates, idx):
    sel = lax.broadcasted_iota(jnp.int32, gates.shape, 0) == idx
    return jnp.sum(jnp.where(sel, gates, 0.0), axis=0, keepdims=True)


def nsa_attention(qkv, cmp_kv, small, rel_bias, bsz, t):
    assert C_WINDOW == 4 * Q_BLOCK and NSA_PAD >= C_WINDOW
    nb = t // Q_BLOCK
    n_sb = t // SEL_BLOCK
    assert n_sb < LANES and n_sb % 8 == 0
    n_sel = min(N_SELECT, n_sb)
    n_chunk = t // CMP_STRIDE
    g_, r = C_KV_GROUPS, C_HEADS // C_KV_GROUPS
    hd = HEAD_DIM
    rows = r * Q_BLOCK
    ql = jnp.arange(Q_BLOCK)[:, None]
    kl = jnp.arange(2 * Q_BLOCK)[None, :]
    dist = ql + Q_BLOCK - kl
    far = rel_bias.astype(F32)[_t5_bucket(jnp.int32(2 * Q_BLOCK))]
    near = (_bias_table(rel_bias, dist, dist >= 0) - far[:, None, None]) * LOG2E
    near = near.reshape(g_, r, Q_BLOCK, 2 * Q_BLOCK).transpose(0, 3, 1, 2).reshape(g_, 2 * Q_BLOCK, rows)
    tok = np.arange(n_chunk)[None, :]
    blk = np.arange(n_sb)[:, None]
    per = SEL_BLOCK // CMP_STRIDE
    mimp = ((tok // per == blk).astype(np.float32) + ((tok + 1) // per == blk).astype(np.float32))
    mimp[:, n_chunk - 1] = 0.0
    blk_cols = np.where(np.arange(t)[:, None] // SEL_BLOCK == np.arange(LANES)[None, :], -MASK_BIG, 0.0)
    pad_cols = np.zeros((NSA_PAD, 2 * LANES), np.float32)
    pad_cols[:, -1] = -MASK_BIG

    def heads(col):
        return qkv[:, :, col:col + g_ * hd].reshape(bsz, t, g_, hd).transpose(0, 2, 1, 3)

    def with_mask(k, cols):
        cols = jnp.broadcast_to(jnp.asarray(cols, BF16), (bsz, g_, t, LANES))
        pad = jnp.broadcast_to(jnp.asarray(pad_cols, BF16), (bsz, g_, NSA_PAD, 2 * LANES))
        return jnp.concatenate([pad, jnp.concatenate([k, cols], axis=-1)], axis=2)

    def padded_t(v):
        return jnp.pad(v.transpose(0, 1, 3, 2), ((0, 0), (0, 0), (0, 0), (NSA_PAD, 0)))

    c0 = C_HEADS * hd
    ks = with_mask(heads(c0), blk_cols)
    vst = padded_t(heads(c0 + g_ * hd))
    kw = with_mask(heads(c0 + 2 * g_ * hd), np.zeros((t, LANES), np.float32))
    vwt = padded_t(heads(c0 + 3 * g_ * hd))
    vct = cmp_kv[1].transpose(0, 1, 3, 2)[None]
    n_gate = 3 * C_HEADS
    gates_t = small[:, :, :n_gate].transpose(0, 2, 1)
    tp = t + NSA_PAD
    kspec = pl.BlockSpec((1, 1, tp, 2 * LANES), lambda b, g, i: (b, g, 0, 0))
    vspec = pl.BlockSpec((1, 1, hd, tp), lambda b, g, i: (b, g, 0, 0))
    return pl.pallas_call(
        functools.partial(_nsa_kernel, n_sel),
        grid=(bsz, g_, nb),
        in_specs=[pl.BlockSpec((1, Q_BLOCK, r * hd), lambda b, g, i: (b, i, g)),
                  pl.BlockSpec((1, 1, 1, n_chunk, hd), lambda b, g, i: (0, b, g, 0, 0)),
                  pl.BlockSpec((1, 1, 1, hd, n_chunk), lambda b, g, i: (0, b, g, 0, 0)),
                  kspec, vspec, kspec, vspec,
                  pl.BlockSpec((1, n_gate, Q_BLOCK), lambda b, g, i: (b, 0, i)),
                  pl.BlockSpec((1, 2 * Q_BLOCK, rows), lambda b, g, i: (g, 0, 0)),
                  pl.BlockSpec((n_sb, n_chunk), lambda b, g, i: (0, 0))],
        out_specs=pl.BlockSpec((1, Q_BLOCK, r * hd), lambda b, g, i: (b, i, g)),
        out_shape=jax.ShapeDtypeStruct((bsz, t, C_HEADS * hd), BF16),
        scratch_shapes=[pltpu.VMEM((1, rows), F32), pltpu.VMEM((1, rows), F32), pltpu.VMEM((hd, rows), F32),
                        pltpu.VMEM((1, rows), F32), pltpu.VMEM((1, rows), F32), pltpu.VMEM((hd, rows), F32),
                        pltpu.VMEM((hd, rows), F32),
                        pltpu.VMEM((4 * Q_BLOCK, rows), F32), pltpu.VMEM((4 * Q_BLOCK, rows), F32)],
        compiler_params=_cp(("parallel", "parallel", "arbitrary")),
        name="nsa_attention",
    )(qkv, cmp_kv, vct, ks, vst, kw, vwt, gates_t, near, jnp.asarray(mimp, BF16))


def _bmm(a, b):
    return jnp.einsum('cij,cjk->cik', a, b, preferred_element_type=F32)


def _bmm_nt(a, b):
    return jnp.einsum('cik,cjk->cij', a, b, preferred_element_type=F32)


def _bmm_f32(a, b):
    return _bmm(a.astype(BF16), b.astype(BF16))


def _gdn_prep_kernel(al_ref, dt_ref, q_ref, qp_ref, k_ref, kp_ref, v_ref, vp_ref, wq_ref, wk_ref, wv_ref,
                     a_ref, beta_ref, arow_ref, u_ref, w_ref, qg_ref, kd_ref, attn_ref, eg_ref):
    h = pl.program_id(1)
    rb = pl.program_id(2)
    rows = q_ref.shape[1]
    c = GDN_CHUNK
    nc = rows // c
    a_log = al_ref[h]
    dt_b = dt_ref[h]

    def conv_silu(x_ref, xp_ref, w_ref):
        prev = jnp.where(rb == 0, 0.0, xp_ref[0])
        x = jnp.concatenate([prev, x_ref[0]], axis=0)
        w = w_ref[...]
        n0 = prev.shape[0] - (D_CONV - 1)
        y = sum(w[j:j + 1] * x[n0 + j:n0 + j + rows] for j in range(D_CONV))
        return _silu(y)

    def l2(x):
        return x * lax.rsqrt(jnp.sum(x * x, axis=-1, keepdims=True) + EPS)

    q = (l2(conv_silu(q_ref, qp_ref, wq_ref)) * SCALE).reshape(nc, c, HEAD_DIM)
    k = l2(conv_silu(k_ref, kp_ref, wk_ref)).reshape(nc, c, HEAD_DIM)
    v = conv_silu(v_ref, vp_ref, wv_ref).reshape(nc, c, HEAD_DIM)

    beta = _sigmoid(beta_ref[0, 0]).reshape(nc, c, HEAD_DIM)
    g_col = (-jnp.exp(a_log) * _softplus(a_ref[0, 0] + dt_b)).reshape(nc, c, HEAD_DIM)
    g_row = (-jnp.exp(a_log) * _softplus(arow_ref[0, 0] + dt_b)).reshape(nc, c, c)

    ii = lax.broadcasted_iota(jnp.int32, (nc, c, c), 1)
    jj = lax.broadcasted_iota(jnp.int32, (nc, c, c), 2)
    tril = (jj <= ii).astype(BF16)
    triu = (ii <= jj).astype(BF16)
    gam = sum(_bmm(tril, p) for p in _split_bf16(g_col, 3))
    gam_row = sum(_bmm(p, triu) for p in _split_bf16(g_row, 3))
    causal = jj <= ii
    decay = jnp.where(causal, jnp.exp(jnp.where(causal, gam[:, :, :c] - gam_row, 0.0)), 0.0)

    kb = k * beta
    kbf = k.astype(BF16)
    lmat = jnp.where(jj < ii, _bmm_nt(kb.astype(BF16), kbf) * decay, 0.0)
    eye = (ii == jj).astype(F32)
    inv = eye - lmat
    pw = lmat
    for _ in range(int(math.log2(c)) - 1):
        pw = _bmm_f32(pw, pw)
        inv = inv + _bmm_f32(inv, pw)
    u = _bmm_f32(inv, v * beta)
    w = _bmm_f32(inv, kb * jnp.exp(gam))
    attn = _bmm_nt(q.astype(BF16), kbf) * decay
    g_last = jnp.broadcast_to(gam[:, c - 1:c, :], gam.shape)

    u_ref[0, 0] = u.reshape(rows, HEAD_DIM)
    w_ref[0, 0] = w.reshape(rows, HEAD_DIM).astype(BF16)
    qg_ref[0, 0] = (q * jnp.exp(gam)).reshape(rows, HEAD_DIM).astype(BF16)
    kd_ref[0, 0] = (k * jnp.exp(g_last - gam)).reshape(rows, HEAD_DIM).astype(BF16)
    attn_ref[0, 0] = attn.reshape(rows, c).astype(BF16)
    eg_ref[0, 0] = jnp.exp(g_last[:, 0:8, :])


def _gdn_scan_kernel(u_ref, w_ref, qg_ref, kd_ref, attn_ref, eg_ref, z_ref, ng_ref, o_ref, state_ref):
    c = GDN_CHUNK
    h = u_ref.shape[1]
    n = u_ref.shape[2] // c
    ng = ng_ref[...]

    @pl.when(pl.program_id(1) == 0)
    def _():
        state_ref[...] = jnp.zeros_like(state_ref)

    def step(ci, carry):
        off = pl.multiple_of(ci * c, c)
        sl = pl.ds(off, c)
        heads = range(h)
        states = [state_ref[hh] for hh in heads]
        sbs = [s.astype(BF16) for s in states]
        ws = [_dot(w_ref[0, hh, sl, :], sbs[hh]) for hh in heads]
        qs = [_dot(qg_ref[0, hh, sl, :], sbs[hh]) for hh in heads]
        vbs = [(u_ref[0, hh, sl, :] - ws[hh]).astype(BF16) for hh in heads]
        os = [qs[hh] + _dot(attn_ref[0, hh, sl, :], vbs[hh]) for hh in heads]
        upd = [lax.dot_general(kd_ref[0, hh, sl, :], vbs[hh], (((0,), (0,)), ((), ())), preferred_element_type=F32)
               for hh in heads]
        for hh in heads:
            cols = slice(hh * HEAD_DIM, (hh + 1) * HEAD_DIM)
            eg = eg_ref[0, hh, ci]
            decayed = (states[hh].reshape(HEAD_DIM // 8, 8, HEAD_DIM) * eg[None]).reshape(HEAD_DIM, HEAD_DIM)
            state_ref[hh] = decayed + upd[hh]
            o_ref[0, sl, cols] = (_rms(os[hh], ng) * _silu(z_ref[0, sl, cols])).astype(o_ref.dtype)
        return carry

    lax.fori_loop(0, n, step, 0)


def gated_deltanet(wide, small, conv_w, a_log, dt_bias, norm_g, bsz, t, rows=1024):
    c = GDN_CHUNK
    h = D_HEADS
    nblk = t // rows
    qcol = 0
    kcol, vcol, zcol = qcol + h, qcol + 2 * h, qcol + 3 * h
    gate_col = 3 * C_HEADS
    beta_t = small[:, :, gate_col:gate_col + h].transpose(0, 2, 1)
    a_t = small[:, :, gate_col + h:gate_col + 2 * h].transpose(0, 2, 1)
    beta_b = jnp.broadcast_to(beta_t[..., None], (bsz, h, t, HEAD_DIM))
    a_b = jnp.broadcast_to(a_t[..., None], (bsz, h, t, HEAD_DIM))
    a_row = jnp.broadcast_to(a_t.reshape(bsz, h, t // c, 1, c), (bsz, h, t // c, c, c)).reshape(bsz, h, t, c)
    hb = rows // 8
    main = lambda col: pl.BlockSpec((1, rows, HEAD_DIM), lambda b, hh, r: (b, r, col + hh))
    halo = lambda col: pl.BlockSpec((1, 8, HEAD_DIM), lambda b, hh, r: (b, jnp.maximum(r * hb - 1, 0), col + hh))
    cw = lambda off: pl.BlockSpec((D_CONV, HEAD_DIM), lambda b, hh, r: (0, off * h + hh))
    per_tok = lambda width: pl.BlockSpec((1, 1, rows, width), lambda b, hh, r: (b, hh, r, 0))
    smem = pl.BlockSpec(memory_space=pltpu.SMEM)
    shp = lambda width, dt: jax.ShapeDtypeStruct((bsz, h, t, width), dt)
    u, w, qg, kd, attn, eg = pl.pallas_call(
        _gdn_prep_kernel,
        grid=(bsz, h, nblk),
        in_specs=[smem, smem, main(qcol), halo(qcol), main(kcol), halo(kcol), main(vcol), halo(vcol),
                  cw(0), cw(1), cw(2), per_tok(HEAD_DIM), per_tok(HEAD_DIM), per_tok(c)],
        out_specs=[per_tok(HEAD_DIM), per_tok(HEAD_DIM), per_tok(HEAD_DIM), per_tok(HEAD_DIM), per_tok(c),
                   pl.BlockSpec((1, 1, rows // c, 8, HEAD_DIM), lambda b, hh, r: (b, hh, r, 0, 0))],
        out_shape=[shp(HEAD_DIM, F32), shp(HEAD_DIM, BF16), shp(HEAD_DIM, BF16), shp(HEAD_DIM, BF16), shp(c, BF16),
                   jax.ShapeDtypeStruct((bsz, h, t // c, 8, HEAD_DIM), F32)],
        compiler_params=_cp(("parallel", "parallel", "parallel")),
        name="gdn_prep",
    )(a_log.astype(F32), dt_bias.astype(F32), wide, wide, wide, wide, wide, wide,
      conv_w, conv_w, conv_w, a_b, beta_b, a_row)
    blk = lambda width: pl.BlockSpec((1, h, rows, width), lambda b, r: (b, 0, r, 0))
    assert zcol % h == 0
    return pl.pallas_call(
        _gdn_scan_kernel,
        grid=(bsz, nblk),
        in_specs=[blk(HEAD_DIM), blk(HEAD_DIM), blk(HEAD_DIM), blk(HEAD_DIM), blk(c),
                  pl.BlockSpec((1, h, rows // c, 8, HEAD_DIM), lambda b, r: (b, 0, r, 0, 0)),
                  pl.BlockSpec((1, rows, h * HEAD_DIM), lambda b, r: (b, r, zcol // h)),
                  pl.BlockSpec((1, HEAD_DIM), lambda b, r: (0, 0))],
        out_specs=pl.BlockSpec((1, rows, h * HEAD_DIM), lambda b, r: (b, r, 0)),
        out_shape=jax.ShapeDtypeStruct((bsz, t, h * HEAD_DIM), BF16),
        scratch_shapes=[pltpu.VMEM((h, HEAD_DIM, HEAD_DIM), F32)],
        compiler_params=_cp(("parallel", "arbitrary")),
        name="gdn_scan",
    )(u, w, qg, kd, attn, eg, wide, norm_g.astype(F32).reshape(1, HEAD_DIM))


def _pad_cols(w, width):
    return jnp.pad(w, ((0, 0), (0, width - w.shape[1])))


def even_mixer(h, norm_g, w_in, b_forget, sinks, w_out, rel_bias, bsz, t):
    n_big = (A_HEADS + 2 * A_KV_HEADS + 3 * B_HEADS) * HEAD_DIM
    big, small = norm_matmul(h, norm_g, w_in[:, :n_big].astype(BF16), BF16,
                             w_side=_pad_cols(w_in[:, n_big:], LANES).astype(BF16), tn=768)
    big = big.reshape(bsz, t, n_big)
    o_a = swa_attention(big, sinks, rel_bias, bsz, t)
    f_t = small.reshape(bsz, t, LANES)[:, :, :B_HEADS].transpose(0, 2, 1)
    o_b = fox_attention(big, forget_cumsum(f_t, b_forget), bsz, t)
    return out_proj(o_a.reshape(bsz * t, -1), o_b.reshape(bsz * t, -1), w_out.astype(BF16), h)


def odd_mixer(h, norm_g, w_in, cmp_pos, cmp_w1, cmp_w2, conv_w, a_log, dt_bias, gdn_norm, w_out, rel_bias, bsz, t):
    hd = HEAD_DIM
    g = C_KV_GROUPS
    o_q = 0
    o_kcmp = C_HEADS * hd
    o_ksel = o_kcmp + 2 * g * hd
    o_gates = o_ksel + 4 * g * hd
    o_qd = o_gates + 3 * C_HEADS
    o_beta = o_qd + 3 * D_HEADS * hd
    o_z = o_beta + 2 * D_HEADS
    w_bf = jnp.concatenate([w_in[:, o_q:o_kcmp], w_in[:, o_ksel:o_gates]], axis=1)
    w_f32 = jnp.concatenate([w_in[:, o_qd:o_beta], w_in[:, o_z:], w_in[:, o_kcmp:o_ksel]], axis=1)
    w_small = _pad_cols(jnp.concatenate([w_in[:, o_gates:o_qd], w_in[:, o_beta:o_z]], axis=1), LANES)
    qkv = norm_matmul(h, norm_g, w_bf.astype(BF16), BF16).reshape(bsz, t, -1)
    wide, small = norm_matmul(h, norm_g, w_f32.astype(BF16), F32, w_side=w_small.astype(BF16), tn=768)
    wide = wide.reshape(bsz, t, -1)
    small = small.reshape(bsz, t, LANES)

    n_chunk = t // CMP_STRIDE
    cmp_kv = compress_tokens(wide, 4 * D_HEADS, cmp_pos, cmp_w1.astype(BF16), cmp_w2.astype(BF16), bsz, t)
    o_c = nsa_attention(qkv, cmp_kv, small, rel_bias, bsz, t)
    o_d = gated_deltanet(wide, small, conv_w, a_log, dt_bias, gdn_norm, bsz, t)
    return out_proj(o_c.reshape(bsz * t, -1), o_d.reshape(bsz * t, -1), w_out.astype(BF16), h)


def kernel(x, rel_bias, norm_mix, norm_ffn, norm_final, ev_w_in, ev_b_forget, ev_sinks, ev_w_out, od_w_in,
           od_cmp_pos, od_cmp_w1, od_cmp_w2, od_conv_w, od_a_log, od_dt_bias, od_gdn_norm, od_w_out, ffn_w_up,
           ffn_conv_w, ffn_conv_b, ffn_w_down):
    bsz, t, d = x.shape
    h = x.reshape(bsz * t, d)
    depth = norm_mix.shape[0]
    for layer in range(depth):
        j = layer // 2
        if layer % 2 == 0:
            h = even_mixer(h, norm_mix[layer], ev_w_in[j], ev_b_forget[j], ev_sinks[j], ev_w_out[j], rel_bias,
                           bsz, t)
        else:
            h = odd_mixer(h, norm_mix[layer], od_w_in[j], od_cmp_pos[j], od_cmp_w1[j], od_cmp_w2[j], od_conv_w[j],
                          od_a_log[j], od_dt_bias[j], od_gdn_norm[j], od_w_out[j], rel_bias, bsz, t)
        h = conv_ffn(h, norm_ffn[layer], ffn_w_up[layer].astype(BF16), ffn_conv_w[layer], ffn_conv_b[layer],
                     ffn_w_down[layer].astype(BF16), t, final_g=norm_final if layer == depth - 1 else None)
    return h.reshape(bsz, t, d)
```

```python
import functools
import math

import jax
import jax.numpy as jnp
import numpy as np
from jax import lax
from jax.experimental import pallas as pl
from jax.experimental.pallas import tpu as pltpu

D_MODEL = 2048
DEPTH = 4
HEAD_DIM = 128
A_HEADS = 8
A_KV_HEADS = 2
A_WINDOW = 128
B_HEADS = 8
C_HEADS = 8
C_KV_GROUPS = 2
CMP_BLOCK = 32
CMP_STRIDE = 16
CMP_HIDDEN = 256
SEL_BLOCK = 64
N_SELECT = 8
C_WINDOW = 512
D_HEADS = 8
D_CONV = 4
GDN_CHUNK = 64
NUM_BUCKETS = 32
MAX_DISTANCE = 128
D_FF = 11 * D_MODEL // 4
FFN_CONV = 3
Q_BLOCK = 128
EPS = 1e-6
NEG_INF = -1e30
FORCE_SCORE = 1e9
SCALE = HEAD_DIM ** -0.5
LOG2E = math.log2(math.e)
MASK_BIG = 2.0 ** 100

F32 = jnp.float32
BF16 = jnp.bfloat16
LANES = 128
HALO = 16
NORM_ROWS = 256
FFN_PIECE = 256
VMEM_LIMIT = 52 * 1024 * 1024


def _cp(dims, vmem=VMEM_LIMIT):
    return pltpu.CompilerParams(dimension_semantics=dims, vmem_limit_bytes=vmem)


def _dot(a, b):
    return jnp.dot(a, b, preferred_element_type=F32)


def _dot_nt(a, b):
    return lax.dot_general(a, b, (((1,), (1,)), ((), ())), preferred_element_type=F32)


def _rms(x, g):
    return x * lax.rsqrt(jnp.mean(x * x, axis=-1, keepdims=True) + EPS) * g


def _sigmoid(x):
    return 1.0 / (1.0 + jnp.exp(-x))


def _silu(x):
    return x * _sigmoid(x)


def _softplus(x):
    return jnp.maximum(x, 0.0) + jnp.log1p(jnp.exp(-jnp.abs(x)))


def _split_bf16(x, parts):
    out = []
    for _ in range(parts - 1):
        p = x.astype(BF16)
        out.append(p)
        x = x - p.astype(F32)
    out.append(x.astype(BF16))
    return out


def _norm_matmul_kernel(has_side, x_ref, g_ref, w_ref, *rest):
    if has_side:
        ws_ref, o_ref, os_ref, xn_ref = rest
    else:
        o_ref, xn_ref = rest

    @pl.when(pl.program_id(1) == 0)
    def _():
        for r0 in range(0, x_ref.shape[0], NORM_ROWS):
            rs = slice(r0, r0 + NORM_ROWS)
            xn_ref[rs, :] = _rms(x_ref[rs, :], g_ref[...]).astype(BF16)
        if has_side:
            os_ref[...] = _dot(xn_ref[...], ws_ref[...])

    o_ref[...] = _dot(xn_ref[...], w_ref[...]).astype(o_ref.dtype)


def norm_matmul(x, g, w, out_dtype, w_side=None, tm=1024, tn=512):
    m, k = x.shape
    n = w.shape[1]
    assert m % tm == 0 and n % tn == 0 and tm % NORM_ROWS == 0
    in_specs = [pl.BlockSpec((tm, k), lambda i, j: (i, 0), pipeline_mode=pl.Buffered(1)),
                pl.BlockSpec((1, k), lambda i, j: (0, 0)),
                pl.BlockSpec((k, tn), lambda i, j: (0, j))]
    out_specs = [pl.BlockSpec((tm, tn), lambda i, j: (i, j))]
    out_shape = [jax.ShapeDtypeStruct((m, n), out_dtype)]
    args = [x, g.reshape(1, k), w]
    if w_side is not None:
        ns = w_side.shape[1]
        in_specs.append(pl.BlockSpec((k, ns), lambda i, j: (0, 0)))
        out_specs.append(pl.BlockSpec((tm, ns), lambda i, j: (i, 0)))
        out_shape.append(jax.ShapeDtypeStruct((m, ns), F32))
        args.append(w_side)
    out = pl.pallas_call(
        functools.partial(_norm_matmul_kernel, w_side is not None),
        grid=(m // tm, n // tn),
        in_specs=in_specs,
        out_specs=out_specs,
        out_shape=out_shape,
        scratch_shapes=[pltpu.VMEM((tm, k), BF16)],
        compiler_params=_cp(("parallel", "arbitrary")),
        name="norm_matmul",
    )(*args)
    return out if w_side is not None else out[0]


def _out_proj_kernel(a1_ref, a2_ref, w1_ref, w2_ref, h_ref, o_ref):
    o_ref[...] = h_ref[...] + _dot(a1_ref[...], w1_ref[...]) + _dot(a2_ref[...], w2_ref[...])


def out_proj(a1, a2, w, h, tm=512):
    m, k1 = a1.shape
    k2 = a2.shape[1]
    n = w.shape[1]
    assert k1 == k2 and w.shape[0] == k1 + k2
    return pl.pallas_call(
        _out_proj_kernel,
        grid=(m // tm,),
        in_specs=[pl.BlockSpec((tm, k1), lambda i: (i, 0)),
                  pl.BlockSpec((tm, k2), lambda i: (i, 0)),
                  pl.BlockSpec((k1, n), lambda i: (0, 0)),
                  pl.BlockSpec((k2, n), lambda i: (1, 0)),
                  pl.BlockSpec((tm, n), lambda i: (i, 0))],
        out_specs=pl.BlockSpec((tm, n), lambda i: (i, 0)),
        out_shape=jax.ShapeDtypeStruct((m, n), F32),
        compiler_params=_cp(("parallel",)),
        name="out_proj",
    )(a1, a2, w, w, h)


def _ffn_kernel(seq_tiles, final, h_ref, hp_ref, g_ref, wu_ref, wg_ref, cw_ref, cb_ref, wd_ref, fg_ref, o_ref,
                xn_ref):
    i = pl.program_id(0)
    j = pl.program_id(1)
    tm = h_ref.shape[0]

    @pl.when(j == 0)
    def _():
        keep = (i % seq_tiles != 0).astype(F32)
        xn_ref[0:HALO, :] = (_rms(hp_ref[...], g_ref[...]) * keep).astype(BF16)
        for r0 in range(0, tm, NORM_ROWS):
            xn_ref[HALO + r0:HALO + r0 + NORM_ROWS, :] = _rms(h_ref[r0:r0 + NORM_ROWS, :], g_ref[...]).astype(BF16)
        o_ref[...] = h_ref[...]

    tf = wu_ref.shape[1]
    pieces = [slice(c0, c0 + FFN_PIECE) for c0 in range(0, tf, FFN_PIECE)]
    up = [(_dot(xn_ref[HALO:, :], wu_ref[:, cs]),
           _dot(xn_ref[...], wg_ref[:, cs]))
          for cs in pieces]
    down = None
    for cs, (u, ge) in zip(pieces, up):
        cw = cw_ref[:, cs]
        gc = (cw[0:1] * ge[HALO - 2:HALO - 2 + tm] + cw[1:2] * ge[HALO - 1:HALO - 1 + tm]
              + cw[2:3] * ge[HALO:] + cb_ref[:, cs])
        part = _dot((_silu(gc) * u).astype(BF16), wd_ref[cs, :])
        down = part if down is None else down + part
    o_ref[...] += down

    if final:
        @pl.when(j == pl.num_programs(1) - 1)
        def _():
            for r0 in range(0, tm, NORM_ROWS):
                rs = slice(r0, r0 + NORM_ROWS)
                o_ref[rs, :] = _rms(o_ref[rs, :], fg_ref[...])


def conv_ffn(h, g, w_up, conv_w, conv_b, w_down, seq, final_g=None, tm=1024, tf=512):
    m, k = h.shape
    dff = w_down.shape[0]
    assert m % tm == 0 and dff % tf == 0 and seq % tm == 0 and tm % HALO == 0 and tm % NORM_ROWS == 0
    hb = tm // HALO
    nf = dff // tf
    fg = (g if final_g is None else final_g).reshape(1, k)
    return pl.pallas_call(
        functools.partial(_ffn_kernel, seq // tm, final_g is not None),
        grid=(m // tm, dff // tf),
        in_specs=[pl.BlockSpec((tm, k), lambda i, j: (i, 0), pipeline_mode=pl.Buffered(1)),
                  pl.BlockSpec((HALO, k), lambda i, j: (jnp.maximum(i * hb - 1, 0), 0)),
                  pl.BlockSpec((1, k), lambda i, j: (0, 0)),
                  pl.BlockSpec((k, tf), lambda i, j: (0, j)),
                  pl.BlockSpec((k, tf), lambda i, j: (0, nf + j)),
                  pl.BlockSpec((FFN_CONV, tf), lambda i, j: (0, j)),
                  pl.BlockSpec((1, tf), lambda i, j: (0, j)),
                  pl.BlockSpec((tf, k), lambda i, j: (j, 0)),
                  pl.BlockSpec((1, k), lambda i, j: (0, 0))],
        out_specs=pl.BlockSpec((tm, k), lambda i, j: (i, 0), pipeline_mode=pl.Buffered(1)),
        out_shape=jax.ShapeDtypeStruct((m, k), F32),
        scratch_shapes=[pltpu.VMEM((HALO + tm, k), BF16)],
        compiler_params=_cp(("parallel", "arbitrary")),
        name="conv_ffn",
    )(h, h, g.reshape(1, k), w_up, w_up, conv_w, conv_b.reshape(1, dff), w_down, fg)


def _t5_bucket(dist):
    max_exact = NUM_BUCKETS // 2
    n = jnp.maximum(dist, 0)
    log_ratio = jnp.log(jnp.maximum(n, 1).astype(F32) / max_exact) / math.log(MAX_DISTANCE / max_exact)
    large = jnp.minimum(max_exact + (log_ratio * (NUM_BUCKETS - max_exact)).astype(jnp.int32), NUM_BUCKETS - 1)
    return jnp.where(n < max_exact, n, large)


def _bias_table(rel_bias, dist, mask):
    b = rel_bias.astype(F32)[_t5_bucket(dist)].transpose(2, 0, 1)
    return jnp.where(mask[None], b, NEG_INF)


def _swa_kernel(sink_ref, q_ref, kp_ref, kc_ref, vp_ref, vc_ref, bias_ref, o_ref):
    g = pl.program_id(1)
    i = pl.program_id(2)
    r = A_HEADS // A_KV_HEADS
    k = jnp.concatenate([kp_ref[0], kc_ref[0]], axis=0)
    v = jnp.concatenate([vp_ref[0], vc_ref[0]], axis=0)
    col = lax.broadcasted_iota(jnp.int32, (Q_BLOCK, 2 * Q_BLOCK), 1)
    first = jnp.logical_and(i == 0, col < Q_BLOCK)
    scores = [_dot_nt(q_ref[0, :, hh * HEAD_DIM:(hh + 1) * HEAD_DIM], k) for hh in range(r)]
    probs = []
    for hh in range(r):
        logits = jnp.where(first, NEG_INF, scores[hh] * SCALE + bias_ref[0, hh])
        sink = sink_ref[g * r + hh]
        m = jnp.maximum(jnp.max(logits, axis=-1, keepdims=True), sink)
        e = jnp.exp(logits - m)
        p = e / (jnp.sum(e, axis=-1, keepdims=True) + jnp.exp(sink - m))
        probs.append(p.astype(BF16))
    for hh in range(r):
        o_ref[0, :, hh * HEAD_DIM:(hh + 1) * HEAD_DIM] = _dot(probs[hh], v).astype(o_ref.dtype)


def swa_attention(big, sinks, rel_bias, bsz, t):
    nb = t // Q_BLOCK
    r = A_HEADS // A_KV_HEADS
    ql = jnp.arange(Q_BLOCK)[:, None]
    kl = jnp.arange(2 * Q_BLOCK)[None, :]
    dist = ql + A_WINDOW - kl
    table = _bias_table(rel_bias, dist, (dist >= 0) & (dist < A_WINDOW))
    table = table.reshape(A_KV_HEADS, r, Q_BLOCK, 2 * Q_BLOCK)
    kcol = A_HEADS * HEAD_DIM // LANES
    vcol = kcol + A_KV_HEADS
    prev = lambda i: jnp.maximum(i - 1, 0)
    return pl.pallas_call(
        _swa_kernel,
        grid=(bsz, A_KV_HEADS, nb),
        in_specs=[pl.BlockSpec(memory_space=pltpu.SMEM),
                  pl.BlockSpec((1, Q_BLOCK, r * HEAD_DIM), lambda b, g, i: (b, i, g)),
                  pl.BlockSpec((1, Q_BLOCK, HEAD_DIM), lambda b, g, i: (b, prev(i), kcol + g)),
                  pl.BlockSpec((1, Q_BLOCK, HEAD_DIM), lambda b, g, i: (b, i, kcol + g)),
                  pl.BlockSpec((1, Q_BLOCK, HEAD_DIM), lambda b, g, i: (b, prev(i), vcol + g)),
                  pl.BlockSpec((1, Q_BLOCK, HEAD_DIM), lambda b, g, i: (b, i, vcol + g)),
                  pl.BlockSpec((1, r, Q_BLOCK, 2 * Q_BLOCK), lambda b, g, i: (g, 0, 0, 0))],
        out_specs=pl.BlockSpec((1, Q_BLOCK, r * HEAD_DIM), lambda b, g, i: (b, i, g)),
        out_shape=jax.ShapeDtypeStruct((bsz, t, A_HEADS * HEAD_DIM), BF16),
        compiler_params=_cp(("parallel", "parallel", "arbitrary")),
        name="swa_attention",
    )(sinks.astype(F32), big, big, big, big, big, table)


C_TERMS = 3
FOX_HEADS_PER_STEP = 2


def _forget_cumsum_kernel(f_ref, b_ref, c_ref):
    x = f_ref[0] + b_ref[...]
    y = jnp.minimum(x, 0.0) - jnp.log1p(jnp.exp(-jnp.abs(x)))
    t = y.shape[1]
    lane = lax.broadcasted_iota(jnp.int32, y.shape, 1)
    s = 1
    while s < t:
        y = y + jnp.where(lane >= s, pltpu.roll(y, s, 1), 0.0)
        s *= 2
    for n, part in enumerate(_split_bf16(y * (-1.0 / SCALE), C_TERMS)):
        c_ref[0, n] = part


def forget_cumsum(f_t, b_forget):
    bsz, h, t = f_t.shape
    return pl.pallas_call(
        _forget_cumsum_kernel,
        grid=(bsz,),
        in_specs=[pl.BlockSpec((1, h, t), lambda b: (b, 0, 0)), pl.BlockSpec((h, 1), lambda b: (0, 0))],
        out_specs=pl.BlockSpec((1, C_TERMS, h, t), lambda b: (b, 0, 0, 0)),
        out_shape=jax.ShapeDtypeStruct((bsz, C_TERMS, h, t), BF16),
        compiler_params=_cp(("parallel",)),
        name="forget_cumsum",
    )(f_t, b_forget.astype(F32).reshape(h, 1))


def _fox_kernel(tq, q_ref, k_ref, ck_ref, vt_ref, o_ref, m_ref, l_ref, acc_ref, s0_ref, s1_ref):
    i = pl.program_id(2)
    c2 = SCALE * LOG2E
    half = tq // 2
    heads = range(ck_ref.shape[1])
    lane = lax.broadcasted_iota(jnp.int32, (tq, LANES), 1)
    ones = jnp.where(lane <= C_TERMS, 1.0, 0.0).astype(BF16)
    qa = [jnp.concatenate([q_ref[0, :, hd * HEAD_DIM:(hd + 1) * HEAD_DIM], ones], axis=1) for hd in heads]
    krow = lax.broadcasted_iota(jnp.int32, (tq, half), 0)
    qcol = lax.broadcasted_iota(jnp.int32, (tq, half), 1)
    last = jnp.maximum(i - 1, 0)

    def key_off(j, diag=False):
        return pl.multiple_of((j if diag else jnp.minimum(j, last)) * tq, tq)

    def scores(hd, j, diag=False):
        valid = True if diag else j < i
        ck_off = pl.multiple_of(jnp.where(valid, (j + 1) * tq, 0), tq)
        k = jnp.concatenate([k_ref[0, pl.ds(key_off(j, diag), tq), hd * HEAD_DIM:(hd + 1) * HEAD_DIM],
                             ck_ref[0, hd, pl.ds(ck_off, tq), :]], axis=1)
        return _dot_nt(k, qa[hd])

    def chunk(hd, s_src, j, diag):
        vt = vt_ref[0, hd, :, pl.ds(key_off(j, diag), tq)]
        for hf in range(2):
            cs = slice(hf * half, (hf + 1) * half)
            s = s_src[:, cs]
            if diag:
                s = jnp.where(krow <= qcol + hf * half, s, NEG_INF)
            zmax = jnp.max(s, axis=0, keepdims=True) * c2
            if diag:
                m_new = zmax
            else:
                m_old = m_ref[hd:hd + 1, cs]
                m_new = jnp.maximum(m_old, zmax)
                alpha = jnp.exp2(m_old - m_new)
            p = jnp.exp2(s * c2 - m_new)
            psum = jnp.sum(p, axis=0, keepdims=True)
            pv = _dot(vt, p.astype(BF16))
            m_ref[hd:hd + 1, cs] = m_new
            if diag:
                l_ref[hd:hd + 1, cs] = psum
                acc_ref[hd, :, cs] = pv
            else:
                l_ref[hd:hd + 1, cs] = alpha * l_ref[hd:hd + 1, cs] + psum
                acc_ref[hd, :, cs] = acc_ref[hd, :, cs] * alpha + pv

    s_diag = [scores(hd, i, diag=True) for hd in heads]
    for hd in heads:
        s0_ref[hd] = scores(hd, 0)
    for hd in heads:
        chunk(hd, s_diag[hd], i, True)

    def far_pair(t, carry):
        a = 2 * t
        for hd in heads:
            s1_ref[hd] = scores(hd, a + 1)
        for hd in heads:
            chunk(hd, s0_ref.at[hd], a, False)
        for hd in heads:
            s0_ref[hd] = scores(hd, a + 2)
        for hd in heads:
            chunk(hd, s1_ref.at[hd], a + 1, False)
        return carry

    lax.fori_loop(0, (i + 1) // 2, far_pair, 0)
    for hd in heads:
        out = acc_ref[hd] * (1.0 / l_ref[hd:hd + 1, :])
        for n in range(tq // HEAD_DIM):
            rs = slice(n * HEAD_DIM, (n + 1) * HEAD_DIM)
            o_ref[0, rs, hd * HEAD_DIM:(hd + 1) * HEAD_DIM] = out[:, rs].T.astype(o_ref.dtype)


def fox_attention(big, c_parts, bsz, t, tq=512):
    qcol = (A_HEADS + 2 * A_KV_HEADS) * HEAD_DIM // LANES
    kcol = qcol + B_HEADS
    vcol = kcol + B_HEADS
    mask_rows = np.zeros((tq, LANES), np.float32)
    mask_rows[:, C_TERMS] = -MASK_BIG
    ck = jnp.pad(c_parts.transpose(0, 2, 3, 1), ((0, 0), (0, 0), (0, 0), (0, LANES - C_TERMS)))
    ck = jnp.concatenate([jnp.broadcast_to(jnp.asarray(mask_rows, BF16), (bsz, B_HEADS, tq, LANES)), ck], axis=2)
    v0 = vcol * LANES
    vt = big[:, :, v0:v0 + B_HEADS * HEAD_DIM].reshape(bsz, t, B_HEADS, HEAD_DIM).transpose(0, 2, 3, 1)
    nh = FOX_HEADS_PER_STEP
    width = nh * HEAD_DIM
    assert B_HEADS % nh == 0 and qcol % nh == 0 and kcol % nh == 0
    return pl.pallas_call(
        functools.partial(_fox_kernel, tq),
        grid=(bsz, B_HEADS // nh, t // tq),
        in_specs=[pl.BlockSpec((1, tq, width), lambda b, h, i: (b, i, qcol // nh + h)),
                  pl.BlockSpec((1, t, width), lambda b, h, i: (b, 0, kcol // nh + h)),
                  pl.BlockSpec((1, nh, tq + t, LANES), lambda b, h, i: (b, h, 0, 0)),
                  pl.BlockSpec((1, nh, HEAD_DIM, t), lambda b, h, i: (b, h, 0, 0))],
        out_specs=pl.BlockSpec((1, tq, width), lambda b, h, i: (b, i, h)),
        scratch_shapes=[pltpu.VMEM((nh, tq), F32), pltpu.VMEM((nh, tq), F32), pltpu.VMEM((nh, HEAD_DIM, tq), F32),
                        pltpu.VMEM((nh, tq, tq), F32), pltpu.VMEM((nh, tq, tq), F32)],
        out_shape=jax.ShapeDtypeStruct((bsz, t, B_HEADS * HEAD_DIM), BF16),
        compiler_params=_cp(("parallel", "parallel", "arbitrary")),
        name="fox_attention",
    )(big, big, ck, vt)


def _compress_kernel(x_ref, pe_ref, w1_ref, w2_ref, o_ref):
    n = x_ref.shape[1] // CMP_STRIDE
    a = jnp.zeros((n, CMP_HIDDEN), F32)
    b = jnp.zeros((n, CMP_HIDDEN), F32)
    for m in range(CMP_STRIDE):
        x = x_ref[0, pl.ds(m, n, stride=CMP_STRIDE), :]
        rows = slice(m * HEAD_DIM, (m + 1) * HEAD_DIM)
        rows_b = slice((CMP_STRIDE + m) * HEAD_DIM, (CMP_STRIDE + m + 1) * HEAD_DIM)
        a = a + _dot((x + pe_ref[0, m:m + 1, :]).astype(BF16), w1_ref[0, rows, :])
        b = b + _dot((x + pe_ref[0, CMP_STRIDE + m:CMP_STRIDE + m + 1, :]).astype(BF16), w1_ref[0, rows_b, :])
    hid = a + pltpu.roll(b, n - 1, 0)
    hid = jax.nn.gelu(hid, approximate=True)
    out = _dot(hid.astype(BF16), w2_ref[0])
    row = lax.broadcasted_iota(jnp.int32, out.shape, 0)
    o_ref[0, 0, 0] = jnp.where(row < n - 1, out, 0.0).astype(o_ref.dtype)


def compress_tokens(wide, col0, pe, w1, w2, bsz, t):
    g = C_KV_GROUPS
    n = t // CMP_STRIDE
    return pl.pallas_call(
        _compress_kernel,
        grid=(2, bsz, g),
        in_specs=[pl.BlockSpec((1, t, HEAD_DIM), lambda s, b, gg: (b, 0, col0 + s * g + gg)),
                  pl.BlockSpec((1, CMP_BLOCK, HEAD_DIM), lambda s, b, gg: (s, 0, 0)),
                  pl.BlockSpec((1, CMP_BLOCK * HEAD_DIM, CMP_HIDDEN), lambda s, b, gg: (s, 0, 0)),
                  pl.BlockSpec((1, CMP_HIDDEN, HEAD_DIM), lambda s, b, gg: (s, 0, 0))],
        out_specs=pl.BlockSpec((1, 1, 1, n, HEAD_DIM), lambda s, b, gg: (s, b, gg, 0, 0)),
        out_shape=jax.ShapeDtypeStruct((2, bsz, g, n, HEAD_DIM), BF16),
        compiler_params=_cp(("parallel", "parallel", "parallel")),
        name="nsa_compress",
    )(wide, pe, w1, w2)


NSA_PAD = 4 * Q_BLOCK


def _nsa_kernel(n_sel, q_ref, kc_ref, vct_ref, ks_ref, vst_ref, kw_ref, vwt_ref, gate_ref, near_ref, mimp_ref,
                o_ref, ms_ref, ls_ref, accs_ref, mw_ref, lw_ref, accw_ref, ocmp_ref, s0_ref, s1_ref):
    sel_state = (ms_ref, ls_ref, accs_ref)
    win_state = (mw_ref, lw_ref, accw_ref)
    g = pl.program_id(1)
    i = pl.program_id(2)
    r = C_HEADS // C_KV_GROUPS
    rows = r * Q_BLOCK
    n_sb = mimp_ref.shape[0]
    n_cmp = kc_ref.shape[3]
    c2 = SCALE * LOG2E

    q = jnp.concatenate([q_ref[0, :, hh * HEAD_DIM:(hh + 1) * HEAD_DIM] for hh in range(r)], axis=0)
    qlane = lax.broadcasted_iota(jnp.int32, (1, rows), 1) & (Q_BLOCK - 1)
    qpos = i * Q_BLOCK + qlane
    half = rows // 2
    lane = lax.broadcasted_iota(jnp.int32, (Q_BLOCK, LANES), 1)

    def chunk(z_all, vt, tab, first, state):
        m_ref, l_ref, acc_ref = state
        for hf in range(2):
            cs = slice(hf * half, (hf + 1) * half)
            s = z_all[:, cs]
            if tab is not None:
                z = s * c2 + tab(cs)
                zmax = jnp.max(z, axis=0, keepdims=True)
            else:
                zmax = jnp.max(s, axis=0, keepdims=True) * c2
            if first:
                m_new = zmax
            else:
                m_old = m_ref[:, cs]
                m_new = jnp.maximum(m_old, zmax)
                alpha = jnp.exp2(m_old - m_new)
            p = jnp.exp2(z - m_new) if tab is not None else jnp.exp2(s * c2 - m_new)
            psum = jnp.sum(p, axis=0, keepdims=True)
            pv = _dot(vt, p.astype(BF16))
            m_ref[:, cs] = m_new
            if first:
                l_ref[:, cs] = psum
                acc_ref[:, cs] = pv
            else:
                l_ref[:, cs] = alpha * l_ref[:, cs] + psum
                acc_ref[:, cs] = acc_ref[:, cs] * alpha + pv

    def keys(k_ref, off, n):
        return k_ref[0, 0, pl.ds(pl.multiple_of(off, Q_BLOCK), n), :]

    def vals(vt_ref, off, n):
        return vt_ref[0, 0, :, pl.ds(pl.multiple_of(off, Q_BLOCK), n)]

    near = lambda cs: near_ref[0, :, cs]
    kl = lax.broadcasted_iota(jnp.int32, (Q_BLOCK, half), 0)
    qh = lax.broadcasted_iota(jnp.int32, (Q_BLOCK, half), 1) & (Q_BLOCK - 1)
    tri_tab = jnp.where(kl > qh, 0.0, NEG_INF)
    tri = lambda cs: tri_tab
    base = (i - 1) * Q_BLOCK + NSA_PAD

    pad_flag = jnp.where(lane == LANES - 1, 1.0, 0.0).astype(BF16)
    qw = jnp.concatenate([q, jnp.concatenate([pad_flag] * r, axis=0)], axis=1)
    s_cmp = _dot_nt(kc_ref[0, 0, 0], q)
    z_w0 = _dot_nt(keys(kw_ref, base, 2 * Q_BLOCK), qw)
    z_w1 = _dot_nt(keys(kw_ref, base - 2 * Q_BLOCK, 2 * Q_BLOCK), qw)
    z_w2 = _dot_nt(keys(kw_ref, base - 3 * Q_BLOCK, Q_BLOCK), qw)

    cend = lax.broadcasted_iota(jnp.int32, (n_cmp, 1), 0) * CMP_STRIDE + (CMP_BLOCK - 1)
    vis = cend <= qpos
    lc = jnp.where(vis, s_cmp * SCALE, NEG_INF)
    mc = jnp.max(lc, axis=0, keepdims=True)
    ec = jnp.where(vis, jnp.exp(lc - mc), 0.0)
    den = jnp.sum(ec, axis=0, keepdims=True)
    pc = ec / jnp.where(den > 0.0, den, 1.0)
    ocmp_ref[...] = _dot(vct_ref[0, 0, 0], pc.astype(BF16))

    imp = pc[:, 0:Q_BLOCK]
    for hh in range(1, r):
        imp = imp + pc[:, hh * Q_BLOCK:(hh + 1) * Q_BLOCK]
    blk = sum(_dot(mimp_ref[...], part) for part in _split_bf16(imp, 3))
    ids = lax.broadcasted_iota(jnp.int32, (n_sb, Q_BLOCK), 0)
    cur = (i * Q_BLOCK + lax.broadcasted_iota(jnp.int32, (n_sb, Q_BLOCK), 1)) // SEL_BLOCK
    forced = (ids == 0) | (ids == cur) | (ids == cur - 1)
    score = jnp.where(forced, FORCE_SCORE, jnp.where(ids > cur, -FORCE_SCORE, blk))
    sub = 8
    tiles = [score[v * sub:(v + 1) * sub] for v in range(n_sb // sub)]
    cnt = [jnp.zeros((sub, Q_BLOCK), F32) for _ in tiles]
    sub_id = lax.broadcasted_iota(jnp.int32, (sub, Q_BLOCK), 0)
    for jp in range(n_sb):
        row = score[jp:jp + 1]
        for v, tile in enumerate(tiles):
            ge = lambda: jnp.where(row >= tile, 1.0, 0.0)
            gt = lambda: jnp.where(row > tile, 1.0, 0.0)
            if v * sub > jp:
                inc = ge()
            elif v * sub + sub - 1 <= jp:
                inc = gt()
            else:
                inc = jnp.where(sub_id + v * sub > jp, ge(), gt())
            cnt[v] = cnt[v] + inc
    notsel = jnp.where(jnp.concatenate(cnt, axis=0) < n_sel, 0.0, 1.0).astype(BF16)
    place = (lax.broadcasted_iota(jnp.int32, (n_sb, LANES), 0)
             == lax.broadcasted_iota(jnp.int32, (n_sb, LANES), 1)).astype(BF16)
    ext = lax.dot_general(notsel, place, (((0,), (0,)), ((), ())), preferred_element_type=F32)
    ext = jnp.where(lane == LANES - 1, 1.0, ext).astype(BF16)
    qa = jnp.concatenate([q, jnp.concatenate([ext] * r, axis=0)], axis=1)

    chunk(z_w0, vals(vwt_ref, base, 2 * Q_BLOCK), near, True, win_state)
    chunk(z_w1, vals(vwt_ref, base - 2 * Q_BLOCK, 2 * Q_BLOCK), None, False, win_state)
    chunk(z_w2, vals(vwt_ref, base - 3 * Q_BLOCK, Q_BLOCK), tri, False, win_state)

    def scores(k):
        return _dot_nt(k, qa)

    far_keys = 4 * Q_BLOCK
    n_far = (i + 2) // 4
    far_off = lambda c: jnp.maximum(base - (c + 1) * far_keys, 0)
    far_scores = lambda c: scores(keys(ks_ref, far_off(c), far_keys))
    z_near = scores(keys(ks_ref, base, 2 * Q_BLOCK))
    s0_ref[...] = far_scores(0)
    chunk(z_near, vals(vst_ref, base, 2 * Q_BLOCK), near, True, sel_state)

    def sel_far_pair(t, carry):
        a = 2 * t
        s1_ref[...] = far_scores(a + 1)
        chunk(s0_ref, vals(vst_ref, far_off(a), far_keys), None, False, sel_state)
        s0_ref[...] = far_scores(a + 2)
        chunk(s1_ref, vals(vst_ref, far_off(a + 1), far_keys), None, False, sel_state)
        return carry

    lax.fori_loop(0, (n_far + 1) // 2, sel_far_pair, 0)

    gates = _sigmoid(gate_ref[0])
    inv_sel = 1.0 / sel_state[1][...]
    inv_win = 1.0 / win_state[1][...]
    for hh in range(r):
        cs = slice(hh * Q_BLOCK, (hh + 1) * Q_BLOCK)
        gate = lambda branch: gate_ref_row(gates, branch * C_HEADS + g * r + hh)
        out = (gate(0) * ocmp_ref[:, cs] + gate(1) * (sel_state[2][:, cs] * inv_sel[:, cs])
               + gate(2) * (win_state[2][:, cs] * inv_win[:, cs]))
        o_ref[0, :, hh * HEAD_DIM:(hh + 1) * HEAD_DIM] = out.T.astype(o_ref.dtype)


def gate_ref_row(gates, idx):
    sel = lax.broadcasted_iota(jnp.int32, gates.shape, 0) == idx
    return jnp.sum(jnp.where(sel, gates, 0.0), axis=0, keepdims=True)


def nsa_attention(qkv, cmp_kv, small, rel_bias, bsz, t):
    assert C_WINDOW == 4 * Q_BLOCK and NSA_PAD >= C_WINDOW
    nb = t // Q_BLOCK
    n_sb = t // SEL_BLOCK
    assert n_sb < LANES and n_sb % 8 == 0
    n_sel = min(N_SELECT, n_sb)
    n_chunk = t // CMP_STRIDE
    g_, r = C_KV_GROUPS, C_HEADS // C_KV_GROUPS
    hd = HEAD_DIM
    rows = r * Q_BLOCK
    ql = jnp.arange(Q_BLOCK)[:, None]
    kl = jnp.arange(2 * Q_BLOCK)[None, :]
    dist = ql + Q_BLOCK - kl
    far = rel_bias.astype(F32)[_t5_bucket(jnp.int32(2 * Q_BLOCK))]
    near = (_bias_table(rel_bias, dist, dist >= 0) - far[:, None, None]) * LOG2E
    near = near.reshape(g_, r, Q_BLOCK, 2 * Q_BLOCK).transpose(0, 3, 1, 2).reshape(g_, 2 * Q_BLOCK, rows)
    tok = np.arange(n_chunk)[None, :]
    blk = np.arange(n_sb)[:, None]
    per = SEL_BLOCK // CMP_STRIDE
    mimp = ((tok // per == blk).astype(np.float32) + ((tok + 1) // per == blk).astype(np.float32))
    mimp[:, n_chunk - 1] = 0.0
    blk_cols = np.where(np.arange(t)[:, None] // SEL_BLOCK == np.arange(LANES)[None, :], -MASK_BIG, 0.0)
    pad_cols = np.zeros((NSA_PAD, 2 * LANES), np.float32)
    pad_cols[:, -1] = -MASK_BIG

    def heads(col):
        return qkv[:, :, col:col + g_ * hd].reshape(bsz, t, g_, hd).transpose(0, 2, 1, 3)

    def with_mask(k, cols):
        cols = jnp.broadcast_to(jnp.asarray(cols, BF16), (bsz, g_, t, LANES))
        pad = jnp.broadcast_to(jnp.asarray(pad_cols, BF16), (bsz, g_, NSA_PAD, 2 * LANES))
        return jnp.concatenate([pad, jnp.concatenate([k, cols], axis=-1)], axis=2)

    def padded_t(v):
        return jnp.pad(v.transpose(0, 1, 3, 2), ((0, 0), (0, 0), (0, 0), (NSA_PAD, 0)))

    c0 = C_HEADS * hd
    ks = with_mask(heads(c0), blk_cols)
    vst = padded_t(heads(c0 + g_ * hd))
    kw = with_mask(heads(c0 + 2 * g_ * hd), np.zeros((t, LANES), np.float32))
    vwt = padded_t(heads(c0 + 3 * g_ * hd))
    vct = cmp_kv[1].transpose(0, 1, 3, 2)[None]
    n_gate = 3 * C_HEADS
    gates_t = small[:, :, :n_gate].transpose(0, 2, 1)
    tp = t + NSA_PAD
    kspec = pl.BlockSpec((1, 1, tp, 2 * LANES), lambda b, g, i: (b, g, 0, 0))
    vspec = pl.BlockSpec((1, 1, hd, tp), lambda b, g, i: (b, g, 0, 0))
    return pl.pallas_call(
        functools.partial(_nsa_kernel, n_sel),
        grid=(bsz, g_, nb),
        in_specs=[pl.BlockSpec((1, Q_BLOCK, r * hd), lambda b, g, i: (b, i, g)),
                  pl.BlockSpec((1, 1, 1, n_chunk, hd), lambda b, g, i: (0, b, g, 0, 0)),
                  pl.BlockSpec((1, 1, 1, hd, n_chunk), lambda b, g, i: (0, b, g, 0, 0)),
                  kspec, vspec, kspec, vspec,
                  pl.BlockSpec((1, n_gate, Q_BLOCK), lambda b, g, i: (b, 0, i)),
                  pl.BlockSpec((1, 2 * Q_BLOCK, rows), lambda b, g, i: (g, 0, 0)),
                  pl.BlockSpec((n_sb, n_chunk), lambda b, g, i: (0, 0))],
        out_specs=pl.BlockSpec((1, Q_BLOCK, r * hd), lambda b, g, i: (b, i, g)),
        out_shape=jax.ShapeDtypeStruct((bsz, t, C_HEADS * hd), BF16),
        scratch_shapes=[pltpu.VMEM((1, rows), F32), pltpu.VMEM((1, rows), F32), pltpu.VMEM((hd, rows), F32),
                        pltpu.VMEM((1, rows), F32), pltpu.VMEM((1, rows), F32), pltpu.VMEM((hd, rows), F32),
                        pltpu.VMEM((hd, rows), F32),
                        pltpu.VMEM((4 * Q_BLOCK, rows), F32), pltpu.VMEM((4 * Q_BLOCK, rows), F32)],
        compiler_params=_cp(("parallel", "parallel", "arbitrary")),
        name="nsa_attention",
    )(qkv, cmp_kv, vct, ks, vst, kw, vwt, gates_t, near, jnp.asarray(mimp, BF16))


def _bmm(a, b):
    return jnp.einsum('cij,cjk->cik', a, b, preferred_element_type=F32)


def _bmm_nt(a, b):
    return jnp.einsum('cik,cjk->cij', a, b, preferred_element_type=F32)


def _bmm_f32(a, b):
    return _bmm(a.astype(BF16), b.astype(BF16))


def _gdn_prep_kernel(al_ref, dt_ref, q_ref, qp_ref, k_ref, kp_ref, v_ref, vp_ref, wq_ref, wk_ref, wv_ref,
                     beta_ref, arow_ref, u_ref, w_ref, qg_ref, kd_ref, attn_ref, eg_ref):
    h = pl.program_id(1)
    rb = pl.program_id(2)
    rows = q_ref.shape[1]
    c = GDN_CHUNK
    nc = rows // c
    a_log = al_ref[h]
    dt_b = dt_ref[h]

    def conv_silu(x_ref, xp_ref, w_ref):
        prev = jnp.where(rb == 0, 0.0, xp_ref[0])
        x = jnp.concatenate([prev, x_ref[0]], axis=0)
        w = w_ref[...]
        n0 = prev.shape[0] - (D_CONV - 1)
        y = sum(w[j:j + 1] * x[n0 + j:n0 + j + rows] for j in range(D_CONV))
        return _silu(y)

    def l2(x):
        return x * lax.rsqrt(jnp.sum(x * x, axis=-1, keepdims=True) + EPS)

    q = (l2(conv_silu(q_ref, qp_ref, wq_ref)) * SCALE).reshape(nc, c, HEAD_DIM)
    k = l2(conv_silu(k_ref, kp_ref, wk_ref)).reshape(nc, c, HEAD_DIM)
    v = conv_silu(v_ref, vp_ref, wv_ref).reshape(nc, c, HEAD_DIM)

    beta = _sigmoid(beta_ref[0, 0]).reshape(nc, c, HEAD_DIM)
    g_row = (-jnp.exp(a_log) * _softplus(arow_ref[0, 0] + dt_b)).reshape(nc, c, c)

    ii = lax.broadcasted_iota(jnp.int32, (nc, c, c), 1)
    jj = lax.broadcasted_iota(jnp.int32, (nc, c, c), 2)
    triu = (ii <= jj).astype(BF16)
    gam_row = sum(_bmm(p, triu) for p in _split_bf16(g_row, 3))
    gam = jnp.sum(jnp.where(ii == jj, gam_row, 0.0), axis=2, keepdims=True)
    causal = jj <= ii
    decay = jnp.where(causal, jnp.exp(jnp.where(causal, gam - gam_row, 0.0)), 0.0)

    kb = k * beta
    kbf = k.astype(BF16)
    lmat = jnp.where(jj < ii, _bmm_nt(kb.astype(BF16), kbf) * decay, 0.0)
    eye = (ii == jj).astype(F32)
    inv = eye - lmat
    pw = lmat
    for _ in range(int(math.log2(c)) - 1):
        pw = _bmm_f32(pw, pw)
        inv = inv + _bmm_f32(inv, pw)
    u = _bmm_f32(inv, v * beta)
    egam = jnp.exp(gam)
    w = _bmm_f32(inv, kb * egam)
    attn = _bmm_nt(q.astype(BF16), kbf) * decay
    g_last = gam[:, c - 1:c, :]

    u_ref[0, 0] = u.reshape(rows, HEAD_DIM)
    w_ref[0, 0] = w.reshape(rows, HEAD_DIM).astype(BF16)
    qg_ref[0, 0] = (q * egam).reshape(rows, HEAD_DIM).astype(BF16)
    kd_ref[0, 0] = (k * jnp.exp(g_last - gam)).reshape(rows, HEAD_DIM).astype(BF16)
    attn_ref[0, 0] = attn.reshape(rows, c).astype(BF16)
    eg_ref[0, 0] = jnp.broadcast_to(jnp.exp(g_last), (nc, 8, HEAD_DIM))


def _gdn_scan_kernel(u_ref, w_ref, qg_ref, kd_ref, attn_ref, eg_ref, z_ref, ng_ref, o_ref, state_ref):
    c = GDN_CHUNK
    h = u_ref.shape[1]
    n = u_ref.shape[2] // c
    ng = ng_ref[...]

    @pl.when(pl.program_id(1) == 0)
    def _():
        state_ref[...] = jnp.zeros_like(state_ref)

    def step(ci, carry):
        off = pl.multiple_of(ci * c, c)
        sl = pl.ds(off, c)
        heads = range(h)
        states = [state_ref[hh] for hh in heads]
        sbs = [s.astype(BF16) for s in states]
        ws = [_dot(w_ref[0, hh, sl, :], sbs[hh]) for hh in heads]
        qs = [_dot(qg_ref[0, hh, sl, :], sbs[hh]) for hh in heads]
        vbs = [(u_ref[0, hh, sl, :] - ws[hh]).astype(BF16) for hh in heads]
        os = [qs[hh] + _dot(attn_ref[0, hh, sl, :], vbs[hh]) for hh in heads]
        upd = [lax.dot_general(kd_ref[0, hh, sl, :], vbs[hh], (((0,), (0,)), ((), ())), preferred_element_type=F32)
               for hh in heads]
        for hh in heads:
            cols = slice(hh * HEAD_DIM, (hh + 1) * HEAD_DIM)
            eg = eg_ref[0, hh, ci]
            decayed = (states[hh].reshape(HEAD_DIM // 8, 8, HEAD_DIM) * eg[None]).reshape(HEAD_DIM, HEAD_DIM)
            state_ref[hh] = decayed + upd[hh]
            o_ref[0, sl, cols] = (_rms(os[hh], ng) * _silu(z_ref[0, sl, cols])).astype(o_ref.dtype)
        return carry

    lax.fori_loop(0, n, step, 0)


def gated_deltanet(wide, small, conv_w, a_log, dt_bias, norm_g, bsz, t, rows=1024):
    c = GDN_CHUNK
    h = D_HEADS
    nblk = t // rows
    qcol = 0
    kcol, vcol, zcol = qcol + h, qcol + 2 * h, qcol + 3 * h
    gate_col = 3 * C_HEADS
    beta_t = small[:, :, gate_col:gate_col + h].transpose(0, 2, 1)
    a_t = small[:, :, gate_col + h:gate_col + 2 * h].transpose(0, 2, 1)
    beta_b = jnp.broadcast_to(beta_t[..., None], (bsz, h, t, HEAD_DIM))
    a_row = jnp.broadcast_to(a_t.reshape(bsz, h, t // c, 1, c), (bsz, h, t // c, c, c)).reshape(bsz, h, t, c)
    hb = rows // 8
    main = lambda col: pl.BlockSpec((1, rows, HEAD_DIM), lambda b, hh, r: (b, r, col + hh))
    halo = lambda col: pl.BlockSpec((1, 8, HEAD_DIM), lambda b, hh, r: (b, jnp.maximum(r * hb - 1, 0), col + hh))
    cw = lambda off: pl.BlockSpec((D_CONV, HEAD_DIM), lambda b, hh, r: (0, off * h + hh))
    per_tok = lambda width: pl.BlockSpec((1, 1, rows, width), lambda b, hh, r: (b, hh, r, 0))
    smem = pl.BlockSpec(memory_space=pltpu.SMEM)
    shp = lambda width, dt: jax.ShapeDtypeStruct((bsz, h, t, width), dt)
    u, w, qg, kd, attn, eg = pl.pallas_call(
        _gdn_prep_kernel,
        grid=(bsz, h, nblk),
        in_specs=[smem, smem, main(qcol), halo(qcol), main(kcol), halo(kcol), main(vcol), halo(vcol),
                  cw(0), cw(1), cw(2), per_tok(HEAD_DIM), per_tok(c)],
        out_specs=[per_tok(HEAD_DIM), per_tok(HEAD_DIM), per_tok(HEAD_DIM), per_tok(HEAD_DIM), per_tok(c),
                   pl.BlockSpec((1, 1, rows // c, 8, HEAD_DIM), lambda b, hh, r: (b, hh, r, 0, 0))],
        out_shape=[shp(HEAD_DIM, F32), shp(HEAD_DIM, BF16), shp(HEAD_DIM, BF16), shp(HEAD_DIM, BF16), shp(c, BF16),
                   jax.ShapeDtypeStruct((bsz, h, t // c, 8, HEAD_DIM), F32)],
        compiler_params=_cp(("parallel", "parallel", "parallel")),
        name="gdn_prep",
    )(a_log.astype(F32), dt_bias.astype(F32), wide, wide, wide, wide, wide, wide,
      conv_w, conv_w, conv_w, beta_b, a_row)
    blk = lambda width: pl.BlockSpec((1, h, rows, width), lambda b, r: (b, 0, r, 0))
    assert zcol % h == 0
    return pl.pallas_call(
        _gdn_scan_kernel,
        grid=(bsz, nblk),
        in_specs=[blk(HEAD_DIM), blk(HEAD_DIM), blk(HEAD_DIM), blk(HEAD_DIM), blk(c),
                  pl.BlockSpec((1, h, rows // c, 8, HEAD_DIM), lambda b, r: (b, 0, r, 0, 0)),
                  pl.BlockSpec((1, rows, h * HEAD_DIM), lambda b, r: (b, r, zcol // h)),
                  pl.BlockSpec((1, HEAD_DIM), lambda b, r: (0, 0))],
        out_specs=pl.BlockSpec((1, rows, h * HEAD_DIM), lambda b, r: (b, r, 0)),
        out_shape=jax.ShapeDtypeStruct((bsz, t, h * HEAD_DIM), BF16),
        scratch_shapes=[pltpu.VMEM((h, HEAD_DIM, HEAD_DIM), F32)],
        compiler_params=_cp(("parallel", "arbitrary")),
        name="gdn_scan",
    )(u, w, qg, kd, attn, eg, wide, norm_g.astype(F32).reshape(1, HEAD_DIM))


def _pad_cols(w, width):
    return jnp.pad(w, ((0, 0), (0, width - w.shape[1])))


def even_mixer(h, norm_g, w_in, b_forget, sinks, w_out, rel_bias, bsz, t):
    n_big = (A_HEADS + 2 * A_KV_HEADS + 3 * B_HEADS) * HEAD_DIM
    big, small = norm_matmul(h, norm_g, w_in[:, :n_big].astype(BF16), BF16,
                             w_side=_pad_cols(w_in[:, n_big:], LANES).astype(BF16), tn=768)
    big = big.reshape(bsz, t, n_big)
    o_a = swa_attention(big, sinks, rel_bias, bsz, t)
    f_t = small.reshape(bsz, t, LANES)[:, :, :B_HEADS].transpose(0, 2, 1)
    o_b = fox_attention(big, forget_cumsum(f_t, b_forget), bsz, t)
    return out_proj(o_a.reshape(bsz * t, -1), o_b.reshape(bsz * t, -1), w_out.astype(BF16), h)


def odd_mixer(h, norm_g, w_in, cmp_pos, cmp_w1, cmp_w2, conv_w, a_log, dt_bias, gdn_norm, w_out, rel_bias, bsz, t):
    hd = HEAD_DIM
    g = C_KV_GROUPS
    o_q = 0
    o_kcmp = C_HEADS * hd
    o_ksel = o_kcmp + 2 * g * hd
    o_gates = o_ksel + 4 * g * hd
    o_qd = o_gates + 3 * C_HEADS
    o_beta = o_qd + 3 * D_HEADS * hd
    o_z = o_beta + 2 * D_HEADS
    w_bf = jnp.concatenate([w_in[:, o_q:o_kcmp], w_in[:, o_ksel:o_gates]], axis=1)
    w_f32 = jnp.concatenate([w_in[:, o_qd:o_beta], w_in[:, o_z:], w_in[:, o_kcmp:o_ksel]], axis=1)
    w_small = _pad_cols(jnp.concatenate([w_in[:, o_gates:o_qd], w_in[:, o_beta:o_z]], axis=1), LANES)
    qkv = norm_matmul(h, norm_g, w_bf.astype(BF16), BF16).reshape(bsz, t, -1)
    wide, small = norm_matmul(h, norm_g, w_f32.astype(BF16), F32, w_side=w_small.astype(BF16), tn=768)
    wide = wide.reshape(bsz, t, -1)
    small = small.reshape(bsz, t, LANES)

    n_chunk = t // CMP_STRIDE
    cmp_kv = compress_tokens(wide, 4 * D_HEADS, cmp_pos, cmp_w1.astype(BF16), cmp_w2.astype(BF16), bsz, t)
    o_c = nsa_attention(qkv, cmp_kv, small, rel_bias, bsz, t)
    o_d = gated_deltanet(wide, small, conv_w, a_log, dt_bias, gdn_norm, bsz, t)
    return out_proj(o_c.reshape(bsz * t, -1), o_d.reshape(bsz * t, -1), w_out.astype(BF16), h)


def kernel(x, rel_bias, norm_mix, norm_ffn, norm_final, ev_w_in, ev_b_forget, ev_sinks, ev_w_out, od_w_in,
           od_cmp_pos, od_cmp_w1, od_cmp_w2, od_conv_w, od_a_log, od_dt_bias, od_gdn_norm, od_w_out, ffn_w_up,
           ffn_conv_w, ffn_conv_b, ffn_w_down):
    bsz, t, d = x.shape
    h = x.reshape(bsz * t, d)
    depth = norm_mix.shape[0]
    for layer in range(depth):
        j = layer // 2
        if layer % 2 == 0:
            h = even_mixer(h, norm_mix[layer], ev_w_in[j], ev_b_forget[j], ev_sinks[j], ev_w_out[j], rel_bias,
                           bsz, t)
        else:
            h = odd_mixer(h, norm_mix[layer], od_w_in[j], od_cmp_pos[j], od_cmp_w1[j], od_cmp_w2[j], od_conv_w[j],
                          od_a_log[j], od_dt_bias[j], od_gdn_norm[j], od_w_out[j], rel_bias, bsz, t)
        h = conv_ffn(h, norm_ffn[layer], ffn_w_up[layer].astype(BF16), ffn_conv_w[layer], ffn_conv_b[layer],
                     ffn_w_down[layer].astype(BF16), t, final_g=norm_final if layer == depth - 1 else None)
    return h.reshape(bsz, t, d)
```

```python
import functools
import math

import jax
import jax.numpy as jnp
import numpy as np
from jax import lax
from jax.experimental import pallas as pl
from jax.experimental.pallas import tpu as pltpu

D_MODEL = 2048
DEPTH = 4
HEAD_DIM = 128
A_HEADS = 8
A_KV_HEADS = 2
A_WINDOW = 128
B_HEADS = 8
C_HEADS = 8
C_KV_GROUPS = 2
CMP_BLOCK = 32
CMP_STRIDE = 16
CMP_HIDDEN = 256
SEL_BLOCK = 64
N_SELECT = 8
C_WINDOW = 512
D_HEADS = 8
D_CONV = 4
GDN_CHUNK = 64
NUM_BUCKETS = 32
MAX_DISTANCE = 128
D_FF = 11 * D_MODEL // 4
FFN_CONV = 3
Q_BLOCK = 128
EPS = 1e-6
NEG_INF = -1e30
FORCE_SCORE = 1e9
SCALE = HEAD_DIM ** -0.5
LOG2E = math.log2(math.e)
MASK_BIG = 2.0 ** 100

F32 = jnp.float32
BF16 = jnp.bfloat16
LANES = 128
HALO = 16
NORM_ROWS = 256
FFN_PIECE = 256
VMEM_LIMIT = 52 * 1024 * 1024


def _cp(dims, vmem=VMEM_LIMIT):
    return pltpu.CompilerParams(dimension_semantics=dims, vmem_limit_bytes=vmem)


def _dot(a, b):
    return jnp.dot(a, b, preferred_element_type=F32)


def _dot_nt(a, b):
    return lax.dot_general(a, b, (((1,), (1,)), ((), ())), preferred_element_type=F32)


def _rms(x, g):
    return x * lax.rsqrt(jnp.mean(x * x, axis=-1, keepdims=True) + EPS) * g


def _sigmoid(x):
    return 1.0 / (1.0 + jnp.exp(-x))


def _silu(x):
    return x * _sigmoid(x)


def _softplus(x):
    return jnp.maximum(x, 0.0) + jnp.log1p(jnp.exp(-jnp.abs(x)))


def _split_bf16(x, parts):
    out = []
    for _ in range(parts - 1):
        p = x.astype(BF16)
        out.append(p)
        x = x - p.astype(F32)
    out.append(x.astype(BF16))
    return out


def _norm_matmul_kernel(has_side, x_ref, g_ref, w_ref, *rest):
    if has_side:
        ws_ref, o_ref, os_ref, xn_ref = rest
    else:
        o_ref, xn_ref = rest

    @pl.when(pl.program_id(1) == 0)
    def _():
        for r0 in range(0, x_ref.shape[0], NORM_ROWS):
            rs = slice(r0, r0 + NORM_ROWS)
            xn_ref[rs, :] = _rms(x_ref[rs, :], g_ref[...]).astype(BF16)
        if has_side:
            os_ref[...] = _dot(xn_ref[...], ws_ref[...])

    o_ref[...] = _dot(xn_ref[...], w_ref[...]).astype(o_ref.dtype)


def norm_matmul(x, g, w, out_dtype, w_side=None, tm=1024, tn=512):
    m, k = x.shape
    n = w.shape[1]
    assert m % tm == 0 and n % tn == 0 and tm % NORM_ROWS == 0
    in_specs = [pl.BlockSpec((tm, k), lambda i, j: (i, 0), pipeline_mode=pl.Buffered(1)),
                pl.BlockSpec((1, k), lambda i, j: (0, 0)),
                pl.BlockSpec((k, tn), lambda i, j: (0, j))]
    out_specs = [pl.BlockSpec((tm, tn), lambda i, j: (i, j))]
    out_shape = [jax.ShapeDtypeStruct((m, n), out_dtype)]
    args = [x, g.reshape(1, k), w]
    if w_side is not None:
        ns = w_side.shape[1]
        in_specs.append(pl.BlockSpec((k, ns), lambda i, j: (0, 0)))
        out_specs.append(pl.BlockSpec((tm, ns), lambda i, j: (i, 0)))
        out_shape.append(jax.ShapeDtypeStruct((m, ns), F32))
        args.append(w_side)
    out = pl.pallas_call(
        functools.partial(_norm_matmul_kernel, w_side is not None),
        grid=(m // tm, n // tn),
        in_specs=in_specs,
        out_specs=out_specs,
        out_shape=out_shape,
        scratch_shapes=[pltpu.VMEM((tm, k), BF16)],
        compiler_params=_cp(("parallel", "arbitrary")),
        name="norm_matmul",
    )(*args)
    return out if w_side is not None else out[0]


def _out_proj_kernel(a1_ref, a2_ref, w1_ref, w2_ref, h_ref, o_ref):
    o_ref[...] = h_ref[...] + _dot(a1_ref[...], w1_ref[...]) + _dot(a2_ref[...], w2_ref[...])


def out_proj(a1, a2, w, h, tm=512):
    m, k1 = a1.shape
    k2 = a2.shape[1]
    n = w.shape[1]
    assert k1 == k2 and w.shape[0] == k1 + k2
    return pl.pallas_call(
        _out_proj_kernel,
        grid=(m // tm,),
        in_specs=[pl.BlockSpec((tm, k1), lambda i: (i, 0)),
                  pl.BlockSpec((tm, k2), lambda i: (i, 0)),
                  pl.BlockSpec((k1, n), lambda i: (0, 0)),
                  pl.BlockSpec((k2, n), lambda i: (1, 0)),
                  pl.BlockSpec((tm, n), lambda i: (i, 0))],
        out_specs=pl.BlockSpec((tm, n), lambda i: (i, 0)),
        out_shape=jax.ShapeDtypeStruct((m, n), F32),
        compiler_params=_cp(("parallel",)),
        name="out_proj",
    )(a1, a2, w, w, h)


def _ffn_kernel(seq_tiles, final, h_ref, hp_ref, g_ref, wu_ref, wg_ref, cw_ref, cb_ref, wd_ref, fg_ref, o_ref,
                xn_ref):
    i = pl.program_id(0)
    j = pl.program_id(1)
    tm = h_ref.shape[0]

    @pl.when(j == 0)
    def _():
        keep = (i % seq_tiles != 0).astype(F32)
        xn_ref[0:HALO, :] = (_rms(hp_ref[...], g_ref[...]) * keep).astype(BF16)
        for r0 in range(0, tm, NORM_ROWS):
            xn_ref[HALO + r0:HALO + r0 + NORM_ROWS, :] = _rms(h_ref[r0:r0 + NORM_ROWS, :], g_ref[...]).astype(BF16)
        o_ref[...] = h_ref[...]

    tf = wu_ref.shape[1]
    pieces = [slice(c0, c0 + FFN_PIECE) for c0 in range(0, tf, FFN_PIECE)]
    up = [(_dot(xn_ref[HALO:, :], wu_ref[:, cs]),
           _dot(xn_ref[...], wg_ref[:, cs]))
          for cs in pieces]
    down = None
    for cs, (u, ge) in zip(pieces, up):
        cw = cw_ref[:, cs]
        gc = (cw[0:1] * ge[HALO - 2:HALO - 2 + tm] + cw[1:2] * ge[HALO - 1:HALO - 1 + tm]
              + cw[2:3] * ge[HALO:] + cb_ref[:, cs])
        part = _dot((_silu(gc) * u).astype(BF16), wd_ref[cs, :])
        down = part if down is None else down + part
    o_ref[...] += down

    if final:
        @pl.when(j == pl.num_programs(1) - 1)
        def _():
            for r0 in range(0, tm, NORM_ROWS):
                rs = slice(r0, r0 + NORM_ROWS)
                o_ref[rs, :] = _rms(o_ref[rs, :], fg_ref[...])


def conv_ffn(h, g, w_up, conv_w, conv_b, w_down, seq, final_g=None, tm=1024, tf=512):
    m, k = h.shape
    dff = w_down.shape[0]
    assert m % tm == 0 and dff % tf == 0 and seq % tm == 0 and tm % HALO == 0 and tm % NORM_ROWS == 0
    hb = tm // HALO
    nf = dff // tf
    fg = (g if final_g is None else final_g).reshape(1, k)
    return pl.pallas_call(
        functools.partial(_ffn_kernel, seq // tm, final_g is not None),
        grid=(m // tm, dff // tf),
        in_specs=[pl.BlockSpec((tm, k), lambda i, j: (i, 0), pipeline_mode=pl.Buffered(1)),
                  pl.BlockSpec((HALO, k), lambda i, j: (jnp.maximum(i * hb - 1, 0), 0)),
                  pl.BlockSpec((1, k), lambda i, j: (0, 0)),
                  pl.BlockSpec((k, tf), lambda i, j: (0, j)),
                  pl.BlockSpec((k, tf), lambda i, j: (0, nf + j)),
                  pl.BlockSpec((FFN_CONV, tf), lambda i, j: (0, j)),
                  pl.BlockSpec((1, tf), lambda i, j: (0, j)),
                  pl.BlockSpec((tf, k), lambda i, j: (j, 0)),
                  pl.BlockSpec((1, k), lambda i, j: (0, 0))],
        out_specs=pl.BlockSpec((tm, k), lambda i, j: (i, 0), pipeline_mode=pl.Buffered(1)),
        out_shape=jax.ShapeDtypeStruct((m, k), F32),
        scratch_shapes=[pltpu.VMEM((HALO + tm, k), BF16)],
        compiler_params=_cp(("parallel", "arbitrary")),
        name="conv_ffn",
    )(h, h, g.reshape(1, k), w_up, w_up, conv_w, conv_b.reshape(1, dff), w_down, fg)


def _t5_bucket(dist):
    max_exact = NUM_BUCKETS // 2
    n = jnp.maximum(dist, 0)
    log_ratio = jnp.log(jnp.maximum(n, 1).astype(F32) / max_exact) / math.log(MAX_DISTANCE / max_exact)
    large = jnp.minimum(max_exact + (log_ratio * (NUM_BUCKETS - max_exact)).astype(jnp.int32), NUM_BUCKETS - 1)
    return jnp.where(n < max_exact, n, large)


def _bias_table(rel_bias, dist, mask):
    b = rel_bias.astype(F32)[_t5_bucket(dist)].transpose(2, 0, 1)
    return jnp.where(mask[None], b, NEG_INF)


def _swa_kernel(sink_ref, q_ref, kp_ref, kc_ref, vp_ref, vc_ref, bias_ref, o_ref):
    g = pl.program_id(1)
    i = pl.program_id(2)
    r = A_HEADS // A_KV_HEADS
    k = jnp.concatenate([kp_ref[0], kc_ref[0]], axis=0)
    v = jnp.concatenate([vp_ref[0], vc_ref[0]], axis=0)
    col = lax.broadcasted_iota(jnp.int32, (Q_BLOCK, 2 * Q_BLOCK), 1)
    first = jnp.logical_and(i == 0, col < Q_BLOCK)
    scores = [_dot_nt(q_ref[0, :, hh * HEAD_DIM:(hh + 1) * HEAD_DIM], k) for hh in range(r)]
    probs = []
    for hh in range(r):
        logits = jnp.where(first, NEG_INF, scores[hh] * SCALE + bias_ref[0, hh])
        sink = sink_ref[g * r + hh]
        m = jnp.maximum(jnp.max(logits, axis=-1, keepdims=True), sink)
        e = jnp.exp(logits - m)
        p = e / (jnp.sum(e, axis=-1, keepdims=True) + jnp.exp(sink - m))
        probs.append(p.astype(BF16))
    for hh in range(r):
        o_ref[0, :, hh * HEAD_DIM:(hh + 1) * HEAD_DIM] = _dot(probs[hh], v).astype(o_ref.dtype)


def swa_attention(big, sinks, rel_bias, bsz, t):
    nb = t // Q_BLOCK
    r = A_HEADS // A_KV_HEADS
    ql = jnp.arange(Q_BLOCK)[:, None]
    kl = jnp.arange(2 * Q_BLOCK)[None, :]
    dist = ql + A_WINDOW - kl
    table = _bias_table(rel_bias, dist, (dist >= 0) & (dist < A_WINDOW))
    table = table.reshape(A_KV_HEADS, r, Q_BLOCK, 2 * Q_BLOCK)
    kcol = A_HEADS * HEAD_DIM // LANES
    vcol = kcol + A_KV_HEADS
    prev = lambda i: jnp.maximum(i - 1, 0)
    return pl.pallas_call(
        _swa_kernel,
        grid=(bsz, A_KV_HEADS, nb),
        in_specs=[pl.BlockSpec(memory_space=pltpu.SMEM),
                  pl.BlockSpec((1, Q_BLOCK, r * HEAD_DIM), lambda b, g, i: (b, i, g)),
                  pl.BlockSpec((1, Q_BLOCK, HEAD_DIM), lambda b, g, i: (b, prev(i), kcol + g)),
                  pl.BlockSpec((1, Q_BLOCK, HEAD_DIM), lambda b, g, i: (b, i, kcol + g)),
                  pl.BlockSpec((1, Q_BLOCK, HEAD_DIM), lambda b, g, i: (b, prev(i), vcol + g)),
                  pl.BlockSpec((1, Q_BLOCK, HEAD_DIM), lambda b, g, i: (b, i, vcol + g)),
                  pl.BlockSpec((1, r, Q_BLOCK, 2 * Q_BLOCK), lambda b, g, i: (g, 0, 0, 0))],
        out_specs=pl.BlockSpec((1, Q_BLOCK, r * HEAD_DIM), lambda b, g, i: (b, i, g)),
        out_shape=jax.ShapeDtypeStruct((bsz, t, A_HEADS * HEAD_DIM), BF16),
        compiler_params=_cp(("parallel", "parallel", "arbitrary")),
        name="swa_attention",
    )(sinks.astype(F32), big, big, big, big, big, table)


C_TERMS = 3
FOX_HEADS_PER_STEP = 4


def _forget_cumsum_kernel(f_ref, b_ref, c_ref):
    x = f_ref[0] + b_ref[...]
    y = jnp.minimum(x, 0.0) - jnp.log1p(jnp.exp(-jnp.abs(x)))
    t = y.shape[1]
    lane = lax.broadcasted_iota(jnp.int32, y.shape, 1)
    s = 1
    while s < t:
        y = y + jnp.where(lane >= s, pltpu.roll(y, s, 1), 0.0)
        s *= 2
    for n, part in enumerate(_split_bf16(y * (-1.0 / SCALE), C_TERMS)):
        c_ref[0, n] = part


def forget_cumsum(f_t, b_forget):
    bsz, h, t = f_t.shape
    return pl.pallas_call(
        _forget_cumsum_kernel,
        grid=(bsz,),
        in_specs=[pl.BlockSpec((1, h, t), lambda b: (b, 0, 0)), pl.BlockSpec((h, 1), lambda b: (0, 0))],
        out_specs=pl.BlockSpec((1, C_TERMS, h, t), lambda b: (b, 0, 0, 0)),
        out_shape=jax.ShapeDtypeStruct((bsz, C_TERMS, h, t), BF16),
        compiler_params=_cp(("parallel",)),
        name="forget_cumsum",
    )(f_t, b_forget.astype(F32).reshape(h, 1))


def _fox_kernel(tq, q_ref, k_ref, ck_ref, vt_ref, o_ref, m_ref, l_ref, acc_ref, s0_ref, s1_ref):
    i = pl.program_id(2)
    c2 = SCALE * LOG2E
    half = tq // 2
    heads = range(ck_ref.shape[1])
    lane = lax.broadcasted_iota(jnp.int32, (tq, LANES), 1)
    ones = jnp.where(lane <= C_TERMS, 1.0, 0.0).astype(BF16)
    qa = [jnp.concatenate([q_ref[0, :, hd * HEAD_DIM:(hd + 1) * HEAD_DIM], ones], axis=1) for hd in heads]
    krow = lax.broadcasted_iota(jnp.int32, (tq, half), 0)
    qcol = lax.broadcasted_iota(jnp.int32, (tq, half), 1)
    last = jnp.maximum(i - 1, 0)

    def key_off(j, diag=False):
        return pl.multiple_of((j if diag else jnp.minimum(j, last)) * tq, tq)

    def scores(hd, j, diag=False):
        valid = True if diag else j < i
        ck_off = pl.multiple_of(jnp.where(valid, (j + 1) * tq, 0), tq)
        k = jnp.concatenate([k_ref[0, pl.ds(key_off(j, diag), tq), hd * HEAD_DIM:(hd + 1) * HEAD_DIM],
                             ck_ref[0, hd, pl.ds(ck_off, tq), :]], axis=1)
        return _dot_nt(k, qa[hd])

    def chunk(hd, s_src, j, diag):
        vt = vt_ref[0, hd, :, pl.ds(key_off(j, diag), tq)]
        for hf in range(2):
            cs = slice(hf * half, (hf + 1) * half)
            s = s_src[:, cs]
            if diag:
                s = jnp.where(krow <= qcol + hf * half, s, NEG_INF)
            zmax = jnp.max(s, axis=0, keepdims=True) * c2
            if diag:
                m_new = zmax
            else:
                m_old = m_ref[hd:hd + 1, cs]
                m_new = jnp.maximum(m_old, zmax)
                alpha = jnp.exp2(m_old - m_new)
            p = jnp.exp2(s * c2 - m_new)
            psum = jnp.sum(p, axis=0, keepdims=True)
            pv = _dot(vt, p.astype(BF16))
            m_ref[hd:hd + 1, cs] = m_new
            if diag:
                l_ref[hd:hd + 1, cs] = psum
                acc_ref[hd, :, cs] = pv
            else:
                l_ref[hd:hd + 1, cs] = alpha * l_ref[hd:hd + 1, cs] + psum
                acc_ref[hd, :, cs] = acc_ref[hd, :, cs] * alpha + pv

    s_diag = [scores(hd, i, diag=True) for hd in heads]
    for hd in heads:
        s0_ref[hd] = scores(hd, 0)
    for hd in heads:
        chunk(hd, s_diag[hd], i, True)

    def far_pair(t, carry):
        a = 2 * t
        for hd in heads:
            s1_ref[hd] = scores(hd, a + 1)
        for hd in heads:
            chunk(hd, s0_ref.at[hd], a, False)
        for hd in heads:
            s0_ref[hd] = scores(hd, a + 2)
        for hd in heads:
            chunk(hd, s1_ref.at[hd], a + 1, False)
        return carry

    lax.fori_loop(0, (i + 1) // 2, far_pair, 0)
    for hd in heads:
        out = acc_ref[hd] * (1.0 / l_ref[hd:hd + 1, :])
        for n in range(tq // HEAD_DIM):
            rs = slice(n * HEAD_DIM, (n + 1) * HEAD_DIM)
            o_ref[0, rs, hd * HEAD_DIM:(hd + 1) * HEAD_DIM] = out[:, rs].T.astype(o_ref.dtype)


def fox_attention(big, c_parts, bsz, t, tq=512):
    qcol = (A_HEADS + 2 * A_KV_HEADS) * HEAD_DIM // LANES
    kcol = qcol + B_HEADS
    vcol = kcol + B_HEADS
    mask_rows = np.zeros((tq, LANES), np.float32)
    mask_rows[:, C_TERMS] = -MASK_BIG
    ck = jnp.pad(c_parts.transpose(0, 2, 3, 1), ((0, 0), (0, 0), (0, 0), (0, LANES - C_TERMS)))
    ck = jnp.concatenate([jnp.broadcast_to(jnp.asarray(mask_rows, BF16), (bsz, B_HEADS, tq, LANES)), ck], axis=2)
    v0 = vcol * LANES
    vt = big[:, :, v0:v0 + B_HEADS * HEAD_DIM].reshape(bsz, t, B_HEADS, HEAD_DIM).transpose(0, 2, 3, 1)
    nh = FOX_HEADS_PER_STEP
    width = nh * HEAD_DIM
    assert B_HEADS % nh == 0 and qcol % nh == 0 and kcol % nh == 0
    return pl.pallas_call(
        functools.partial(_fox_kernel, tq),
        grid=(bsz, B_HEADS // nh, t // tq),
        in_specs=[pl.BlockSpec((1, tq, width), lambda b, h, i: (b, i, qcol // nh + h)),
                  pl.BlockSpec((1, t, width), lambda b, h, i: (b, 0, kcol // nh + h)),
                  pl.BlockSpec((1, nh, tq + t, LANES), lambda b, h, i: (b, h, 0, 0)),
                  pl.BlockSpec((1, nh, HEAD_DIM, t), lambda b, h, i: (b, h, 0, 0))],
        out_specs=pl.BlockSpec((1, tq, width), lambda b, h, i: (b, i, h)),
        scratch_shapes=[pltpu.VMEM((nh, tq), F32), pltpu.VMEM((nh, tq), F32), pltpu.VMEM((nh, HEAD_DIM, tq), F32),
                        pltpu.VMEM((nh, tq, tq), F32), pltpu.VMEM((nh, tq, tq), F32)],
        out_shape=jax.ShapeDtypeStruct((bsz, t, B_HEADS * HEAD_DIM), BF16),
        compiler_params=_cp(("parallel", "parallel", "arbitrary")),
        name="fox_attention",
    )(big, big, ck, vt)


def _compress_kernel(x_ref, pe_ref, w1_ref, w2_ref, o_ref):
    n = x_ref.shape[1] // CMP_STRIDE
    a = jnp.zeros((n, CMP_HIDDEN), F32)
    b = jnp.zeros((n, CMP_HIDDEN), F32)
    for m in range(CMP_STRIDE):
        x = x_ref[0, pl.ds(m, n, stride=CMP_STRIDE), :]
        rows = slice(m * HEAD_DIM, (m + 1) * HEAD_DIM)
        rows_b = slice((CMP_STRIDE + m) * HEAD_DIM, (CMP_STRIDE + m + 1) * HEAD_DIM)
        a = a + _dot((x + pe_ref[0, m:m + 1, :]).astype(BF16), w1_ref[0, rows, :])
        b = b + _dot((x + pe_ref[0, CMP_STRIDE + m:CMP_STRIDE + m + 1, :]).astype(BF16), w1_ref[0, rows_b, :])
    hid = a + pltpu.roll(b, n - 1, 0)
    hid = jax.nn.gelu(hid, approximate=True)
    out = _dot(hid.astype(BF16), w2_ref[0])
    row = lax.broadcasted_iota(jnp.int32, out.shape, 0)
    o_ref[0, 0, 0] = jnp.where(row < n - 1, out, 0.0).astype(o_ref.dtype)


def compress_tokens(wide, col0, pe, w1, w2, bsz, t):
    g = C_KV_GROUPS
    n = t // CMP_STRIDE
    return pl.pallas_call(
        _compress_kernel,
        grid=(2, bsz, g),
        in_specs=[pl.BlockSpec((1, t, HEAD_DIM), lambda s, b, gg: (b, 0, col0 + s * g + gg)),
                  pl.BlockSpec((1, CMP_BLOCK, HEAD_DIM), lambda s, b, gg: (s, 0, 0)),
                  pl.BlockSpec((1, CMP_BLOCK * HEAD_DIM, CMP_HIDDEN), lambda s, b, gg: (s, 0, 0)),
                  pl.BlockSpec((1, CMP_HIDDEN, HEAD_DIM), lambda s, b, gg: (s, 0, 0))],
        out_specs=pl.BlockSpec((1, 1, 1, n, HEAD_DIM), lambda s, b, gg: (s, b, gg, 0, 0)),
        out_shape=jax.ShapeDtypeStruct((2, bsz, g, n, HEAD_DIM), BF16),
        compiler_params=_cp(("parallel", "parallel", "parallel")),
        name="nsa_compress",
    )(wide, pe, w1, w2)


NSA_PAD = 4 * Q_BLOCK


def _nsa_kernel(n_sel, q_ref, kc_ref, vct_ref, ks_ref, vst_ref, kw_ref, vwt_ref, gate_ref, near_ref, mimp_ref,
                o_ref, ms_ref, ls_ref, accs_ref, mw_ref, lw_ref, accw_ref, ocmp_ref, s0_ref, s1_ref):
    sel_state = (ms_ref, ls_ref, accs_ref)
    win_state = (mw_ref, lw_ref, accw_ref)
    g = pl.program_id(1)
    i = pl.program_id(2)
    r = C_HEADS // C_KV_GROUPS
    rows = r * Q_BLOCK
    n_sb = mimp_ref.shape[0]
    n_cmp = kc_ref.shape[3]
    c2 = SCALE * LOG2E

    q = jnp.concatenate([q_ref[0, :, hh * HEAD_DIM:(hh + 1) * HEAD_DIM] for hh in range(r)], axis=0)
    qlane = lax.broadcasted_iota(jnp.int32, (1, rows), 1) & (Q_BLOCK - 1)
    qpos = i * Q_BLOCK + qlane
    half = rows // 2
    lane = lax.broadcasted_iota(jnp.int32, (Q_BLOCK, LANES), 1)

    def chunk(z_all, vt, tab, first, state):
        m_ref, l_ref, acc_ref = state
        for hf in range(2):
            cs = slice(hf * half, (hf + 1) * half)
            s = z_all[:, cs]
            if tab is not None:
                z = s * c2 + tab(cs)
                zmax = jnp.max(z, axis=0, keepdims=True)
            else:
                zmax = jnp.max(s, axis=0, keepdims=True) * c2
            if first:
                m_new = zmax
            else:
                m_old = m_ref[:, cs]
                m_new = jnp.maximum(m_old, zmax)
                alpha = jnp.exp2(m_old - m_new)
            p = jnp.exp2(z - m_new) if tab is not None else jnp.exp2(s * c2 - m_new)
            psum = jnp.sum(p, axis=0, keepdims=True)
            pv = _dot(vt, p.astype(BF16))
            m_ref[:, cs] = m_new
            if first:
                l_ref[:, cs] = psum
                acc_ref[:, cs] = pv
            else:
                l_ref[:, cs] = alpha * l_ref[:, cs] + psum
                acc_ref[:, cs] = acc_ref[:, cs] * alpha + pv

    def keys(k_ref, off, n):
        return k_ref[0, 0, pl.ds(pl.multiple_of(off, Q_BLOCK), n), :]

    def vals(vt_ref, off, n):
        return vt_ref[0, 0, :, pl.ds(pl.multiple_of(off, Q_BLOCK), n)]

    near = lambda cs: near_ref[0, :, cs]
    kl = lax.broadcasted_iota(jnp.int32, (Q_BLOCK, half), 0)
    qh = lax.broadcasted_iota(jnp.int32, (Q_BLOCK, half), 1) & (Q_BLOCK - 1)
    tri_tab = jnp.where(kl > qh, 0.0, NEG_INF)
    tri = lambda cs: tri_tab
    base = (i - 1) * Q_BLOCK + NSA_PAD

    pad_flag = jnp.where(lane == LANES - 1, 1.0, 0.0).astype(BF16)
    qw = jnp.concatenate([q, jnp.concatenate([pad_flag] * r, axis=0)], axis=1)
    s_cmp = _dot_nt(kc_ref[0, 0, 0], q)
    z_w0 = _dot_nt(keys(kw_ref, base, 2 * Q_BLOCK), qw)
    z_w1 = _dot_nt(keys(kw_ref, base - 2 * Q_BLOCK, 2 * Q_BLOCK), qw)
    z_w2 = _dot_nt(keys(kw_ref, base - 3 * Q_BLOCK, Q_BLOCK), qw)

    cend = lax.broadcasted_iota(jnp.int32, (n_cmp, 1), 0) * CMP_STRIDE + (CMP_BLOCK - 1)
    vis = cend <= qpos
    lc = jnp.where(vis, s_cmp * SCALE, NEG_INF)
    mc = jnp.max(lc, axis=0, keepdims=True)
    ec = jnp.where(vis, jnp.exp(lc - mc), 0.0)
    den = jnp.sum(ec, axis=0, keepdims=True)
    pc = ec / jnp.where(den > 0.0, den, 1.0)
    ocmp_ref[...] = _dot(vct_ref[0, 0, 0], pc.astype(BF16))

    imp = pc[:, 0:Q_BLOCK]
    for hh in range(1, r):
        imp = imp + pc[:, hh * Q_BLOCK:(hh + 1) * Q_BLOCK]
    blk = sum(_dot(mimp_ref[...], part) for part in _split_bf16(imp, 3))
    ids = lax.broadcasted_iota(jnp.int32, (n_sb, Q_BLOCK), 0)
    cur = (i * Q_BLOCK + lax.broadcasted_iota(jnp.int32, (n_sb, Q_BLOCK), 1)) // SEL_BLOCK
    forced = (ids == 0) | (ids == cur) | (ids == cur - 1)
    score = jnp.where(forced, FORCE_SCORE, jnp.where(ids > cur, -FORCE_SCORE, blk))
    sub = 8
    tiles = [score[v * sub:(v + 1) * sub] for v in range(n_sb // sub)]
    cnt = [jnp.zeros((sub, Q_BLOCK), F32) for _ in tiles]
    sub_id = lax.broadcasted_iota(jnp.int32, (sub, Q_BLOCK), 0)
    for jp in range(n_sb):
        row = score[jp:jp + 1]
        for v, tile in enumerate(tiles):
            ge = lambda: jnp.where(row >= tile, 1.0, 0.0)
            gt = lambda: jnp.where(row > tile, 1.0, 0.0)
            if v * sub > jp:
                inc = ge()
            elif v * sub + sub - 1 <= jp:
                inc = gt()
            else:
                inc = jnp.where(sub_id + v * sub > jp, ge(), gt())
            cnt[v] = cnt[v] + inc
    notsel = jnp.where(jnp.concatenate(cnt, axis=0) < n_sel, 0.0, 1.0).astype(BF16)
    place = (lax.broadcasted_iota(jnp.int32, (n_sb, LANES), 0)
             == lax.broadcasted_iota(jnp.int32, (n_sb, LANES), 1)).astype(BF16)
    ext = lax.dot_general(notsel, place, (((0,), (0,)), ((), ())), preferred_element_type=F32)
    ext = jnp.where(lane == LANES - 1, 1.0, ext).astype(BF16)
    qa = jnp.concatenate([q, jnp.concatenate([ext] * r, axis=0)], axis=1)

    chunk(z_w0, vals(vwt_ref, base, 2 * Q_BLOCK), near, True, win_state)
    chunk(z_w1, vals(vwt_ref, base - 2 * Q_BLOCK, 2 * Q_BLOCK), None, False, win_state)
    chunk(z_w2, vals(vwt_ref, base - 3 * Q_BLOCK, Q_BLOCK), tri, False, win_state)

    def scores(k):
        return _dot_nt(k, qa)

    far_keys = 4 * Q_BLOCK
    n_far = (i + 2) // 4
    far_off = lambda c: jnp.maximum(base - (c + 1) * far_keys, 0)
    far_scores = lambda c: scores(keys(ks_ref, far_off(c), far_keys))
    z_near = scores(keys(ks_ref, base, 2 * Q_BLOCK))
    s0_ref[...] = far_scores(0)
    chunk(z_near, vals(vst_ref, base, 2 * Q_BLOCK), near, True, sel_state)

    def sel_far_pair(t, carry):
        a = 2 * t
        s1_ref[...] = far_scores(a + 1)
        chunk(s0_ref, vals(vst_ref, far_off(a), far_keys), None, False, sel_state)
        s0_ref[...] = far_scores(a + 2)
        chunk(s1_ref, vals(vst_ref, far_off(a + 1), far_keys), None, False, sel_state)
        return carry

    lax.fori_loop(0, (n_far + 1) // 2, sel_far_pair, 0)

    gates = _sigmoid(gate_ref[0])
    inv_sel = 1.0 / sel_state[1][...]
    inv_win = 1.0 / win_state[1][...]
    for hh in range(r):
        cs = slice(hh * Q_BLOCK, (hh + 1) * Q_BLOCK)
        gate = lambda branch: gate_ref_row(gates, branch * C_HEADS + g * r + hh)
        out = (gate(0) * ocmp_ref[:, cs] + gate(1) * (sel_state[2][:, cs] * inv_sel[:, cs])
               + gate(2) * (win_state[2][:, cs] * inv_win[:, cs]))
        o_ref[0, :, hh * HEAD_DIM:(hh + 1) * HEAD_DIM] = out.T.astype(o_ref.dtype)


def gate_ref_row(gates, idx):
    sel = lax.broadcasted_iota(jnp.int32, gates.shape, 0) == idx
    return jnp.sum(jnp.where(sel, gates, 0.0), axis=0, keepdims=True)


def nsa_attention(qkv, cmp_kv, small, rel_bias, bsz, t):
    assert C_WINDOW == 4 * Q_BLOCK and NSA_PAD >= C_WINDOW
    nb = t // Q_BLOCK
    n_sb = t // SEL_BLOCK
    assert n_sb < LANES and n_sb % 8 == 0
    n_sel = min(N_SELECT, n_sb)
    n_chunk = t // CMP_STRIDE
    g_, r = C_KV_GROUPS, C_HEADS // C_KV_GROUPS
    hd = HEAD_DIM
    rows = r * Q_BLOCK
    ql = jnp.arange(Q_BLOCK)[:, None]
    kl = jnp.arange(2 * Q_BLOCK)[None, :]
    dist = ql + Q_BLOCK - kl
    far = rel_bias.astype(F32)[_t5_bucket(jnp.int32(2 * Q_BLOCK))]
    near = (_bias_table(rel_bias, dist, dist >= 0) - far[:, None, None]) * LOG2E
    near = near.reshape(g_, r, Q_BLOCK, 2 * Q_BLOCK).transpose(0, 3, 1, 2).reshape(g_, 2 * Q_BLOCK, rows)
    tok = np.arange(n_chunk)[None, :]
    blk = np.arange(n_sb)[:, None]
    per = SEL_BLOCK // CMP_STRIDE
    mimp = ((tok // per == blk).astype(np.float32) + ((tok + 1) // per == blk).astype(np.float32))
    mimp[:, n_chunk - 1] = 0.0
    blk_cols = np.where(np.arange(t)[:, None] // SEL_BLOCK == np.arange(LANES)[None, :], -MASK_BIG, 0.0)
    pad_cols = np.zeros((NSA_PAD, 2 * LANES), np.float32)
    pad_cols[:, -1] = -MASK_BIG

    def heads(col):
        return qkv[:, :, col:col + g_ * hd].reshape(bsz, t, g_, hd).transpose(0, 2, 1, 3)

    def with_mask(k, cols):
        cols = jnp.broadcast_to(jnp.asarray(cols, BF16), (bsz, g_, t, LANES))
        pad = jnp.broadcast_to(jnp.asarray(pad_cols, BF16), (bsz, g_, NSA_PAD, 2 * LANES))
        return jnp.concatenate([pad, jnp.concatenate([k, cols], axis=-1)], axis=2)

    def padded_t(v):
        return jnp.pad(v.transpose(0, 1, 3, 2), ((0, 0), (0, 0), (0, 0), (NSA_PAD, 0)))

    c0 = C_HEADS * hd
    ks = with_mask(heads(c0), blk_cols)
    vst = padded_t(heads(c0 + g_ * hd))
    kw = with_mask(heads(c0 + 2 * g_ * hd), np.zeros((t, LANES), np.float32))
    vwt = padded_t(heads(c0 + 3 * g_ * hd))
    vct = cmp_kv[1].transpose(0, 1, 3, 2)[None]
    n_gate = 3 * C_HEADS
    gates_t = small[:, :, :n_gate].transpose(0, 2, 1)
    tp = t + NSA_PAD
    kspec = pl.BlockSpec((1, 1, tp, 2 * LANES), lambda b, g, i: (b, g, 0, 0))
    vspec = pl.BlockSpec((1, 1, hd, tp), lambda b, g, i: (b, g, 0, 0))
    return pl.pallas_call(
        functools.partial(_nsa_kernel, n_sel),
        grid=(bsz, g_, nb),
        in_specs=[pl.BlockSpec((1, Q_BLOCK, r * hd), lambda b, g, i: (b, i, g)),
                  pl.BlockSpec((1, 1, 1, n_chunk, hd), lambda b, g, i: (0, b, g, 0, 0)),
                  pl.BlockSpec((1, 1, 1, hd, n_chunk), lambda b, g, i: (0, b, g, 0, 0)),
                  kspec, vspec, kspec, vspec,
                  pl.BlockSpec((1, n_gate, Q_BLOCK), lambda b, g, i: (b, 0, i)),
                  pl.BlockSpec((1, 2 * Q_BLOCK, rows), lambda b, g, i: (g, 0, 0)),
                  pl.BlockSpec((n_sb, n_chunk), lambda b, g, i: (0, 0))],
        out_specs=pl.BlockSpec((1, Q_BLOCK, r * hd), lambda b, g, i: (b, i, g)),
        out_shape=jax.ShapeDtypeStruct((bsz, t, C_HEADS * hd), BF16),
        scratch_shapes=[pltpu.VMEM((1, rows), F32), pltpu.VMEM((1, rows), F32), pltpu.VMEM((hd, rows), F32),
                        pltpu.VMEM((1, rows), F32), pltpu.VMEM((1, rows), F32), pltpu.VMEM((hd, rows), F32),
                        pltpu.VMEM((hd, rows), F32),
                        pltpu.VMEM((4 * Q_BLOCK, rows), F32), pltpu.VMEM((4 * Q_BLOCK, rows), F32)],
        compiler_params=_cp(("parallel", "parallel", "arbitrary")),
        name="nsa_attention",
    )(qkv, cmp_kv, vct, ks, vst, kw, vwt, gates_t, near, jnp.asarray(mimp, BF16))


def _bmm(a, b):
    return jnp.einsum('cij,cjk->cik', a, b, preferred_element_type=F32)


def _bmm_nt(a, b):
    return jnp.einsum('cik,cjk->cij', a, b, preferred_element_type=F32)


def _bmm_f32(a, b):
    return _bmm(a.astype(BF16), b.astype(BF16))


def _gdn_prep_kernel(al_ref, dt_ref, q_ref, qp_ref, k_ref, kp_ref, v_ref, vp_ref, wq_ref, wk_ref, wv_ref,
                     beta_ref, arow_ref, u_ref, w_ref, qg_ref, kd_ref, attn_ref, eg_ref):
    h = pl.program_id(1)
    rb = pl.program_id(2)
    rows = q_ref.shape[1]
    c = GDN_CHUNK
    nc = rows // c
    a_log = al_ref[h]
    dt_b = dt_ref[h]

    def conv_silu(x_ref, xp_ref, w_ref):
        prev = jnp.where(rb == 0, 0.0, xp_ref[0])
        x = jnp.concatenate([prev, x_ref[0]], axis=0)
        w = w_ref[...]
        n0 = prev.shape[0] - (D_CONV - 1)
        y = sum(w[j:j + 1] * x[n0 + j:n0 + j + rows] for j in range(D_CONV))
        return _silu(y)

    def l2(x):
        return x * lax.rsqrt(jnp.sum(x * x, axis=-1, keepdims=True) + EPS)

    q = (l2(conv_silu(q_ref, qp_ref, wq_ref)) * SCALE).reshape(nc, c, HEAD_DIM)
    k = l2(conv_silu(k_ref, kp_ref, wk_ref)).reshape(nc, c, HEAD_DIM)
    v = conv_silu(v_ref, vp_ref, wv_ref).reshape(nc, c, HEAD_DIM)

    beta = _sigmoid(beta_ref[0, 0]).reshape(nc, c, HEAD_DIM)
    g_row = (-jnp.exp(a_log) * _softplus(arow_ref[0, 0] + dt_b)).reshape(nc, c, c)

    ii = lax.broadcasted_iota(jnp.int32, (nc, c, c), 1)
    jj = lax.broadcasted_iota(jnp.int32, (nc, c, c), 2)
    triu = (ii <= jj).astype(BF16)
    gam_row = sum(_bmm(p, triu) for p in _split_bf16(g_row, 3))
    gam = jnp.sum(jnp.where(ii == jj, gam_row, 0.0), axis=2, keepdims=True)
    causal = jj <= ii
    decay = jnp.where(causal, jnp.exp(jnp.where(causal, gam - gam_row, 0.0)), 0.0)

    kb = k * beta
    kbf = k.astype(BF16)
    lmat = jnp.where(jj < ii, _bmm_nt(kb.astype(BF16), kbf) * decay, 0.0)
    eye = (ii == jj).astype(F32)
    inv = eye - lmat
    pw = lmat
    for _ in range(int(math.log2(c)) - 1):
        pw = _bmm_f32(pw, pw)
        inv = inv + _bmm_f32(inv, pw)
    u = _bmm_f32(inv, v * beta)
    egam = jnp.exp(gam)
    w = _bmm_f32(inv, kb * egam)
    attn = _bmm_nt(q.astype(BF16), kbf) * decay
    g_last = gam[:, c - 1:c, :]

    u_ref[0, 0] = u.reshape(rows, HEAD_DIM)
    w_ref[0, 0] = w.reshape(rows, HEAD_DIM).astype(BF16)
    qg_ref[0, 0] = (q * egam).reshape(rows, HEAD_DIM).astype(BF16)
    kd_ref[0, 0] = (k * jnp.exp(g_last - gam)).reshape(rows, HEAD_DIM).astype(BF16)
    attn_ref[0, 0] = attn.reshape(rows, c).astype(BF16)
    eg_ref[0, 0] = jnp.broadcast_to(jnp.exp(g_last), (nc, 8, HEAD_DIM))


def _gdn_scan_kernel(u_ref, w_ref, qg_ref, kd_ref, attn_ref, eg_ref, z_ref, ng_ref, o_ref, state_ref):
    c = GDN_CHUNK
    h = u_ref.shape[1]
    n = u_ref.shape[2] // c
    ng = ng_ref[...]

    @pl.when(pl.program_id(1) == 0)
    def _():
        state_ref[...] = jnp.zeros_like(state_ref)

    def step(ci, carry):
        off = pl.multiple_of(ci * c, c)
        sl = pl.ds(off, c)
        heads = range(h)
        states = [state_ref[hh] for hh in heads]
        sbs = [s.astype(BF16) for s in states]
        ws = [_dot(w_ref[0, hh, sl, :], sbs[hh]) for hh in heads]
        qs = [_dot(qg_ref[0, hh, sl, :], sbs[hh]) for hh in heads]
        vbs = [(u_ref[0, hh, sl, :] - ws[hh]).astype(BF16) for hh in heads]
        os = [qs[hh] + _dot(attn_ref[0, hh, sl, :], vbs[hh]) for hh in heads]
        upd = [lax.dot_general(kd_ref[0, hh, sl, :], vbs[hh], (((0,), (0,)), ((), ())), preferred_element_type=F32)
               for hh in heads]
        for hh in heads:
            cols = slice(hh * HEAD_DIM, (hh + 1) * HEAD_DIM)
            eg = eg_ref[0, hh, ci]
            decayed = (states[hh].reshape(HEAD_DIM // 8, 8, HEAD_DIM) * eg[None]).reshape(HEAD_DIM, HEAD_DIM)
            state_ref[hh] = decayed + upd[hh]
            o_ref[0, sl, cols] = (_rms(os[hh], ng) * _silu(z_ref[0, sl, cols])).astype(o_ref.dtype)
        return carry

    lax.fori_loop(0, n, step, 0)


def gated_deltanet(wide, small, conv_w, a_log, dt_bias, norm_g, bsz, t, rows=1024):
    c = GDN_CHUNK
    h = D_HEADS
    nblk = t // rows
    qcol = 0
    kcol, vcol, zcol = qcol + h, qcol + 2 * h, qcol + 3 * h
    gate_col = 3 * C_HEADS
    beta_t = small[:, :, gate_col:gate_col + h].transpose(0, 2, 1)
    a_t = small[:, :, gate_col + h:gate_col + 2 * h].transpose(0, 2, 1)
    beta_b = jnp.broadcast_to(beta_t[..., None], (bsz, h, t, HEAD_DIM))
    a_row = jnp.broadcast_to(a_t.reshape(bsz, h, t // c, 1, c), (bsz, h, t // c, c, c)).reshape(bsz, h, t, c)
    hb = rows // 8
    main = lambda col: pl.BlockSpec((1, rows, HEAD_DIM), lambda b, hh, r: (b, r, col + hh))
    halo = lambda col: pl.BlockSpec((1, 8, HEAD_DIM), lambda b, hh, r: (b, jnp.maximum(r * hb - 1, 0), col + hh))
    cw = lambda off: pl.BlockSpec((D_CONV, HEAD_DIM), lambda b, hh, r: (0, off * h + hh))
    per_tok = lambda width: pl.BlockSpec((1, 1, rows, width), lambda b, hh, r: (b, hh, r, 0))
    smem = pl.BlockSpec(memory_space=pltpu.SMEM)
    shp = lambda width, dt: jax.ShapeDtypeStruct((bsz, h, t, width), dt)
    u, w, qg, kd, attn, eg = pl.pallas_call(
        _gdn_prep_kernel,
        grid=(bsz, h, nblk),
        in_specs=[smem, smem, main(qcol), halo(qcol), main(kcol), halo(kcol), main(vcol), halo(vcol),
                  cw(0), cw(1), cw(2), per_tok(HEAD_DIM), per_tok(c)],
        out_specs=[per_tok(HEAD_DIM), per_tok(HEAD_DIM), per_tok(HEAD_DIM), per_tok(HEAD_DIM), per_tok(c),
                   pl.BlockSpec((1, 1, rows // c, 8, HEAD_DIM), lambda b, hh, r: (b, hh, r, 0, 0))],
        out_shape=[shp(HEAD_DIM, F32), shp(HEAD_DIM, BF16), shp(HEAD_DIM, BF16), shp(HEAD_DIM, BF16), shp(c, BF16),
                   jax.ShapeDtypeStruct((bsz, h, t // c, 8, HEAD_DIM), F32)],
        compiler_params=_cp(("parallel", "parallel", "parallel")),
        name="gdn_prep",
    )(a_log.astype(F32), dt_bias.astype(F32), wide, wide, wide, wide, wide, wide,
      conv_w, conv_w, conv_w, beta_b, a_row)
    blk = lambda width: pl.BlockSpec((1, h, rows, width), lambda b, r: (b, 0, r, 0))
    assert zcol % h == 0
    return pl.pallas_call(
        _gdn_scan_kernel,
        grid=(bsz, nblk),
        in_specs=[blk(HEAD_DIM), blk(HEAD_DIM), blk(HEAD_DIM), blk(HEAD_DIM), blk(c),
                  pl.BlockSpec((1, h, rows // c, 8, HEAD_DIM), lambda b, r: (b, 0, r, 0, 0)),
                  pl.BlockSpec((1, rows, h * HEAD_DIM), lambda b, r: (b, r, zcol // h)),
                  pl.BlockSpec((1, HEAD_DIM), lambda b, r: (0, 0))],
        out_specs=pl.BlockSpec((1, rows, h * HEAD_DIM), lambda b, r: (b, r, 0)),
        out_shape=jax.ShapeDtypeStruct((bsz, t, h * HEAD_DIM), BF16),
        scratch_shapes=[pltpu.VMEM((h, HEAD_DIM, HEAD_DIM), F32)],
        compiler_params=_cp(("parallel", "arbitrary")),
        name="gdn_scan",
    )(u, w, qg, kd, attn, eg, wide, norm_g.astype(F32).reshape(1, HEAD_DIM))


def _pad_cols(w, width):
    return jnp.pad(w, ((0, 0), (0, width - w.shape[1])))


def even_mixer(h, norm_g, w_in, b_forget, sinks, w_out, rel_bias, bsz, t):
    n_big = (A_HEADS + 2 * A_KV_HEADS + 3 * B_HEADS) * HEAD_DIM
    big, small = norm_matmul(h, norm_g, w_in[:, :n_big].astype(BF16), BF16,
                             w_side=_pad_cols(w_in[:, n_big:], LANES).astype(BF16), tn=768)
    big = big.reshape(bsz, t, n_big)
    o_a = swa_attention(big, sinks, rel_bias, bsz, t)
    f_t = small.reshape(bsz, t, LANES)[:, :, :B_HEADS].transpose(0, 2, 1)
    o_b = fox_attention(big, forget_cumsum(f_t, b_forget), bsz, t)
    return out_proj(o_a.reshape(bsz * t, -1), o_b.reshape(bsz * t, -1), w_out.astype(BF16), h)


def odd_mixer(h, norm_g, w_in, cmp_pos, cmp_w1, cmp_w2, conv_w, a_log, dt_bias, gdn_norm, w_out, rel_bias, bsz, t):
    hd = HEAD_DIM
    g = C_KV_GROUPS
    o_q = 0
    o_kcmp = C_HEADS * hd
    o_ksel = o_kcmp + 2 * g * hd
    o_gates = o_ksel + 4 * g * hd
    o_qd = o_gates + 3 * C_HEADS
    o_beta = o_qd + 3 * D_HEADS * hd
    o_z = o_beta + 2 * D_HEADS
    w_bf = jnp.concatenate([w_in[:, o_q:o_kcmp], w_in[:, o_ksel:o_gates]], axis=1)
    w_f32 = jnp.concatenate([w_in[:, o_qd:o_beta], w_in[:, o_z:], w_in[:, o_kcmp:o_ksel]], axis=1)
    w_small = _pad_cols(jnp.concatenate([w_in[:, o_gates:o_qd], w_in[:, o_beta:o_z]], axis=1), LANES)
    qkv = norm_matmul(h, norm_g, w_bf.astype(BF16), BF16).reshape(bsz, t, -1)
    wide, small = norm_matmul(h, norm_g, w_f32.astype(BF16), F32, w_side=w_small.astype(BF16), tn=768)
    wide = wide.reshape(bsz, t, -1)
    small = small.reshape(bsz, t, LANES)

    n_chunk = t // CMP_STRIDE
    cmp_kv = compress_tokens(wide, 4 * D_HEADS, cmp_pos, cmp_w1.astype(BF16), cmp_w2.astype(BF16), bsz, t)
    o_c = nsa_attention(qkv, cmp_kv, small, rel_bias, bsz, t)
    o_d = gated_deltanet(wide, small, conv_w, a_log, dt_bias, gdn_norm, bsz, t)
    return out_proj(o_c.reshape(bsz * t, -1), o_d.reshape(bsz * t, -1), w_out.astype(BF16), h)


def kernel(x, rel_bias, norm_mix, norm_ffn, norm_final, ev_w_in, ev_b_forget, ev_sinks, ev_w_out, od_w_in,
           od_cmp_pos, od_cmp_w1, od_cmp_w2, od_conv_w, od_a_log, od_dt_bias, od_gdn_norm, od_w_out, ffn_w_up,
           ffn_conv_w, ffn_conv_b, ffn_w_down):
    bsz, t, d = x.shape
    h = x.reshape(bsz * t, d)
    depth = norm_mix.shape[0]
    for layer in range(depth):
        j = layer // 2
        if layer % 2 == 0:
            h = even_mixer(h, norm_mix[layer], ev_w_in[j], ev_b_forget[j], ev_sinks[j], ev_w_out[j], rel_bias,
                           bsz, t)
        else:
            h = odd_mixer(h, norm_mix[layer], od_w_in[j], od_cmp_pos[j], od_cmp_w1[j], od_cmp_w2[j], od_conv_w[j],
                          od_a_log[j], od_dt_bias[j], od_gdn_norm[j], od_w_out[j], rel_bias, bsz, t)
        h = conv_ffn(h, norm_ffn[layer], ffn_w_up[layer].astype(BF16), ffn_conv_w[layer], ffn_conv_b[layer],
                     ffn_w_down[layer].astype(BF16), t, final_g=norm_final if layer == depth - 1 else None)
    return h.reshape(bsz, t, d)
```

```python
import functools
import math

import jax
import jax.numpy as jnp
import numpy as np
from jax import lax
from jax.experimental import pallas as pl
from jax.experimental.pallas import tpu as pltpu

D_MODEL = 2048
DEPTH = 4
HEAD_DIM = 128
A_HEADS = 8
A_KV_HEADS = 2
A_WINDOW = 128
B_HEADS = 8
C_HEADS = 8
C_KV_GROUPS = 2
CMP_BLOCK = 32
CMP_STRIDE = 16
CMP_HIDDEN = 256
SEL_BLOCK = 64
N_SELECT = 8
C_WINDOW = 512
D_HEADS = 8
D_CONV = 4
GDN_CHUNK = 64
NUM_BUCKETS = 32
MAX_DISTANCE = 128
D_FF = 11 * D_MODEL // 4
FFN_CONV = 3
Q_BLOCK = 128
EPS = 1e-6
NEG_INF = -1e30
FORCE_SCORE = 1e9
SCALE = HEAD_DIM ** -0.5
LOG2E = math.log2(math.e)
MASK_BIG = 2.0 ** 100

F32 = jnp.float32
BF16 = jnp.bfloat16
LANES = 128
HALO = 16
NORM_ROWS = 256
FFN_PIECE = 256
INV_BLOCK = 4
VMEM_LIMIT = 52 * 1024 * 1024


def _cp(dims, vmem=VMEM_LIMIT):
    return pltpu.CompilerParams(dimension_semantics=dims, vmem_limit_bytes=vmem)


def _dot(a, b):
    return jnp.dot(a, b, preferred_element_type=F32)


def _dot_nt(a, b):
    return lax.dot_general(a, b, (((1,), (1,)), ((), ())), preferred_element_type=F32)


def _rms(x, g):
    return x * lax.rsqrt(jnp.mean(x * x, axis=-1, keepdims=True) + EPS) * g


def _sigmoid(x):
    return 1.0 / (1.0 + jnp.exp(-x))


def _silu(x):
    return x * _sigmoid(x)


def _softplus(x):
    return jnp.maximum(x, 0.0) + jnp.log1p(jnp.exp(-jnp.abs(x)))


def _split_bf16(x, parts):
    out = []
    for _ in range(parts - 1):
        p = x.astype(BF16)
        out.append(p)
        x = x - p.astype(F32)
    out.append(x.astype(BF16))
    return out


def _norm_matmul_kernel(has_side, x_ref, g_ref, w_ref, *rest):
    if has_side:
        ws_ref, o_ref, os_ref, xn_ref = rest
    else:
        o_ref, xn_ref = rest

    @pl.when(pl.program_id(1) == 0)
    def _():
        for r0 in range(0, x_ref.shape[0], NORM_ROWS):
            rs = slice(r0, r0 + NORM_ROWS)
            xn_ref[rs, :] = _rms(x_ref[rs, :], g_ref[...]).astype(BF16)
        if has_side:
            os_ref[...] = _dot(xn_ref[...], ws_ref[...])

    o_ref[...] = _dot(xn_ref[...], w_ref[...]).astype(o_ref.dtype)


def norm_matmul(x, g, w, out_dtype, w_side=None, tm=1024, tn=512):
    m, k = x.shape
    n = w.shape[1]
    assert m % tm == 0 and n % tn == 0 and tm % NORM_ROWS == 0
    in_specs = [pl.BlockSpec((tm, k), lambda i, j: (i, 0), pipeline_mode=pl.Buffered(1)),
                pl.BlockSpec((1, k), lambda i, j: (0, 0)),
                pl.BlockSpec((k, tn), lambda i, j: (0, j))]
    out_specs = [pl.BlockSpec((tm, tn), lambda i, j: (i, j))]
    out_shape = [jax.ShapeDtypeStruct((m, n), out_dtype)]
    args = [x, g.reshape(1, k), w]
    if w_side is not None:
        ns = w_side.shape[1]
        in_specs.append(pl.BlockSpec((k, ns), lambda i, j: (0, 0)))
        out_specs.append(pl.BlockSpec((tm, ns), lambda i, j: (i, 0)))
        out_shape.append(jax.ShapeDtypeStruct((m, ns), F32))
        args.append(w_side)
    out = pl.pallas_call(
        functools.partial(_norm_matmul_kernel, w_side is not None),
        grid=(m // tm, n // tn),
        in_specs=in_specs,
        out_specs=out_specs,
        out_shape=out_shape,
        scratch_shapes=[pltpu.VMEM((tm, k), BF16)],
        compiler_params=_cp(("parallel", "arbitrary")),
        name="norm_matmul",
    )(*args)
    return out if w_side is not None else out[0]


def _out_proj_kernel(a1_ref, a2_ref, w1_ref, w2_ref, h_ref, o_ref):
    o_ref[...] = h_ref[...] + _dot(a1_ref[...], w1_ref[...]) + _dot(a2_ref[...], w2_ref[...])


def out_proj(a1, a2, w, h, tm=512):
    m, k1 = a1.shape
    k2 = a2.shape[1]
    n = w.shape[1]
    assert k1 == k2 and w.shape[0] == k1 + k2
    return pl.pallas_call(
        _out_proj_kernel,
        grid=(m // tm,),
        in_specs=[pl.BlockSpec((tm, k1), lambda i: (i, 0)),
                  pl.BlockSpec((tm, k2), lambda i: (i, 0)),
                  pl.BlockSpec((k1, n), lambda i: (0, 0)),
                  pl.BlockSpec((k2, n), lambda i: (1, 0)),
                  pl.BlockSpec((tm, n), lambda i: (i, 0))],
        out_specs=pl.BlockSpec((tm, n), lambda i: (i, 0)),
        out_shape=jax.ShapeDtypeStruct((m, n), F32),
        compiler_params=_cp(("parallel",)),
        name="out_proj",
    )(a1, a2, w, w, h)


def _ffn_kernel(seq_tiles, final, h_ref, hp_ref, g_ref, wu_ref, wg_ref, cw_ref, cb_ref, wd_ref, fg_ref, o_ref,
                xn_ref):
    i = pl.program_id(0)
    j = pl.program_id(1)
    tm = h_ref.shape[0]

    @pl.when(j == 0)
    def _():
        keep = (i % seq_tiles != 0).astype(F32)
        xn_ref[0:HALO, :] = (_rms(hp_ref[...], g_ref[...]) * keep).astype(BF16)
        for r0 in range(0, tm, NORM_ROWS):
            xn_ref[HALO + r0:HALO + r0 + NORM_ROWS, :] = _rms(h_ref[r0:r0 + NORM_ROWS, :], g_ref[...]).astype(BF16)
        o_ref[...] = h_ref[...]

    tf = wu_ref.shape[1]
    pieces = [slice(c0, c0 + FFN_PIECE) for c0 in range(0, tf, FFN_PIECE)]
    up = [(_dot(xn_ref[HALO:, :], wu_ref[:, cs]),
           _dot(xn_ref[...], wg_ref[:, cs]))
          for cs in pieces]
    down = None
    for cs, (u, ge) in zip(pieces, up):
        cw = cw_ref[:, cs]
        gc = (cw[0:1] * ge[HALO - 2:HALO - 2 + tm] + cw[1:2] * ge[HALO - 1:HALO - 1 + tm]
              + cw[2:3] * ge[HALO:] + cb_ref[:, cs])
        part = _dot((_silu(gc) * u).astype(BF16), wd_ref[cs, :])
        down = part if down is None else down + part
    o_ref[...] += down

    if final:
        @pl.when(j == pl.num_programs(1) - 1)
        def _():
            for r0 in range(0, tm, NORM_ROWS):
                rs = slice(r0, r0 + NORM_ROWS)
                o_ref[rs, :] = _rms(o_ref[rs, :], fg_ref[...])


def conv_ffn(h, g, w_up, conv_w, conv_b, w_down, seq, final_g=None, tm=1024, tf=512):
    m, k = h.shape
    dff = w_down.shape[0]
    assert m % tm == 0 and dff % tf == 0 and seq % tm == 0 and tm % HALO == 0 and tm % NORM_ROWS == 0
    hb = tm // HALO
    nf = dff // tf
    fg = (g if final_g is None else final_g).reshape(1, k)
    return pl.pallas_call(
        functools.partial(_ffn_kernel, seq // tm, final_g is not None),
        grid=(m // tm, dff // tf),
        in_specs=[pl.BlockSpec((tm, k), lambda i, j: (i, 0), pipeline_mode=pl.Buffered(1)),
                  pl.BlockSpec((HALO, k), lambda i, j: (jnp.maximum(i * hb - 1, 0), 0)),
                  pl.BlockSpec((1, k), lambda i, j: (0, 0)),
                  pl.BlockSpec((k, tf), lambda i, j: (0, j)),
                  pl.BlockSpec((k, tf), lambda i, j: (0, nf + j)),
                  pl.BlockSpec((FFN_CONV, tf), lambda i, j: (0, j)),
                  pl.BlockSpec((1, tf), lambda i, j: (0, j)),
                  pl.BlockSpec((tf, k), lambda i, j: (j, 0)),
                  pl.BlockSpec((1, k), lambda i, j: (0, 0))],
        out_specs=pl.BlockSpec((tm, k), lambda i, j: (i, 0), pipeline_mode=pl.Buffered(1)),
        out_shape=jax.ShapeDtypeStruct((m, k), F32),
        scratch_shapes=[pltpu.VMEM((HALO + tm, k), BF16)],
        compiler_params=_cp(("parallel", "arbitrary")),
        name="conv_ffn",
    )(h, h, g.reshape(1, k), w_up, w_up, conv_w, conv_b.reshape(1, dff), w_down, fg)


def _t5_bucket(dist):
    max_exact = NUM_BUCKETS // 2
    n = jnp.maximum(dist, 0)
    log_ratio = jnp.log(jnp.maximum(n, 1).astype(F32) / max_exact) / math.log(MAX_DISTANCE / max_exact)
    large = jnp.minimum(max_exact + (log_ratio * (NUM_BUCKETS - max_exact)).astype(jnp.int32), NUM_BUCKETS - 1)
    return jnp.where(n < max_exact, n, large)


def _bias_table(rel_bias, dist, mask):
    b = rel_bias.astype(F32)[_t5_bucket(dist)].transpose(2, 0, 1)
    return jnp.where(mask[None], b, NEG_INF)


def _swa_kernel(sink_ref, q_ref, kp_ref, kc_ref, vp_ref, vc_ref, bias_ref, o_ref):
    g = pl.program_id(1)
    i = pl.program_id(2)
    r = A_HEADS // A_KV_HEADS
    k = jnp.concatenate([kp_ref[0], kc_ref[0]], axis=0)
    v = jnp.concatenate([vp_ref[0], vc_ref[0]], axis=0)
    col = lax.broadcasted_iota(jnp.int32, (Q_BLOCK, 2 * Q_BLOCK), 1)
    first = jnp.logical_and(i == 0, col < Q_BLOCK)
    scores = [_dot_nt(q_ref[0, :, hh * HEAD_DIM:(hh + 1) * HEAD_DIM], k) for hh in range(r)]
    probs = []
    for hh in range(r):
        logits = jnp.where(first, NEG_INF, scores[hh] * SCALE + bias_ref[0, hh])
        sink = sink_ref[g * r + hh]
        m = jnp.maximum(jnp.max(logits, axis=-1, keepdims=True), sink)
        e = jnp.exp(logits - m)
        p = e / (jnp.sum(e, axis=-1, keepdims=True) + jnp.exp(sink - m))
        probs.append(p.astype(BF16))
    for hh in range(r):
        o_ref[0, :, hh * HEAD_DIM:(hh + 1) * HEAD_DIM] = _dot(probs[hh], v).astype(o_ref.dtype)


def swa_attention(big, sinks, rel_bias, bsz, t):
    nb = t // Q_BLOCK
    r = A_HEADS // A_KV_HEADS
    ql = jnp.arange(Q_BLOCK)[:, None]
    kl = jnp.arange(2 * Q_BLOCK)[None, :]
    dist = ql + A_WINDOW - kl
    table = _bias_table(rel_bias, dist, (dist >= 0) & (dist < A_WINDOW))
    table = table.reshape(A_KV_HEADS, r, Q_BLOCK, 2 * Q_BLOCK)
    kcol = A_HEADS * HEAD_DIM // LANES
    vcol = kcol + A_KV_HEADS
    prev = lambda i: jnp.maximum(i - 1, 0)
    return pl.pallas_call(
        _swa_kernel,
        grid=(bsz, A_KV_HEADS, nb),
        in_specs=[pl.BlockSpec(memory_space=pltpu.SMEM),
                  pl.BlockSpec((1, Q_BLOCK, r * HEAD_DIM), lambda b, g, i: (b, i, g)),
                  pl.BlockSpec((1, Q_BLOCK, HEAD_DIM), lambda b, g, i: (b, prev(i), kcol + g)),
                  pl.BlockSpec((1, Q_BLOCK, HEAD_DIM), lambda b, g, i: (b, i, kcol + g)),
                  pl.BlockSpec((1, Q_BLOCK, HEAD_DIM), lambda b, g, i: (b, prev(i), vcol + g)),
                  pl.BlockSpec((1, Q_BLOCK, HEAD_DIM), lambda b, g, i: (b, i, vcol + g)),
                  pl.BlockSpec((1, r, Q_BLOCK, 2 * Q_BLOCK), lambda b, g, i: (g, 0, 0, 0))],
        out_specs=pl.BlockSpec((1, Q_BLOCK, r * HEAD_DIM), lambda b, g, i: (b, i, g)),
        out_shape=jax.ShapeDtypeStruct((bsz, t, A_HEADS * HEAD_DIM), BF16),
        compiler_params=_cp(("parallel", "parallel", "arbitrary")),
        name="swa_attention",
    )(sinks.astype(F32), big, big, big, big, big, table)


C_TERMS = 3
FOX_HEADS_PER_STEP = 4


def _forget_cumsum_kernel(f_ref, b_ref, c_ref):
    x = f_ref[0] + b_ref[...]
    y = jnp.minimum(x, 0.0) - jnp.log1p(jnp.exp(-jnp.abs(x)))
    t = y.shape[1]
    lane = lax.broadcasted_iota(jnp.int32, y.shape, 1)
    s = 1
    while s < t:
        y = y + jnp.where(lane >= s, pltpu.roll(y, s, 1), 0.0)
        s *= 2
    for n, part in enumerate(_split_bf16(y * (-1.0 / SCALE), C_TERMS)):
        c_ref[0, n] = part


def forget_cumsum(f_t, b_forget):
    bsz, h, t = f_t.shape
    return pl.pallas_call(
        _forget_cumsum_kernel,
        grid=(bsz,),
        in_specs=[pl.BlockSpec((1, h, t), lambda b: (b, 0, 0)), pl.BlockSpec((h, 1), lambda b: (0, 0))],
        out_specs=pl.BlockSpec((1, C_TERMS, h, t), lambda b: (b, 0, 0, 0)),
        out_shape=jax.ShapeDtypeStruct((bsz, C_TERMS, h, t), BF16),
        compiler_params=_cp(("parallel",)),
        name="forget_cumsum",
    )(f_t, b_forget.astype(F32).reshape(h, 1))


def _fox_kernel(tq, q_ref, k_ref, ck_ref, vt_ref, o_ref, m_ref, l_ref, acc_ref, s0_ref, s1_ref):
    i = pl.program_id(2)
    c2 = SCALE * LOG2E
    half = tq // 2
    heads = range(ck_ref.shape[1])
    lane = lax.broadcasted_iota(jnp.int32, (tq, LANES), 1)
    ones = jnp.where(lane <= C_TERMS, 1.0, 0.0).astype(BF16)
    qa = [jnp.concatenate([q_ref[0, :, hd * HEAD_DIM:(hd + 1) * HEAD_DIM], ones], axis=1) for hd in heads]
    krow = lax.broadcasted_iota(jnp.int32, (tq, half), 0)
    qcol = lax.broadcasted_iota(jnp.int32, (tq, half), 1)
    last = jnp.maximum(i - 1, 0)

    def key_off(j, diag=False):
        return pl.multiple_of((j if diag else jnp.minimum(j, last)) * tq, tq)

    def scores(hd, j, diag=False):
        valid = True if diag else j < i
        ck_off = pl.multiple_of(jnp.where(valid, (j + 1) * tq, 0), tq)
        k = jnp.concatenate([k_ref[0, pl.ds(key_off(j, diag), tq), hd * HEAD_DIM:(hd + 1) * HEAD_DIM],
                             ck_ref[0, hd, pl.ds(ck_off, tq), :]], axis=1)
        return _dot_nt(k, qa[hd])

    def chunk(hd, s_src, j, diag):
        vt = vt_ref[0, hd, :, pl.ds(key_off(j, diag), tq)]
        for hf in range(2):
            cs = slice(hf * half, (hf + 1) * half)
            s = s_src[:, cs]
            if diag:
                s = jnp.where(krow <= qcol + hf * half, s, NEG_INF)
            zmax = jnp.max(s, axis=0, keepdims=True) * c2
            if diag:
                m_new = zmax
            else:
                m_old = m_ref[hd:hd + 1, cs]
                m_new = jnp.maximum(m_old, zmax)
                alpha = jnp.exp2(m_old - m_new)
            p = jnp.exp2(s * c2 - m_new)
            psum = jnp.sum(p, axis=0, keepdims=True)
            pv = _dot(vt, p.astype(BF16))
            m_ref[hd:hd + 1, cs] = m_new
            if diag:
                l_ref[hd:hd + 1, cs] = psum
                acc_ref[hd, :, cs] = pv
            else:
                l_ref[hd:hd + 1, cs] = alpha * l_ref[hd:hd + 1, cs] + psum
                acc_ref[hd, :, cs] = acc_ref[hd, :, cs] * alpha + pv

    s_diag = [scores(hd, i, diag=True) for hd in heads]
    for hd in heads:
        s0_ref[hd] = scores(hd, 0)
    for hd in heads:
        chunk(hd, s_diag[hd], i, True)

    def far_pair(t, carry):
        a = 2 * t
        for hd in heads:
            s1_ref[hd] = scores(hd, a + 1)
        for hd in heads:
            chunk(hd, s0_ref.at[hd], a, False)
        for hd in heads:
            s0_ref[hd] = scores(hd, a + 2)
        for hd in heads:
            chunk(hd, s1_ref.at[hd], a + 1, False)
        return carry

    lax.fori_loop(0, (i + 1) // 2, far_pair, 0)
    for hd in heads:
        out = acc_ref[hd] * (1.0 / l_ref[hd:hd + 1, :])
        for n in range(tq // HEAD_DIM):
            rs = slice(n * HEAD_DIM, (n + 1) * HEAD_DIM)
            o_ref[0, rs, hd * HEAD_DIM:(hd + 1) * HEAD_DIM] = out[:, rs].T.astype(o_ref.dtype)


def fox_attention(big, c_parts, bsz, t, tq=512):
    qcol = (A_HEADS + 2 * A_KV_HEADS) * HEAD_DIM // LANES
    kcol = qcol + B_HEADS
    vcol = kcol + B_HEADS
    mask_rows = np.zeros((tq, LANES), np.float32)
    mask_rows[:, C_TERMS] = -MASK_BIG
    ck = jnp.pad(c_parts.transpose(0, 2, 3, 1), ((0, 0), (0, 0), (0, 0), (0, LANES - C_TERMS)))
    ck = jnp.concatenate([jnp.broadcast_to(jnp.asarray(mask_rows, BF16), (bsz, B_HEADS, tq, LANES)), ck], axis=2)
    v0 = vcol * LANES
    vt = big[:, :, v0:v0 + B_HEADS * HEAD_DIM].reshape(bsz, t, B_HEADS, HEAD_DIM).transpose(0, 2, 3, 1)
    nh = FOX_HEADS_PER_STEP
    width = nh * HEAD_DIM
    assert B_HEADS % nh == 0 and qcol % nh == 0 and kcol % nh == 0
    return pl.pallas_call(
        functools.partial(_fox_kernel, tq),
        grid=(bsz, B_HEADS // nh, t // tq),
        in_specs=[pl.BlockSpec((1, tq, width), lambda b, h, i: (b, i, qcol // nh + h)),
                  pl.BlockSpec((1, t, width), lambda b, h, i: (b, 0, kcol // nh + h)),
                  pl.BlockSpec((1, nh, tq + t, LANES), lambda b, h, i: (b, h, 0, 0)),
                  pl.BlockSpec((1, nh, HEAD_DIM, t), lambda b, h, i: (b, h, 0, 0))],
        out_specs=pl.BlockSpec((1, tq, width), lambda b, h, i: (b, i, h)),
        scratch_shapes=[pltpu.VMEM((nh, tq), F32), pltpu.VMEM((nh, tq), F32), pltpu.VMEM((nh, HEAD_DIM, tq), F32),
                        pltpu.VMEM((nh, tq, tq), F32), pltpu.VMEM((nh, tq, tq), F32)],
        out_shape=jax.ShapeDtypeStruct((bsz, t, B_HEADS * HEAD_DIM), BF16),
        compiler_params=_cp(("parallel", "parallel", "arbitrary")),
        name="fox_attention",
    )(big, big, ck, vt)


def _compress_kernel(x_ref, pe_ref, w1_ref, w2_ref, o_ref):
    n = x_ref.shape[1] // CMP_STRIDE
    a = jnp.zeros((n, CMP_HIDDEN), F32)
    b = jnp.zeros((n, CMP_HIDDEN), F32)
    for m in range(CMP_STRIDE):
        x = x_ref[0, pl.ds(m, n, stride=CMP_STRIDE), :]
        rows = slice(m * HEAD_DIM, (m + 1) * HEAD_DIM)
        rows_b = slice((CMP_STRIDE + m) * HEAD_DIM, (CMP_STRIDE + m + 1) * HEAD_DIM)
        a = a + _dot((x + pe_ref[0, m:m + 1, :]).astype(BF16), w1_ref[0, rows, :])
        b = b + _dot((x + pe_ref[0, CMP_STRIDE + m:CMP_STRIDE + m + 1, :]).astype(BF16), w1_ref[0, rows_b, :])
    hid = a + pltpu.roll(b, n - 1, 0)
    hid = jax.nn.gelu(hid, approximate=True)
    out = _dot(hid.astype(BF16), w2_ref[0])
    row = lax.broadcasted_iota(jnp.int32, out.shape, 0)
    o_ref[0, 0, 0] = jnp.where(row < n - 1, out, 0.0).astype(o_ref.dtype)


def compress_tokens(wide, col0, pe, w1, w2, bsz, t):
    g = C_KV_GROUPS
    n = t // CMP_STRIDE
    return pl.pallas_call(
        _compress_kernel,
        grid=(2, bsz, g),
        in_specs=[pl.BlockSpec((1, t, HEAD_DIM), lambda s, b, gg: (b, 0, col0 + s * g + gg)),
                  pl.BlockSpec((1, CMP_BLOCK, HEAD_DIM), lambda s, b, gg: (s, 0, 0)),
                  pl.BlockSpec((1, CMP_BLOCK * HEAD_DIM, CMP_HIDDEN), lambda s, b, gg: (s, 0, 0)),
                  pl.BlockSpec((1, CMP_HIDDEN, HEAD_DIM), lambda s, b, gg: (s, 0, 0))],
        out_specs=pl.BlockSpec((1, 1, 1, n, HEAD_DIM), lambda s, b, gg: (s, b, gg, 0, 0)),
        out_shape=jax.ShapeDtypeStruct((2, bsz, g, n, HEAD_DIM), BF16),
        compiler_params=_cp(("parallel", "parallel", "parallel")),
        name="nsa_compress",
    )(wide, pe, w1, w2)


NSA_PAD = 4 * Q_BLOCK


def _nsa_kernel(n_sel, q_ref, kc_ref, vct_ref, ks_ref, vst_ref, kw_ref, vwt_ref, gate_ref, near_ref, mimp_ref,
                o_ref, ms_ref, ls_ref, accs_ref, mw_ref, lw_ref, accw_ref, ocmp_ref, s0_ref, s1_ref):
    sel_state = (ms_ref, ls_ref, accs_ref)
    win_state = (mw_ref, lw_ref, accw_ref)
    g = pl.program_id(1)
    i = pl.program_id(2)
    r = C_HEADS // C_KV_GROUPS
    rows = r * Q_BLOCK
    n_sb = mimp_ref.shape[0]
    n_cmp = kc_ref.shape[3]
    c2 = SCALE * LOG2E

    q = jnp.concatenate([q_ref[0, :, hh * HEAD_DIM:(hh + 1) * HEAD_DIM] for hh in range(r)], axis=0)
    qlane = lax.broadcasted_iota(jnp.int32, (1, rows), 1) & (Q_BLOCK - 1)
    qpos = i * Q_BLOCK + qlane
    half = rows // 2
    lane = lax.broadcasted_iota(jnp.int32, (Q_BLOCK, LANES), 1)

    def chunk(z_all, vt, tab, first, state):
        m_ref, l_ref, acc_ref = state
        for hf in range(2):
            cs = slice(hf * half, (hf + 1) * half)
            s = z_all[:, cs]
            if tab is not None:
                z = s * c2 + tab(cs)
                zmax = jnp.max(z, axis=0, keepdims=True)
            else:
                zmax = jnp.max(s, axis=0, keepdims=True) * c2
            if first:
                m_new = zmax
            else:
                m_old = m_ref[:, cs]
                m_new = jnp.maximum(m_old, zmax)
                alpha = jnp.exp2(m_old - m_new)
            p = jnp.exp2(z - m_new) if tab is not None else jnp.exp2(s * c2 - m_new)
            psum = jnp.sum(p, axis=0, keepdims=True)
            pv = _dot(vt, p.astype(BF16))
            m_ref[:, cs] = m_new
            if first:
                l_ref[:, cs] = psum
                acc_ref[:, cs] = pv
            else:
                l_ref[:, cs] = alpha * l_ref[:, cs] + psum
                acc_ref[:, cs] = acc_ref[:, cs] * alpha + pv

    def keys(k_ref, off, n):
        return k_ref[0, 0, pl.ds(pl.multiple_of(off, Q_BLOCK), n), :]

    def vals(vt_ref, off, n):
        return vt_ref[0, 0, :, pl.ds(pl.multiple_of(off, Q_BLOCK), n)]

    near = lambda cs: near_ref[0, :, cs]
    kl = lax.broadcasted_iota(jnp.int32, (Q_BLOCK, half), 0)
    qh = lax.broadcasted_iota(jnp.int32, (Q_BLOCK, half), 1) & (Q_BLOCK - 1)
    tri_tab = jnp.where(kl > qh, 0.0, NEG_INF)
    tri = lambda cs: tri_tab
    base = (i - 1) * Q_BLOCK + NSA_PAD

    pad_flag = jnp.where(lane == LANES - 1, 1.0, 0.0).astype(BF16)
    qw = jnp.concatenate([q, jnp.concatenate([pad_flag] * r, axis=0)], axis=1)
    s_cmp = _dot_nt(kc_ref[0, 0, 0], q)
    z_w0 = _dot_nt(keys(kw_ref, base, 2 * Q_BLOCK), qw)
    z_w1 = _dot_nt(keys(kw_ref, base - 2 * Q_BLOCK, 2 * Q_BLOCK), qw)
    z_w2 = _dot_nt(keys(kw_ref, base - 3 * Q_BLOCK, Q_BLOCK), qw)

    cend = lax.broadcasted_iota(jnp.int32, (n_cmp, 1), 0) * CMP_STRIDE + (CMP_BLOCK - 1)
    vis = cend <= qpos
    lc = jnp.where(vis, s_cmp * SCALE, NEG_INF)
    mc = jnp.max(lc, axis=0, keepdims=True)
    ec = jnp.where(vis, jnp.exp(lc - mc), 0.0)
    den = jnp.sum(ec, axis=0, keepdims=True)
    pc = ec / jnp.where(den > 0.0, den, 1.0)
    ocmp_ref[...] = _dot(vct_ref[0, 0, 0], pc.astype(BF16))

    imp = pc[:, 0:Q_BLOCK]
    for hh in range(1, r):
        imp = imp + pc[:, hh * Q_BLOCK:(hh + 1) * Q_BLOCK]
    blk = sum(_dot(mimp_ref[...], part) for part in _split_bf16(imp, 3))
    ids = lax.broadcasted_iota(jnp.int32, (n_sb, Q_BLOCK), 0)
    cur = (i * Q_BLOCK + lax.broadcasted_iota(jnp.int32, (n_sb, Q_BLOCK), 1)) // SEL_BLOCK
    forced = (ids == 0) | (ids == cur) | (ids == cur - 1)
    score = jnp.where(forced, FORCE_SCORE, jnp.where(ids > cur, -FORCE_SCORE, blk))
    sub = 8
    tiles = [score[v * sub:(v + 1) * sub] for v in range(n_sb // sub)]
    cnt = [jnp.zeros((sub, Q_BLOCK), F32) for _ in tiles]
    sub_id = lax.broadcasted_iota(jnp.int32, (sub, Q_BLOCK), 0)
    for jp in range(n_sb):
        row = score[jp:jp + 1]
        for v, tile in enumerate(tiles):
            ge = lambda: jnp.where(row >= tile, 1.0, 0.0)
            gt = lambda: jnp.where(row > tile, 1.0, 0.0)
            if v * sub > jp:
                inc = ge()
            elif v * sub + sub - 1 <= jp:
                inc = gt()
            else:
                inc = jnp.where(sub_id + v * sub > jp, ge(), gt())
            cnt[v] = cnt[v] + inc
    notsel = jnp.where(jnp.concatenate(cnt, axis=0) < n_sel, 0.0, 1.0).astype(BF16)
    place = (lax.broadcasted_iota(jnp.int32, (n_sb, LANES), 0)
             == lax.broadcasted_iota(jnp.int32, (n_sb, LANES), 1)).astype(BF16)
    ext = lax.dot_general(notsel, place, (((0,), (0,)), ((), ())), preferred_element_type=F32)
    ext = jnp.where(lane == LANES - 1, 1.0, ext).astype(BF16)
    qa = jnp.concatenate([q, jnp.concatenate([ext] * r, axis=0)], axis=1)

    chunk(z_w0, vals(vwt_ref, base, 2 * Q_BLOCK), near, True, win_state)
    chunk(z_w1, vals(vwt_ref, base - 2 * Q_BLOCK, 2 * Q_BLOCK), None, False, win_state)
    chunk(z_w2, vals(vwt_ref, base - 3 * Q_BLOCK, Q_BLOCK), tri, False, win_state)

    def scores(k):
        return _dot_nt(k, qa)

    far_keys = 4 * Q_BLOCK
    n_far = (i + 2) // 4
    far_off = lambda c: jnp.maximum(base - (c + 1) * far_keys, 0)
    far_scores = lambda c: scores(keys(ks_ref, far_off(c), far_keys))
    z_near = scores(keys(ks_ref, base, 2 * Q_BLOCK))
    s0_ref[...] = far_scores(0)
    chunk(z_near, vals(vst_ref, base, 2 * Q_BLOCK), near, True, sel_state)

    def sel_far_pair(t, carry):
        a = 2 * t
        s1_ref[...] = far_scores(a + 1)
        chunk(s0_ref, vals(vst_ref, far_off(a), far_keys), None, False, sel_state)
        s0_ref[...] = far_scores(a + 2)
        chunk(s1_ref, vals(vst_ref, far_off(a + 1), far_keys), None, False, sel_state)
        return carry

    lax.fori_loop(0, (n_far + 1) // 2, sel_far_pair, 0)

    gates = _sigmoid(gate_ref[0])
    inv_sel = 1.0 / sel_state[1][...]
    inv_win = 1.0 / win_state[1][...]
    for hh in range(r):
        cs = slice(hh * Q_BLOCK, (hh + 1) * Q_BLOCK)
        gate = lambda branch: gate_ref_row(gates, branch * C_HEADS + g * r + hh)
        out = (gate(0) * ocmp_ref[:, cs] + gate(1) * (sel_state[2][:, cs] * inv_sel[:, cs])
               + gate(2) * (win_state[2][:, cs] * inv_win[:, cs]))
        o_ref[0, :, hh * HEAD_DIM:(hh + 1) * HEAD_DIM] = out.T.astype(o_ref.dtype)


def gate_ref_row(gates, idx):
    sel = lax.broadcasted_iota(jnp.int32, gates.shape, 0) == idx
    return jnp.sum(jnp.where(sel, gates, 0.0), axis=0, keepdims=True)


def nsa_attention(qkv, cmp_kv, small, rel_bias, bsz, t):
    assert C_WINDOW == 4 * Q_BLOCK and NSA_PAD >= C_WINDOW
    nb = t // Q_BLOCK
    n_sb = t // SEL_BLOCK
    assert n_sb < LANES and n_sb % 8 == 0
    n_sel = min(N_SELECT, n_sb)
    n_chunk = t // CMP_STRIDE
    g_, r = C_KV_GROUPS, C_HEADS // C_KV_GROUPS
    hd = HEAD_DIM
    rows = r * Q_BLOCK
    ql = jnp.arange(Q_BLOCK)[:, None]
    kl = jnp.arange(2 * Q_BLOCK)[None, :]
    dist = ql + Q_BLOCK - kl
    far = rel_bias.astype(F32)[_t5_bucket(jnp.int32(2 * Q_BLOCK))]
    near = (_bias_table(rel_bias, dist, dist >= 0) - far[:, None, None]) * LOG2E
    near = near.reshape(g_, r, Q_BLOCK, 2 * Q_BLOCK).transpose(0, 3, 1, 2).reshape(g_, 2 * Q_BLOCK, rows)
    tok = np.arange(n_chunk)[None, :]
    blk = np.arange(n_sb)[:, None]
    per = SEL_BLOCK // CMP_STRIDE
    mimp = ((tok // per == blk).astype(np.float32) + ((tok + 1) // per == blk).astype(np.float32))
    mimp[:, n_chunk - 1] = 0.0
    blk_cols = np.where(np.arange(t)[:, None] // SEL_BLOCK == np.arange(LANES)[None, :], -MASK_BIG, 0.0)
    pad_cols = np.zeros((NSA_PAD, 2 * LANES), np.float32)
    pad_cols[:, -1] = -MASK_BIG

    def heads(col):
        return qkv[:, :, col:col + g_ * hd].reshape(bsz, t, g_, hd).transpose(0, 2, 1, 3)

    def with_mask(k, cols):
        cols = jnp.broadcast_to(jnp.asarray(cols, BF16), (bsz, g_, t, LANES))
        pad = jnp.broadcast_to(jnp.asarray(pad_cols, BF16), (bsz, g_, NSA_PAD, 2 * LANES))
        return jnp.concatenate([pad, jnp.concatenate([k, cols], axis=-1)], axis=2)

    def padded_t(v):
        return jnp.pad(v.transpose(0, 1, 3, 2), ((0, 0), (0, 0), (0, 0), (NSA_PAD, 0)))

    c0 = C_HEADS * hd
    ks = with_mask(heads(c0), blk_cols)
    vst = padded_t(heads(c0 + g_ * hd))
    kw = with_mask(heads(c0 + 2 * g_ * hd), np.zeros((t, LANES), np.float32))
    vwt = padded_t(heads(c0 + 3 * g_ * hd))
    vct = cmp_kv[1].transpose(0, 1, 3, 2)[None]
    n_gate = 3 * C_HEADS
    gates_t = small[:, :, :n_gate].transpose(0, 2, 1)
    tp = t + NSA_PAD
    kspec = pl.BlockSpec((1, 1, tp, 2 * LANES), lambda b, g, i: (b, g, 0, 0))
    vspec = pl.BlockSpec((1, 1, hd, tp), lambda b, g, i: (b, g, 0, 0))
    return pl.pallas_call(
        functools.partial(_nsa_kernel, n_sel),
        grid=(bsz, g_, nb),
        in_specs=[pl.BlockSpec((1, Q_BLOCK, r * hd), lambda b, g, i: (b, i, g)),
                  pl.BlockSpec((1, 1, 1, n_chunk, hd), lambda b, g, i: (0, b, g, 0, 0)),
                  pl.BlockSpec((1, 1, 1, hd, n_chunk), lambda b, g, i: (0, b, g, 0, 0)),
                  kspec, vspec, kspec, vspec,
                  pl.BlockSpec((1, n_gate, Q_BLOCK), lambda b, g, i: (b, 0, i)),
                  pl.BlockSpec((1, 2 * Q_BLOCK, rows), lambda b, g, i: (g, 0, 0)),
                  pl.BlockSpec((n_sb, n_chunk), lambda b, g, i: (0, 0))],
        out_specs=pl.BlockSpec((1, Q_BLOCK, r * hd), lambda b, g, i: (b, i, g)),
        out_shape=jax.ShapeDtypeStruct((bsz, t, C_HEADS * hd), BF16),
        scratch_shapes=[pltpu.VMEM((1, rows), F32), pltpu.VMEM((1, rows), F32), pltpu.VMEM((hd, rows), F32),
                        pltpu.VMEM((1, rows), F32), pltpu.VMEM((1, rows), F32), pltpu.VMEM((hd, rows), F32),
                        pltpu.VMEM((hd, rows), F32),
                        pltpu.VMEM((4 * Q_BLOCK, rows), F32), pltpu.VMEM((4 * Q_BLOCK, rows), F32)],
        compiler_params=_cp(("parallel", "parallel", "arbitrary")),
        name="nsa_attention",
    )(qkv, cmp_kv, vct, ks, vst, kw, vwt, gates_t, near, jnp.asarray(mimp, BF16))


def _bmm(a, b):
    return jnp.einsum('cij,cjk->cik', a, b, preferred_element_type=F32)


def _bmm_nt(a, b):
    return jnp.einsum('cik,cjk->cij', a, b, preferred_element_type=F32)


def _bmm_f32(a, b):
    return _bmm(a.astype(BF16), b.astype(BF16))


def _gdn_prep_kernel(al_ref, dt_ref, q_ref, qp_ref, k_ref, kp_ref, v_ref, vp_ref, wq_ref, wk_ref, wv_ref,
                     beta_ref, arow_ref, u_ref, w_ref, qg_ref, kd_ref, attn_ref, eg_ref):
    h = pl.program_id(1)
    rb = pl.program_id(2)
    rows = q_ref.shape[1]
    c = GDN_CHUNK
    nc = rows // c
    a_log = al_ref[h]
    dt_b = dt_ref[h]

    def conv_silu(x_ref, xp_ref, w_ref):
        prev = jnp.where(rb == 0, 0.0, xp_ref[0])
        x = jnp.concatenate([prev, x_ref[0]], axis=0)
        w = w_ref[...]
        n0 = prev.shape[0] - (D_CONV - 1)
        y = sum(w[j:j + 1] * x[n0 + j:n0 + j + rows] for j in range(D_CONV))
        return _silu(y)

    def l2(x):
        return x * lax.rsqrt(jnp.sum(x * x, axis=-1, keepdims=True) + EPS)

    q = (l2(conv_silu(q_ref, qp_ref, wq_ref)) * SCALE).reshape(nc, c, HEAD_DIM)
    k = l2(conv_silu(k_ref, kp_ref, wk_ref)).reshape(nc, c, HEAD_DIM)
    v = conv_silu(v_ref, vp_ref, wv_ref).reshape(nc, c, HEAD_DIM)

    beta = _sigmoid(beta_ref[0, 0]).reshape(nc, c, HEAD_DIM)
    g_row = (-jnp.exp(a_log) * _softplus(arow_ref[0, 0] + dt_b)).reshape(nc, c, c)

    ii = lax.broadcasted_iota(jnp.int32, (nc, c, c), 1)
    jj = lax.broadcasted_iota(jnp.int32, (nc, c, c), 2)
    triu = (ii <= jj).astype(BF16)
    gam_row = sum(_bmm(p, triu) for p in _split_bf16(g_row, 3))
    gam = jnp.sum(jnp.where(ii == jj, gam_row, 0.0), axis=2, keepdims=True)
    causal = jj <= ii
    decay = jnp.where(causal, jnp.exp(jnp.where(causal, gam - gam_row, 0.0)), 0.0)

    kb = k * beta
    kbf = k.astype(BF16)
    lmat = jnp.where(jj < ii, _bmm_nt(kb.astype(BF16), kbf) * decay, 0.0)
    same = lambda s: (ii // s) == (jj // s)
    eye = (ii == jj).astype(F32)
    pw = jnp.where(same(INV_BLOCK), lmat, 0.0)
    inv = eye - pw
    for _ in range(int(math.log2(INV_BLOCK)) - 1):
        pw = _bmm_f32(pw, pw)
        inv = inv + _bmm_f32(inv, pw)
    s = INV_BLOCK
    while s < c:
        joins = jnp.where(same(2 * s) & jnp.logical_not(same(s)), lmat, 0.0)
        inv = inv - _bmm_f32(_bmm_f32(inv, joins), inv)
        s *= 2
    u = _bmm_f32(inv, v * beta)
    egam = jnp.exp(gam)
    w = _bmm_f32(inv, kb * egam)
    attn = _bmm_nt(q.astype(BF16), kbf) * decay
    g_last = gam[:, c - 1:c, :]

    u_ref[0, 0] = u.reshape(rows, HEAD_DIM)
    w_ref[0, 0] = w.reshape(rows, HEAD_DIM).astype(BF16)
    qg_ref[0, 0] = (q * egam).reshape(rows, HEAD_DIM).astype(BF16)
    kd_ref[0, 0] = (k * jnp.exp(g_last - gam)).reshape(rows, HEAD_DIM).astype(BF16)
    attn_ref[0, 0] = attn.reshape(rows, c).astype(BF16)
    eg_ref[0, 0] = jnp.broadcast_to(jnp.exp(g_last), (nc, 8, HEAD_DIM))


def _gdn_scan_kernel(u_ref, w_ref, qg_ref, kd_ref, attn_ref, eg_ref, z_ref, ng_ref, o_ref, state_ref):
    c = GDN_CHUNK
    h = u_ref.shape[1]
    n = u_ref.shape[2] // c
    ng = ng_ref[...]

    @pl.when(pl.program_id(1) == 0)
    def _():
        state_ref[...] = jnp.zeros_like(state_ref)

    def step(ci, carry):
        off = pl.multiple_of(ci * c, c)
        sl = pl.ds(off, c)
        heads = range(h)
        states = [state_ref[hh] for hh in heads]
        sbs = [s.astype(BF16) for s in states]
        ws = [_dot(w_ref[0, hh, sl, :], sbs[hh]) for hh in heads]
        qs = [_dot(qg_ref[0, hh, sl, :], sbs[hh]) for hh in heads]
        vbs = [(u_ref[0, hh, sl, :] - ws[hh]).astype(BF16) for hh in heads]
        os = [qs[hh] + _dot(attn_ref[0, hh, sl, :], vbs[hh]) for hh in heads]
        upd = [lax.dot_general(kd_ref[0, hh, sl, :], vbs[hh], (((0,), (0,)), ((), ())), preferred_element_type=F32)
               for hh in heads]
        for hh in heads:
            cols = slice(hh * HEAD_DIM, (hh + 1) * HEAD_DIM)
            eg = eg_ref[0, hh, ci]
            decayed = (states[hh].reshape(HEAD_DIM // 8, 8, HEAD_DIM) * eg[None]).reshape(HEAD_DIM, HEAD_DIM)
            state_ref[hh] = decayed + upd[hh]
            o_ref[0, sl, cols] = (_rms(os[hh], ng) * _silu(z_ref[0, sl, cols])).astype(o_ref.dtype)
        return carry

    lax.fori_loop(0, n, step, 0)


def gated_deltanet(wide, small, conv_w, a_log, dt_bias, norm_g, bsz, t, rows=1024):
    c = GDN_CHUNK
    h = D_HEADS
    nblk = t // rows
    qcol = 0
    kcol, vcol, zcol = qcol + h, qcol + 2 * h, qcol + 3 * h
    gate_col = 3 * C_HEADS
    beta_t = small[:, :, gate_col:gate_col + h].transpose(0, 2, 1)
    a_t = small[:, :, gate_col + h:gate_col + 2 * h].transpose(0, 2, 1)
    beta_b = jnp.broadcast_to(beta_t[..., None], (bsz, h, t, HEAD_DIM))
    a_row = jnp.broadcast_to(a_t.reshape(bsz, h, t // c, 1, c), (bsz, h, t // c, c, c)).reshape(bsz, h, t, c)
    hb = rows // 8
    main = lambda col: pl.BlockSpec((1, rows, HEAD_DIM), lambda b, hh, r: (b, r, col + hh))
    halo = lambda col: pl.BlockSpec((1, 8, HEAD_DIM), lambda b, hh, r: (b, jnp.maximum(r * hb - 1, 0), col + hh))
    cw = lambda off: pl.BlockSpec((D_CONV, HEAD_DIM), lambda b, hh, r: (0, off * h + hh))
    per_tok = lambda width: pl.BlockSpec((1, 1, rows, width), lambda b, hh, r: (b, hh, r, 0))
    smem = pl.BlockSpec(memory_space=pltpu.SMEM)
    shp = lambda width, dt: jax.ShapeDtypeStruct((bsz, h, t, width), dt)
    u, w, qg, kd, attn, eg = pl.pallas_call(
        _gdn_prep_kernel,
        grid=(bsz, h, nblk),
        in_specs=[smem, smem, main(qcol), halo(qcol), main(kcol), halo(kcol), main(vcol), halo(vcol),
                  cw(0), cw(1), cw(2), per_tok(HEAD_DIM), per_tok(c)],
        out_specs=[per_tok(HEAD_DIM), per_tok(HEAD_DIM), per_tok(HEAD_DIM), per_tok(HEAD_DIM), per_tok(c),
                   pl.BlockSpec((1, 1, rows // c, 8, HEAD_DIM), lambda b, hh, r: (b, hh, r, 0, 0))],
        out_shape=[shp(HEAD_DIM, F32), shp(HEAD_DIM, BF16), shp(HEAD_DIM, BF16), shp(HEAD_DIM, BF16), shp(c, BF16),
                   jax.ShapeDtypeStruct((bsz, h, t // c, 8, HEAD_DIM), F32)],
        compiler_params=_cp(("parallel", "parallel", "parallel")),
        name="gdn_prep",
    )(a_log.astype(F32), dt_bias.astype(F32), wide, wide, wide, wide, wide, wide,
      conv_w, conv_w, conv_w, beta_b, a_row)
    blk = lambda width: pl.BlockSpec((1, h, rows, width), lambda b, r: (b, 0, r, 0))
    assert zcol % h == 0
    return pl.pallas_call(
        _gdn_scan_kernel,
        grid=(bsz, nblk),
        in_specs=[blk(HEAD_DIM), blk(HEAD_DIM), blk(HEAD_DIM), blk(HEAD_DIM), blk(c),
                  pl.BlockSpec((1, h, rows // c, 8, HEAD_DIM), lambda b, r: (b, 0, r, 0, 0)),
                  pl.BlockSpec((1, rows, h * HEAD_DIM), lambda b, r: (b, r, zcol // h)),
                  pl.BlockSpec((1, HEAD_DIM), lambda b, r: (0, 0))],
        out_specs=pl.BlockSpec((1, rows, h * HEAD_DIM), lambda b, r: (b, r, 0)),
        out_shape=jax.ShapeDtypeStruct((bsz, t, h * HEAD_DIM), BF16),
        scratch_shapes=[pltpu.VMEM((h, HEAD_DIM, HEAD_DIM), F32)],
        compiler_params=_cp(("parallel", "arbitrary")),
        name="gdn_scan",
    )(u, w, qg, kd, attn, eg, wide, norm_g.astype(F32).reshape(1, HEAD_DIM))


def _pad_cols(w, width):
    return jnp.pad(w, ((0, 0), (0, width - w.shape[1])))


def even_mixer(h, norm_g, w_in, b_forget, sinks, w_out, rel_bias, bsz, t):
    n_big = (A_HEADS + 2 * A_KV_HEADS + 3 * B_HEADS) * HEAD_DIM
    big, small = norm_matmul(h, norm_g, w_in[:, :n_big].astype(BF16), BF16,
                             w_side=_pad_cols(w_in[:, n_big:], LANES).astype(BF16), tn=768)
    big = big.reshape(bsz, t, n_big)
    o_a = swa_attention(big, sinks, rel_bias, bsz, t)
    f_t = small.reshape(bsz, t, LANES)[:, :, :B_HEADS].transpose(0, 2, 1)
    o_b = fox_attention(big, forget_cumsum(f_t, b_forget), bsz, t)
    return out_proj(o_a.reshape(bsz * t, -1), o_b.reshape(bsz * t, -1), w_out.astype(BF16), h)


def odd_mixer(h, norm_g, w_in, cmp_pos, cmp_w1, cmp_w2, conv_w, a_log, dt_bias, gdn_norm, w_out, rel_bias, bsz, t):
    hd = HEAD_DIM
    g = C_KV_GROUPS
    o_q = 0
    o_kcmp = C_HEADS * hd
    o_ksel = o_kcmp + 2 * g * hd
    o_gates = o_ksel + 4 * g * hd
    o_qd = o_gates + 3 * C_HEADS
    o_beta = o_qd + 3 * D_HEADS * hd
    o_z = o_beta + 2 * D_HEADS
    w_bf = jnp.concatenate([w_in[:, o_q:o_kcmp], w_in[:, o_ksel:o_gates]], axis=1)
    w_f32 = jnp.concatenate([w_in[:, o_qd:o_beta], w_in[:, o_z:], w_in[:, o_kcmp:o_ksel]], axis=1)
    w_small = _pad_cols(jnp.concatenate([w_in[:, o_gates:o_qd], w_in[:, o_beta:o_z]], axis=1), LANES)
    qkv = norm_matmul(h, norm_g, w_bf.astype(BF16), BF16).reshape(bsz, t, -1)
    wide, small = norm_matmul(h, norm_g, w_f32.astype(BF16), F32, w_side=w_small.astype(BF16), tn=768)
    wide = wide.reshape(bsz, t, -1)
    small = small.reshape(bsz, t, LANES)

    n_chunk = t // CMP_STRIDE
    cmp_kv = compress_tokens(wide, 4 * D_HEADS, cmp_pos, cmp_w1.astype(BF16), cmp_w2.astype(BF16), bsz, t)
    o_c = nsa_attention(qkv, cmp_kv, small, rel_bias, bsz, t)
    o_d = gated_deltanet(wide, small, conv_w, a_log, dt_bias, gdn_norm, bsz, t)
    return out_proj(o_c.reshape(bsz * t, -1), o_d.reshape(bsz * t, -1), w_out.astype(BF16), h)


def kernel(x, rel_bias, norm_mix, norm_ffn, norm_final, ev_w_in, ev_b_forget, ev_sinks, ev_w_out, od_w_in,
           od_cmp_pos, od_cmp_w1, od_cmp_w2, od_conv_w, od_a_log, od_dt_bias, od_gdn_norm, od_w_out, ffn_w_up,
           ffn_conv_w, ffn_conv_b, ffn_w_down):
    bsz, t, d = x.shape
    h = x.reshape(bsz * t, d)
    depth = norm_mix.shape[0]
    for layer in range(depth):
        j = layer // 2
        if layer % 2 == 0:
            h = even_mixer(h, norm_mix[layer], ev_w_in[j], ev_b_forget[j], ev_sinks[j], ev_w_out[j], rel_bias,
                           bsz, t)
        else:
            h = odd_mixer(h, norm_mix[layer], od_w_in[j], od_cmp_pos[j], od_cmp_w1[j], od_cmp_w2[j], od_conv_w[j],
                          od_a_log[j], od_dt_bias[j], od_gdn_norm[j], od_w_out[j], rel_bias, bsz, t)
        h = conv_ffn(h, norm_ffn[layer], ffn_w_up[layer].astype(BF16), ffn_conv_w[layer], ffn_conv_b[layer],
                     ffn_w_down[layer].astype(BF16), t, final_g=norm_final if layer == depth - 1 else None)
    return h.reshape(bsz, t, d)
```
